```python
import math
import jax, jax.numpy as jnp
from jax import lax
import numpy as np

D_MODEL = 1024
BATCH = 8
SEQ = 2048
DEPTH = 1
DEC_BATCH = 32
DEC_SEQ = 8
PAST_LEN = 8192
PAGE_SIZE = 128

D_MIX = D_MODEL
D_ATT = D_MIX // 2
D_CONV = D_MIX - D_ATT
HEAD_DIM = 64
N_HEADS = D_ATT // HEAD_DIM
CONV_WIDTH = 3
DILATED_PAIRS = ((128, 1), (512, 4), (2048, 16))
MAX_WINDOW = max(w for w, _ in DILATED_PAIRS)
Q_BLOCK = 128
LN_EPS = 1e-5
DEEPNORM_ALPHA = (2.0 * DEPTH) ** 0.25
DEEPNORM_BETA = (8.0 * DEPTH) ** -0.25
SPLIT_SIZES = (D_ATT, D_ATT, D_ATT, D_ATT, D_CONV, D_CONV, D_CONV, D_CONV)
D_IN = sum(SPLIT_SIZES)

kernel_name = "hymba_dilated_swa_shortconv_step"


def _dilated_branch(q, k_all, v_all, window, dilation, offset):
    b, t, h, dh = q.shape
    n_keys = window // dilation + 1
    qb = math.gcd(t, Q_BLOCK)
    n_blk = t // qb
    dist = dilation * jnp.arange(n_keys, dtype=jnp.int32)
    scale = dh ** -0.5

    def block(start):
        q_blk = lax.dynamic_slice_in_dim(q, start, qb, axis=1)
        q_idx = offset + start + jnp.arange(qb, dtype=jnp.int32)
        idx = q_idx[:, None] - dist[None, :]
        valid = idx >= 0
        idx = jnp.maximum(idx, 0)
        k_g = jnp.take(k_all, idx, axis=1)
        v_g = jnp.take(v_all, idx, axis=1)
        s = jnp.einsum('bqhd,bqjhd->bhqj', q_blk, k_g).astype(jnp.float32) * scale
        s = jnp.where(valid[None, None], s, -jnp.inf)
        lse = jax.nn.logsumexp(s, axis=-1)
        p = jnp.exp(s - lse[..., None]).astype(v_all.dtype)
        o = jnp.einsum('bhqj,bqjhd->bqhd', p, v_g)
        return o, lse

    o, lse = lax.map(block, jnp.arange(n_blk, dtype=jnp.int32) * qb)
    o = jnp.moveaxis(o, 0, 1).reshape(b, t, h, dh)
    lse = jnp.transpose(lse, (1, 0, 3, 2)).reshape(b, t, h)
    return o, lse


def _dilated_attention(q, k_all, v_all, offset):
    outs, lses = [], []
    for window, dilation in DILATED_PAIRS:
        o, lse = _dilated_branch(q, k_all, v_all, window, dilation, offset)
        outs.append(o)
        lses.append(lse)
    wts = jax.nn.softmax(jnp.stack(lses, axis=0), axis=0)
    return jnp.einsum('rbth,rbthd->bthd', wts.astype(q.dtype), jnp.stack(outs, axis=0))


def _layer_norm(x, g, b):
    xf = x.astype(jnp.float32)
    mu = jnp.mean(xf, axis=-1, keepdims=True)
    var = jnp.mean(jnp.square(xf - mu), axis=-1, keepdims=True)
    y = (xf - mu) * lax.rsqrt(var + LN_EPS) * g.astype(jnp.float32) + b.astype(jnp.float32)
    return y.astype(x.dtype)


def _hybrid_layer(x, k_past, v_past, conv_past, w_in, conv_w, w_out, ln_g, ln_b):
    bsz, t, _ = x.shape
    proj = jnp.einsum('btd,de->bte', x, w_in)
    q, k, v, g_att, gb, gc, hh, g_conv = jnp.split(
        proj, [int(s) for s in np.cumsum(SPLIT_SIZES)[:-1]], axis=-1)
    q = q.reshape(bsz, t, N_HEADS, HEAD_DIM)
    k = k.reshape(bsz, t, N_HEADS, HEAD_DIM)
    v = v.reshape(bsz, t, N_HEADS, HEAD_DIM)

    if k_past is None:
        k_all, v_all, offset = k, v, 0
    else:
        k_all = jnp.concatenate([k_past, k], axis=1)
        v_all = jnp.concatenate([v_past, v], axis=1)
        offset = k_past.shape[1]
    o_att = _dilated_attention(q, k_all, v_all, offset).reshape(bsz, t, D_ATT)
    att = o_att * jax.nn.silu(g_att)

    u = gc * hh
    u_pad = jnp.concatenate([conv_past.astype(u.dtype), u], axis=1)
    y_conv = conv_w[0] * u_pad[:, 0:t]
    for j in range(1, CONV_WIDTH):
        y_conv = y_conv + conv_w[j] * u_pad[:, j:j + t]
    z = gb * y_conv * jax.nn.silu(g_conv)

    mix = jnp.concatenate([att, z], axis=-1)
    out = jnp.einsum('bte,ed->btd', mix, w_out)
    y = _layer_norm(DEEPNORM_ALPHA * x + out, ln_g, ln_b)

    n_all = k_all.shape[1]
    keep = min(MAX_WINDOW, n_all)
    new_k = k_all[:, n_all - keep:]
    new_v = v_all[:, n_all - keep:]
    new_conv = u_pad[:, t:]
    return y, new_k, new_v, new_conv


def setup_inputs(seed: int = 0) -> dict:
    key = jax.random.key(seed)
    ks = jax.random.split(key, 10)
    win_buf = min(MAX_WINDOW, PAST_LEN)
    x_prompt = jax.random.normal(ks[0], (BATCH, SEQ, D_MODEL), jnp.float32)
    x_sample = jax.random.normal(ks[1], (DEC_BATCH, DEC_SEQ, D_MODEL), jnp.float32)
    cache_k = jax.random.normal(ks[2], (DEC_BATCH, win_buf, N_HEADS, HEAD_DIM), jnp.float32)
    cache_v = jax.random.normal(ks[3], (DEC_BATCH, win_buf, N_HEADS, HEAD_DIM), jnp.float32)
    state_conv = jax.random.normal(ks[4], (DEC_BATCH, CONV_WIDTH - 1, D_CONV), jnp.float32)
    w_in = jax.random.normal(ks[5], (D_MODEL, D_IN), jnp.float32) * D_MODEL ** -0.5
    conv_w = jax.random.normal(ks[6], (CONV_WIDTH, D_CONV), jnp.float32) * CONV_WIDTH ** -0.5
    w_out = jax.random.normal(ks[7], (D_MIX, D_MODEL), jnp.float32) * (D_MIX ** -0.5) * DEEPNORM_BETA
    ln_g = 1.0 + 0.1 * jax.random.normal(ks[8], (D_MODEL,), jnp.float32)
    ln_b = 0.01 * jax.random.normal(ks[9], (D_MODEL,), jnp.float32)
    return {"x_prompt": x_prompt, "x_sample": x_sample, "cache_k": cache_k,
            "cache_v": cache_v, "state_conv": state_conv, "w_in": w_in,
            "conv_w": conv_w, "w_out": w_out, "ln_g": ln_g, "ln_b": ln_b}


def reference(x_prompt, x_sample, cache_k, cache_v, state_conv, w_in, conv_w, w_out, ln_g, ln_b):
    y_prompt = x_prompt
    y_sample = x_sample
    conv0 = jnp.zeros((x_prompt.shape[0], CONV_WIDTH - 1, D_CONV), x_prompt.dtype)
    for _layer in range(DEPTH):
        y_prompt, new_k_prompt, new_v_prompt, new_conv_prompt = _hybrid_layer(
            y_prompt, None, None, conv0, w_in, conv_w, w_out, ln_g, ln_b)
        y_sample, new_k_sample, new_v_sample, new_conv_sample = _hybrid_layer(
            y_sample, cache_k, cache_v, state_conv, w_in, conv_w, w_out, ln_g, ln_b)
    return (y_prompt, y_sample, new_k_prompt, new_v_prompt, new_conv_prompt,
            new_k_sample, new_v_sample, new_conv_sample)
```

```python
import functools
import math

import numpy as np
import jax
import jax.numpy as jnp
from jax import lax
from jax.experimental import pallas as pl
from jax.experimental.pallas import tpu as pltpu

DILATED_PAIRS = ((128, 1), (512, 4), (2048, 16))
LN_EPS = 1e-5
DEPTH = 1
DEEPNORM_ALPHA = (2.0 * DEPTH) ** 0.25

LANES = 128
SUBLANES = 8
Q_BLOCK = 128
NEG_BIG = -1e30

MIB = 1024 * 1024


def _silu(x):
    return x * (1.0 / (1.0 + jnp.exp(-x)))


def _inproj_kernel(x_ref, w_ref, cw_ref, hist_ref,
                   q_ref, kb_ref, vb_ref, kt_ref, vt_ref, sg_ref, z_ref, nc_ref,
                   u_scr, *, tm, d_att, d_conv, n_pairs, q_scale):
    t = pl.program_id(1)
    x = x_ref[0].astype(jnp.bfloat16)

    def proj(c0, width):
        return jnp.dot(x, w_ref[:, c0:c0 + width], preferred_element_type=jnp.float32)

    q = proj(0, d_att) * q_scale
    k = proj(d_att, d_att)
    v = proj(2 * d_att, d_att)
    g_att = proj(3 * d_att, d_att)
    kt_ref[0] = k.T
    vt_ref[0] = v.T
    qb = q.astype(jnp.bfloat16)
    kb = k.astype(jnp.bfloat16)
    vb = v.astype(jnp.bfloat16)
    sg = _silu(g_att).astype(jnp.bfloat16)
    for p in range(n_pairs):
        sl = slice(p * LANES, (p + 1) * LANES)
        q_ref[0, p] = qb[:, sl]
        kb_ref[0, p] = kb[:, sl]
        vb_ref[0, p] = vb[:, sl]
        sg_ref[0, p] = sg[:, sl]

    c0 = 4 * d_att
    gb = proj(c0, d_conv)
    gc = proj(c0 + d_conv, d_conv)
    hh = proj(c0 + 2 * d_conv, d_conv)
    g_conv = proj(c0 + 3 * d_conv, d_conv)
    u = gc * hh

    @pl.when(t == 0)
    def _():
        u_scr[SUBLANES - 2:SUBLANES, :] = hist_ref[0]

    u_scr[SUBLANES:SUBLANES + tm, :] = u
    um2 = u_scr[SUBLANES - 2:SUBLANES - 2 + tm, :]
    um1 = u_scr[SUBLANES - 1:SUBLANES - 1 + tm, :]
    y_conv = cw_ref[0:1, :] * um2 + cw_ref[1:2, :] * um1 + cw_ref[2:3, :] * u
    z_ref[0] = (gb * y_conv * _silu(g_conv)).astype(jnp.bfloat16)
    tail = u[tm - 2:tm, :]
    u_scr[SUBLANES - 2:SUBLANES, :] = tail
    nc_ref[0] = tail


def _inproj(x, w_in_bf, conv_w, hist, *, tm, d_att, d_conv, q_scale):
    b, t, d = x.shape
    n_pairs = d_att // LANES
    n_out = w_in_bf.shape[1]
    kern = functools.partial(_inproj_kernel, tm=tm, d_att=d_att, d_conv=d_conv,
                             n_pairs=n_pairs, q_scale=q_scale)
    pm = jax.ShapeDtypeStruct((b, n_pairs, t, LANES), jnp.bfloat16)
    pm_spec = pl.BlockSpec((1, n_pairs, tm, LANES), lambda i, j: (i, 0, j, 0))
    row_spec = lambda c: pl.BlockSpec((1, tm, c), lambda i, j: (i, j, 0))
    col_spec = pl.BlockSpec((1, d_att, tm), lambda i, j: (i, 0, j))
    vmem = (2 * d * n_out * 2 + 2 * tm * d * 4 + 2 * tm * (4 * d_att * 2 + 2 * d_att * 4 + d_conv * 2)
            + (tm + SUBLANES) * d_conv * 4 + 12 * tm * d_att * 4)
    return pl.pallas_call(
        kern,
        grid=(b, t // tm),
        in_specs=[
            row_spec(d),
            pl.BlockSpec((d, n_out), lambda i, j: (0, 0)),
            pl.BlockSpec(conv_w.shape, lambda i, j: (0, 0)),
            pl.BlockSpec((1, 2, d_conv), lambda i, j: (i, 0, 0)),
        ],
        out_specs=[pm_spec, pm_spec, pm_spec, col_spec, col_spec, pm_spec,
                   row_spec(d_conv), pl.BlockSpec((1, 2, d_conv), lambda i, j: (i, 0, 0))],
        out_shape=[pm, pm, pm,
                   jax.ShapeDtypeStruct((b, d_att, t), jnp.float32),
                   jax.ShapeDtypeStruct((b, d_att, t), jnp.float32),
                   pm,
                   jax.ShapeDtypeStruct((b, t, d_conv), jnp.bfloat16),
                   jax.ShapeDtypeStruct((b, 2, d_conv), jnp.float32)],
        scratch_shapes=[pltpu.VMEM((tm + SUBLANES, d_conv), jnp.float32)],
        compiler_params=pltpu.CompilerParams(
            dimension_semantics=("parallel", "arbitrary"), vmem_limit_bytes=int(vmem)),
        name="prompt_inproj",
    )(x, w_in_bf, conv_w, hist)


def _attn_block(q, kwin, vwin, bias, lo):
    zero = jnp.zeros_like(q)
    q2 = jnp.concatenate([jnp.where(lo, q, zero), jnp.where(lo, zero, q)], axis=0)
    s = lax.dot_general(q2, kwin, (((1,), (1,)), ((), ())), preferred_element_type=jnp.float32)
    ps, ls, lses = [], [], []
    for h in range(2):
        sh = s[h * Q_BLOCK:(h + 1) * Q_BLOCK] + bias
        m = jnp.max(sh, axis=-1, keepdims=True)
        p = jnp.exp(sh - m)
        l = jnp.sum(p, axis=-1, keepdims=True)
        ps.append(p.astype(jnp.bfloat16))
        ls.append(l)
        lses.append(m + jnp.log(l))
    pv = jnp.dot(jnp.concatenate(ps, axis=0), vwin, preferred_element_type=jnp.float32)
    o = jnp.where(lo, pv[:Q_BLOCK] * (1.0 / ls[0]), pv[Q_BLOCK:] * (1.0 / ls[1]))
    lse = jnp.where(lo, lses[0], lses[1])
    return o, lse


def _prompt_attn_kernel(*refs, seq, dils, head_dim):
    n_br = len(dils)
    qkv = [refs[3 * i:3 * i + 3] for i in range(n_br)]
    sg_ref = refs[3 * n_br]
    att_ref = refs[3 * n_br + 1]
    scr = refs[3 * n_br + 2:]
    o_scr = scr[0::2]
    l_scr = scr[1::2]

    lane = lax.broadcasted_iota(jnp.int32, (Q_BLOCK, LANES), 1)
    lo = lane < head_dim
    row = lax.broadcasted_iota(jnp.int32, (Q_BLOCK, 2 * Q_BLOCK), 0)
    col = lax.broadcasted_iota(jnp.int32, (Q_BLOCK, 2 * Q_BLOCK), 1)
    band = jnp.where((col >= row) & (col <= row + Q_BLOCK), 0.0, NEG_BIG).astype(jnp.float32)
    row1 = lax.broadcasted_iota(jnp.int32, (Q_BLOCK, Q_BLOCK), 0)
    col1 = lax.broadcasted_iota(jnp.int32, (Q_BLOCK, Q_BLOCK), 1)
    causal = jnp.where(col1 <= row1, 0.0, NEG_BIG).astype(jnp.float32)

    for (q_ref, k_ref, v_ref), d, o_ref, l_ref in zip(qkv, dils, o_scr, l_scr):
        n_blk = seq // d // Q_BLOCK
        for r in range(d):
            lanes = slice(r * LANES, (r + 1) * LANES)

            def store(row0, o, lse, d=d, r=r, o_ref=o_ref, l_ref=l_ref):
                if d == 1:
                    idx = pl.ds(row0, Q_BLOCK)
                else:
                    idx = pl.ds(r + d * row0, Q_BLOCK, stride=d)
                o_ref[idx, :] = o
                l_ref[idx, :] = lse

            q0 = q_ref[0:Q_BLOCK, lanes]
            o, lse = _attn_block(q0, k_ref[0:Q_BLOCK, lanes], v_ref[0:Q_BLOCK, lanes], causal, lo)
            store(0, o, lse)

            if n_blk > 1:
                def body(qb, carry, q_ref=q_ref, k_ref=k_ref, v_ref=v_ref, lanes=lanes, store=store):
                    row0 = pl.multiple_of(qb * Q_BLOCK, Q_BLOCK)
                    k0 = pl.multiple_of(row0 - Q_BLOCK, Q_BLOCK)
                    q = q_ref[pl.ds(row0, Q_BLOCK), lanes]
                    kwin = k_ref[pl.ds(k0, 2 * Q_BLOCK), lanes]
                    vwin = v_ref[pl.ds(k0, 2 * Q_BLOCK), lanes]
                    o, lse = _attn_block(q, kwin, vwin, band, lo)
                    store(row0, o, lse)
                    return carry

                lax.fori_loop(1, n_blk, body, 0)

    def merge(i, carry):
        rows = pl.ds(pl.multiple_of(i * Q_BLOCK, Q_BLOCK), Q_BLOCK)
        lses = [l_ref[rows, :] for l_ref in l_scr]
        top = functools.reduce(jnp.maximum, lses)
        es = [jnp.exp(x - top) for x in lses]
        num = functools.reduce(lambda a, b: a + b, [e * o_ref[rows, :] for e, o_ref in zip(es, o_scr)])
        den = functools.reduce(lambda a, b: a + b, es)
        att = num * (1.0 / den) * sg_ref[rows, :].astype(jnp.float32)
        att_ref[rows, :] = att.astype(att_ref.dtype)
        return carry

    lax.fori_loop(0, seq // Q_BLOCK, merge, 0)


def _prompt_attn(q_pm, k_pm, v_pm, sg_pm, *, head_dim):
    b, n_pairs, t, _ = q_pm.shape
    dils = tuple(d for _, d in DILATED_PAIRS)
    for w, d in DILATED_PAIRS:
        assert w == d * Q_BLOCK and t % (d * Q_BLOCK) == 0
    ins, specs = [], []
    for d in dils:
        for a in (q_pm, k_pm, v_pm):
            ins.append(a.reshape(b, n_pairs, t // d, d * LANES))
            specs.append(pl.BlockSpec((None, None, t // d, d * LANES), lambda i, j: (i, j, 0, 0)))
    pair_spec = pl.BlockSpec((None, None, t, LANES), lambda i, j: (i, j, 0, 0))
    kern = functools.partial(_prompt_attn_kernel, seq=t, dils=dils, head_dim=head_dim)
    vmem = 2 * (3 * len(dils) + 2) * t * LANES * 2 + 2 * len(dils) * t * LANES * 4 + 16 * MIB
    return pl.pallas_call(
        kern,
        grid=(b, n_pairs),
        in_specs=specs + [pair_spec],
        out_specs=pair_spec,
        out_shape=jax.ShapeDtypeStruct((b, n_pairs, t, LANES), jnp.bfloat16),
        scratch_shapes=[pltpu.VMEM((t, LANES), jnp.float32) for _ in range(2 * len(dils))],
        compiler_params=pltpu.CompilerParams(
            dimension_semantics=("parallel", "parallel"), vmem_limit_bytes=int(vmem)),
        name="prompt_attention",
    )(*ins, sg_pm)


def _outproj_kernel(att_ref, z_ref, x_ref, w_ref, g_ref, b_ref, y_ref, *, n_pairs):
    parts = [att_ref[0, p].astype(jnp.bfloat16) for p in range(n_pairs)]
    mix = jnp.concatenate(parts + [z_ref[0].astype(jnp.bfloat16)], axis=-1)
    out = jnp.dot(mix, w_ref[...], preferred_element_type=jnp.float32)
    h = DEEPNORM_ALPHA * x_ref[0] + out
    mu = jnp.mean(h, axis=-1, keepdims=True)
    c = h - mu
    var = jnp.mean(c * c, axis=-1, keepdims=True)
    y_ref[0] = c * lax.rsqrt(var + LN_EPS) * g_ref[...] + b_ref[...]


def _outproj(att_pm, z, x, w_out_bf, ln_g, ln_b, *, tm):
    b, t, d = x.shape
    n_pairs = att_pm.shape[1]
    d_conv = z.shape[-1]
    d_mix = w_out_bf.shape[0]
    vmem = (2 * d_mix * d * 2 + 4 * tm * d * 4 + 2 * tm * n_pairs * LANES * att_pm.dtype.itemsize
            + 2 * tm * d_conv * 2 + 6 * tm * d * 4)
    return pl.pallas_call(
        functools.partial(_outproj_kernel, n_pairs=n_pairs),
        grid=(b, t // tm),
        in_specs=[
            pl.BlockSpec((1, n_pairs, tm, LANES), lambda i, j: (i, 0, j, 0)),
            pl.BlockSpec((1, tm, d_conv), lambda i, j: (i, j, 0)),
            pl.BlockSpec((1, tm, d), lambda i, j: (i, j, 0)),
            pl.BlockSpec((d_mix, d), lambda i, j: (0, 0)),
            pl.BlockSpec((1, d), lambda i, j: (0, 0)),
            pl.BlockSpec((1, d), lambda i, j: (0, 0)),
        ],
        out_specs=pl.BlockSpec((1, tm, d), lambda i, j: (i, j, 0)),
        out_shape=jax.ShapeDtypeStruct((b, t, d), jnp.float32),
        compiler_params=pltpu.CompilerParams(
            dimension_semantics=("parallel", "parallel"), vmem_limit_bytes=int(vmem)),
        name="outproj_layernorm",
    )(att_pm, z, x, w_out_bf, ln_g.reshape(1, d), ln_b.reshape(1, d))


def _sample_inproj_kernel(x_ref, w_ref, cw_ref, st_ref,
                          q_ref, k_ref, v_ref, sg_ref, z_ref, nc_ref, u_scr,
                          *, nb, ts, d_att, d_conv, q_scale):
    x = x_ref[...].astype(jnp.bfloat16)

    def proj(c0, width):
        return jnp.dot(x, w_ref[:, c0:c0 + width], preferred_element_type=jnp.float32)

    q_ref[...] = proj(0, d_att) * q_scale
    k_ref[...] = proj(d_att, d_att)
    v_ref[...] = proj(2 * d_att, d_att)
    sg_ref[...] = _silu(proj(3 * d_att, d_att))
    c0 = 4 * d_att
    gb = proj(c0, d_conv)
    gc = proj(c0 + d_conv, d_conv)
    hh = proj(c0 + 2 * d_conv, d_conv)
    g_conv = proj(c0 + 3 * d_conv, d_conv)
    u = (gc * hh).reshape(nb, ts, d_conv)
    u_scr[:, SUBLANES - 2:SUBLANES, :] = st_ref[...]
    u_scr[:, SUBLANES:SUBLANES + ts, :] = u
    um2 = u_scr[:, SUBLANES - 2:SUBLANES - 2 + ts, :]
    um1 = u_scr[:, SUBLANES - 1:SUBLANES - 1 + ts, :]
    cw = cw_ref[...]
    y_conv = cw[0:1, :][None] * um2 + cw[1:2, :][None] * um1 + cw[2:3, :][None] * u
    z = gb * y_conv.reshape(nb * ts, d_conv) * _silu(g_conv)
    z_ref[...] = z.astype(jnp.bfloat16)
    nc_ref[...] = u_scr[:, ts + SUBLANES - 2:ts + SUBLANES, :]


def _sample_inproj(x2, w_in_bf, conv_w, state, *, nb, ts, d_att, d_conv, q_scale):
    rows, d = x2.shape
    f32 = lambda c: jax.ShapeDtypeStruct((rows, c), jnp.float32)
    kern = functools.partial(_sample_inproj_kernel, nb=nb, ts=ts, d_att=d_att, d_conv=d_conv, q_scale=q_scale)
    vmem = d * w_in_bf.shape[1] * 2 + 16 * rows * d * 4 + 8 * MIB
    return pl.pallas_call(
        kern,
        out_shape=[f32(d_att), f32(d_att), f32(d_att), f32(d_att),
                   jax.ShapeDtypeStruct((rows, d_conv), jnp.bfloat16),
                   jax.ShapeDtypeStruct((nb, 2, d_conv), jnp.float32)],
        scratch_shapes=[pltpu.VMEM((nb, ts + SUBLANES, d_conv), jnp.float32)],
        compiler_params=pltpu.CompilerParams(vmem_limit_bytes=int(vmem)),
        name="sample_inproj",
    )(x2, w_in_bf, conv_w, state)


def _branch_multiplicity(n_q, n_cache):
    i = np.arange(n_q)[:, None]
    pos = np.arange(n_cache + n_q)[None, :]
    dist = n_cache + i - pos
    mult = np.zeros(dist.shape, np.float32)
    for window, dil in DILATED_PAIRS:
        mult += (dist >= 0) & (dist <= window) & (dist % dil == 0)
    return mult


def _sample_attn_kernel(ck_ref, cv_ref, q_ref, kn_ref, vn_ref, sg_ref, mc_ref, mn_ref,
                        nk_ref, nv_ref, att_ref, *, n_heads, head_dim, ts, n_cache, n_pairs):
    d_att = n_heads * head_dim
    rows = n_heads * ts
    head_of_lane = lax.broadcasted_iota(jnp.int32, (rows, d_att), 1) // head_dim
    head_of_row = lax.broadcasted_iota(jnp.int32, (rows, d_att), 0) // ts
    own = head_of_lane == head_of_row

    q = q_ref[0]
    qx = jnp.where(own, jnp.concatenate([q] * n_heads, axis=0), 0.0).astype(jnp.bfloat16)
    kct = ck_ref[0]
    vct = cv_ref[0]
    pad = jnp.zeros((LANES - ts, d_att), jnp.float32)
    knt = jnp.concatenate([pad, kn_ref[0]], axis=0).T
    vnt = jnp.concatenate([pad, vn_ref[0]], axis=0).T
    nt = (((1,), (1,)), ((), ()))
    s_c = jnp.dot(qx, kct.astype(jnp.bfloat16), preferred_element_type=jnp.float32)
    s_n = jnp.dot(qx, knt.astype(jnp.bfloat16), preferred_element_type=jnp.float32)
    mc = mc_ref[...]
    mn = mn_ref[...]
    s_c = jnp.where(mc > 0, s_c, NEG_BIG)
    s_n = jnp.where(mn > 0, s_n, NEG_BIG)
    m = jnp.maximum(jnp.max(s_c, axis=-1, keepdims=True), jnp.max(s_n, axis=-1, keepdims=True))
    e_c = jnp.exp(s_c - m) * mc
    e_n = jnp.exp(s_n - m) * mn
    l = jnp.sum(e_c, axis=-1, keepdims=True) + jnp.sum(e_n, axis=-1, keepdims=True)
    ox = (lax.dot_general(e_c.astype(jnp.bfloat16), vct.astype(jnp.bfloat16), nt,
                          preferred_element_type=jnp.float32)
          + lax.dot_general(e_n.astype(jnp.bfloat16), vnt.astype(jnp.bfloat16), nt,
                            preferred_element_type=jnp.float32))
    ox = jnp.where(own, ox * (1.0 / l), 0.0)
    o = ox[0:ts]
    for h in range(1, n_heads):
        o = o + ox[h * ts:(h + 1) * ts]
    att = o * sg_ref[0]
    for p in range(n_pairs):
        att_ref[0, p] = att[:, p * LANES:(p + 1) * LANES]

    main = n_cache - LANES
    is_old = lax.broadcasted_iota(jnp.int32, (d_att, LANES), 1) < LANES - ts
    for src, new_t, dst in ((kct, knt, nk_ref), (vct, vnt, nv_ref)):
        rolled = pltpu.roll(src, n_cache - ts, axis=1)
        dst[0, :, 0:main] = rolled[:, 0:main]
        dst[0, :, main:n_cache] = jnp.where(is_old, rolled[:, main:n_cache], new_t)


def _sample_attn(cache_kt, cache_vt, q, kn, vn, sg, *, n_heads, head_dim):
    nb, d_att, n_cache = cache_kt.shape
    ts = q.shape[1]
    n_pairs = d_att // LANES
    rows = n_heads * ts
    mult = np.tile(_branch_multiplicity(ts, n_cache), (n_heads, 1))
    mult_c = jnp.asarray(mult[:, :n_cache])
    mult_n = jnp.asarray(np.pad(mult[:, n_cache:], ((0, 0), (LANES - ts, 0))))
    cache_spec = pl.BlockSpec((1, d_att, n_cache), lambda i: (i, 0, 0))
    row_spec = pl.BlockSpec((1, ts, d_att), lambda i: (i, 0, 0))
    kern = functools.partial(_sample_attn_kernel, n_heads=n_heads, head_dim=head_dim, ts=ts,
                             n_cache=n_cache, n_pairs=n_pairs)
    vmem = 10 * n_cache * d_att * 4 + 2 * n_cache * d_att * 2 + 8 * rows * n_cache * 4 + 8 * MIB
    return pl.pallas_call(
        kern,
        grid=(nb,),
        in_specs=[cache_spec, cache_spec, row_spec, row_spec, row_spec, row_spec,
                  pl.BlockSpec((rows, n_cache), lambda i: (0, 0)),
                  pl.BlockSpec((rows, LANES), lambda i: (0, 0))],
        out_specs=[cache_spec, cache_spec,
                   pl.BlockSpec((1, n_pairs, ts, LANES), lambda i: (0, 0, i, 0))],
        out_shape=[jax.ShapeDtypeStruct((nb, d_att, n_cache), jnp.float32),
                   jax.ShapeDtypeStruct((nb, d_att, n_cache), jnp.float32),
                   jax.ShapeDtypeStruct((1, n_pairs, nb * ts, LANES), jnp.float32)],
        compiler_params=pltpu.CompilerParams(
            dimension_semantics=("parallel",), vmem_limit_bytes=int(vmem)),
        name="sample_attention",
    )(cache_kt, cache_vt, q, kn, vn, sg, mult_c, mult_n)


def kernel(x_prompt, x_sample, cache_k, cache_v, state_conv, w_in, conv_w, w_out, ln_g, ln_b):
    b, t, d = x_prompt.shape
    nb, ts, _ = x_sample.shape
    _, n_cache, n_heads, head_dim = cache_k.shape
    d_att = n_heads * head_dim
    d_conv = conv_w.shape[1]
    assert w_in.shape == (d, 4 * d_att + 4 * d_conv) and w_out.shape == (d_att + d_conv, d)
    assert d_att % LANES == 0 and LANES == 2 * head_dim and conv_w.shape[0] == 3
    max_window = max(w for w, _ in DILATED_PAIRS)
    assert n_cache == max_window and t <= max_window
    q_scale = head_dim ** -0.5
    assert math.log2(q_scale).is_integer()

    w_in_bf = w_in.astype(jnp.bfloat16)
    w_out_bf = w_out.astype(jnp.bfloat16)

    def to_time_minor(a):
        return jnp.transpose(a, (0, 2, 3, 1)).reshape(a.shape[0], d_att, a.shape[1])

    def from_time_minor(a):
        return jnp.transpose(a.reshape(a.shape[0], n_heads, head_dim, a.shape[2]), (0, 3, 1, 2))

    conv0 = jnp.zeros((b, 2, d_conv), jnp.float32)
    q_pm, k_pm, v_pm, kt_new, vt_new, sg_pm, z, new_conv_prompt = _inproj(
        x_prompt, w_in_bf, conv_w, conv0, tm=512, d_att=d_att, d_conv=d_conv, q_scale=q_scale)
    att_pm = _prompt_attn(q_pm, k_pm, v_pm, sg_pm, head_dim=head_dim)
    y_prompt = _outproj(att_pm, z, x_prompt, w_out_bf, ln_g, ln_b, tm=512)

    q_s, k_s, v_s, sg_s, z_s, new_conv_sample = _sample_inproj(
        x_sample.reshape(nb * ts, d), w_in_bf, conv_w, state_conv, nb=nb, ts=ts, d_att=d_att, d_conv=d_conv,
        q_scale=q_scale)
    three = lambda a: a.reshape(nb, ts, d_att)
    new_kt_s, new_vt_s, att_s = _sample_attn(
        to_time_minor(cache_k), to_time_minor(cache_v),
        three(q_s), three(k_s), three(v_s), three(sg_s), n_heads=n_heads, head_dim=head_dim)
    y_sample = _outproj(att_s, z_s.reshape(1, nb * ts, d_conv), x_sample.reshape(1, nb * ts, d),
                        w_out_bf, ln_g, ln_b, tm=nb * ts)

    return (y_prompt, y_sample.reshape(nb, ts, d),
            from_time_minor(kt_new), from_time_minor(vt_new), new_conv_prompt,
            from_time_minor(new_kt_s), from_time_minor(new_vt_s), new_conv_sample)
```

```python
import functools
import math

import numpy as np
import jax
import jax.numpy as jnp
from jax import lax
from jax.experimental import pallas as pl
from jax.experimental.pallas import tpu as pltpu

DILATED_PAIRS = ((128, 1), (512, 4), (2048, 16))
LN_EPS = 1e-5
DEPTH = 1
DEEPNORM_ALPHA = (2.0 * DEPTH) ** 0.25

LANES = 128
SUBLANES = 8
Q_BLOCK = 128
NEG_BIG = -1e30
LOOP_UNROLL = 15

MIB = 1024 * 1024


def _dilations():
    dils = tuple(sorted(d for _, d in DILATED_PAIRS))
    assert dils[0] == 1 and all(b % a == 0 for a, b in zip(dils, dils[1:]))
    return dils


def _silu(x):
    return x * (1.0 / (1.0 + jnp.exp(-x)))


def _emit_streams(val, refs, dils, scr, tm):
    n_pairs = val.shape[1] // LANES
    for p in range(n_pairs):
        col = val[:, p * LANES:(p + 1) * LANES]
        refs[0][0, p, 0] = col.astype(jnp.bfloat16)
        if len(dils) > 1:
            scr[0][...] = col
        for lvl in range(1, len(dils)):
            d_prev = dils[lvl - 1]
            f = dils[lvl] // d_prev
            n_prev = tm // d_prev
            n = n_prev // f
            src, dst = scr[(lvl - 1) % 2], scr[lvl % 2]
            for s in range(d_prev):
                for r2 in range(f):
                    r = s + d_prev * r2
                    rows = src[pl.ds(s * n_prev + r2, n, stride=f), :]
                    refs[lvl][0, p, r] = rows.astype(jnp.bfloat16)
                    if lvl + 1 < len(dils):
                        dst[r * n:(r + 1) * n, :] = rows


def _inproj_kernel(*refs, tm, d_att, d_conv, n_pairs, q_scale, dils):
    x_ref, w_ref, cw_ref, hist_ref = refs[:4]
    n_d = len(dils)
    q_refs = refs[4:4 + n_d]
    k_refs = refs[4 + n_d:4 + 2 * n_d]
    v_refs = refs[4 + 2 * n_d:4 + 3 * n_d]
    kt_ref, vt_ref, sg_ref, z_ref, nc_ref, u_scr, s_a, s_b = refs[4 + 3 * n_d:]
    t = pl.program_id(1)
    x = x_ref[0].astype(jnp.bfloat16)

    def proj(c0, width):
        return jnp.dot(x, w_ref[:, c0:c0 + width], preferred_element_type=jnp.float32)

    q = proj(0, d_att) * q_scale
    _emit_streams(q, q_refs, dils, (s_a, s_b), tm)
    k = proj(d_att, d_att)
    kt_ref[0] = k.T
    _emit_streams(k, k_refs, dils, (s_a, s_b), tm)
    v = proj(2 * d_att, d_att)
    vt_ref[0] = v.T
    _emit_streams(v, v_refs, dils, (s_a, s_b), tm)
    sg = _silu(proj(3 * d_att, d_att)).astype(jnp.bfloat16)
    for p in range(n_pairs):
        sg_ref[0, p] = sg[:, p * LANES:(p + 1) * LANES]

    c0 = 4 * d_att
    gb = proj(c0, d_conv)
    gc = proj(c0 + d_conv, d_conv)
    hh = proj(c0 + 2 * d_conv, d_conv)
    g_conv = proj(c0 + 3 * d_conv, d_conv)
    u = gc * hh

    @pl.when(t == 0)
    def _():
        u_scr[SUBLANES - 2:SUBLANES, :] = hist_ref[0]

    u_scr[SUBLANES:SUBLANES + tm, :] = u
    um2 = u_scr[SUBLANES - 2:SUBLANES - 2 + tm, :]
    um1 = u_scr[SUBLANES - 1:SUBLANES - 1 + tm, :]
    y_conv = cw_ref[0:1, :] * um2 + cw_ref[1:2, :] * um1 + cw_ref[2:3, :] * u
    z_ref[0] = (gb * y_conv * _silu(g_conv)).astype(jnp.bfloat16)
    tail = u[tm - 2:tm, :]
    u_scr[SUBLANES - 2:SUBLANES, :] = tail
    nc_ref[0] = tail


def _inproj(x, w_in_bf, conv_w, hist, *, tm, d_att, d_conv, q_scale):
    b, t, d = x.shape
    n_pairs = d_att // LANES
    n_out = w_in_bf.shape[1]
    dils = _dilations()
    assert tm % (dils[-1] * 2 * SUBLANES) == 0
    kern = functools.partial(_inproj_kernel, tm=tm, d_att=d_att, d_conv=d_conv,
                             n_pairs=n_pairs, q_scale=q_scale, dils=dils)
    st_shapes = [jax.ShapeDtypeStruct((b, n_pairs, dd, t // dd, LANES), jnp.bfloat16) for dd in dils]
    st_specs = [pl.BlockSpec((1, n_pairs, dd, tm // dd, LANES), lambda i, j: (i, 0, 0, j, 0)) for dd in dils]
    pm = jax.ShapeDtypeStruct((b, n_pairs, t, LANES), jnp.bfloat16)
    pm_spec = pl.BlockSpec((1, n_pairs, tm, LANES), lambda i, j: (i, 0, j, 0))
    row_spec = lambda c: pl.BlockSpec((1, tm, c), lambda i, j: (i, j, 0))
    col_spec = pl.BlockSpec((1, d_att, tm), lambda i, j: (i, 0, j))
    vmem = (2 * d * n_out * 2 + 2 * tm * d * 4
            + 2 * tm * ((3 * len(dils) + 1) * d_att * 2 + 2 * d_att * 4 + d_conv * 2)
            + (tm + SUBLANES) * d_conv * 4 + 2 * tm * LANES * 4 + 12 * tm * d_att * 4)
    return pl.pallas_call(
        kern,
        grid=(b, t // tm),
        in_specs=[
            row_spec(d),
            pl.BlockSpec((d, n_out), lambda i, j: (0, 0)),
            pl.BlockSpec(conv_w.shape, lambda i, j: (0, 0)),
            pl.BlockSpec((1, 2, d_conv), lambda i, j: (i, 0, 0)),
        ],
        out_specs=st_specs * 3 + [col_spec, col_spec, pm_spec,
                                  row_spec(d_conv), pl.BlockSpec((1, 2, d_conv), lambda i, j: (i, 0, 0))],
        out_shape=st_shapes * 3 + [
            jax.ShapeDtypeStruct((b, d_att, t), jnp.float32),
            jax.ShapeDtypeStruct((b, d_att, t), jnp.float32),
            pm,
            jax.ShapeDtypeStruct((b, t, d_conv), jnp.bfloat16),
            jax.ShapeDtypeStruct((b, 2, d_conv), jnp.float32)],
        scratch_shapes=[pltpu.VMEM((tm + SUBLANES, d_conv), jnp.float32),
                        pltpu.VMEM((tm, LANES), jnp.float32),
                        pltpu.VMEM((tm, LANES), jnp.float32)],
        compiler_params=pltpu.CompilerParams(
            dimension_semantics=("parallel", "arbitrary"), vmem_limit_bytes=int(vmem)),
        name="prompt_inproj",
    )(x, w_in_bf, conv_w, hist)


def _attn_block(q, kwin, vwin, bias, lo):
    zero = jnp.zeros_like(q)
    q2 = jnp.concatenate([jnp.where(lo, q, zero), jnp.where(lo, zero, q)], axis=0)
    s = lax.dot_general(q2, kwin, (((1,), (1,)), ((), ())), preferred_element_type=jnp.float32)
    ps, ls, lses = [], [], []
    for h in range(2):
        sh = s[h * Q_BLOCK:(h + 1) * Q_BLOCK] + bias
        m = jnp.max(sh, axis=-1, keepdims=True)
        p = jnp.exp(sh - m)
        l = jnp.sum(p, axis=-1, keepdims=True)
        ps.append(p.astype(jnp.bfloat16))
        ls.append(l)
        lses.append(m + jnp.log(l))
    pv = jnp.dot(jnp.concatenate(ps, axis=0), vwin, preferred_element_type=jnp.float32)
    o = jnp.where(lo, pv[:Q_BLOCK] * (1.0 / ls[0]), pv[Q_BLOCK:] * (1.0 / ls[1]))
    lse = jnp.where(lo, lses[0], lses[1])
    return o, lse


def _prompt_attn_kernel(*refs, seq, dils, head_dim):
    n_br = len(dils)
    qkv = [refs[3 * i:3 * i + 3] for i in range(n_br)]
    sg_ref = refs[3 * n_br]
    att_ref = refs[3 * n_br + 1]
    scr = refs[3 * n_br + 2:]
    o_scr = scr[0::2]
    l_scr = scr[1::2]

    lane = lax.broadcasted_iota(jnp.int32, (Q_BLOCK, LANES), 1)
    lo = lane < head_dim
    row = lax.broadcasted_iota(jnp.int32, (Q_BLOCK, 2 * Q_BLOCK), 0)
    col = lax.broadcasted_iota(jnp.int32, (Q_BLOCK, 2 * Q_BLOCK), 1)
    band = jnp.where((col >= row) & (col <= row + Q_BLOCK), 0.0, NEG_BIG).astype(jnp.float32)
    row1 = lax.broadcasted_iota(jnp.int32, (Q_BLOCK, Q_BLOCK), 0)
    col1 = lax.broadcasted_iota(jnp.int32, (Q_BLOCK, Q_BLOCK), 1)
    causal = jnp.where(col1 <= row1, 0.0, NEG_BIG).astype(jnp.float32)

    for (q_ref, k_ref, v_ref), d, o_ref, l_ref in zip(qkv, dils, o_scr, l_scr):
        n_blk = seq // d // Q_BLOCK
        for r in range(d):

            def store(row0, o, lse, d=d, r=r, o_ref=o_ref, l_ref=l_ref):
                if d == 1:
                    idx = pl.ds(row0, Q_BLOCK)
                else:
                    idx = pl.ds(r + d * row0, Q_BLOCK, stride=d)
                o_ref[idx, :] = o
                l_ref[idx, :] = lse

            q0 = q_ref[r, 0:Q_BLOCK, :]
            o, lse = _attn_block(q0, k_ref[r, 0:Q_BLOCK, :], v_ref[r, 0:Q_BLOCK, :], causal, lo)
            store(0, o, lse)

            if n_blk > 1:
                def body(qb, carry, q_ref=q_ref, k_ref=k_ref, v_ref=v_ref, r=r, store=store):
                    row0 = pl.multiple_of(qb * Q_BLOCK, Q_BLOCK)
                    k0 = pl.multiple_of(row0 - Q_BLOCK, Q_BLOCK)
                    q = q_ref[r, pl.ds(row0, Q_BLOCK), :]
                    kwin = k_ref[r, pl.ds(k0, 2 * Q_BLOCK), :]
                    vwin = v_ref[r, pl.ds(k0, 2 * Q_BLOCK), :]
                    o, lse = _attn_block(q, kwin, vwin, band, lo)
                    store(row0, o, lse)
                    return carry

                lax.fori_loop(1, n_blk, body, 0, unroll=min(n_blk - 1, LOOP_UNROLL))

    def merge(i, carry):
        rows = pl.ds(pl.multiple_of(i * Q_BLOCK, Q_BLOCK), Q_BLOCK)
        lses = [l_ref[rows, :] for l_ref in l_scr]
        top = functools.reduce(jnp.maximum, lses)
        es = [jnp.exp(x - top) for x in lses]
        num = functools.reduce(lambda a, b: a + b, [e * o_ref[rows, :] for e, o_ref in zip(es, o_scr)])
        den = functools.reduce(lambda a, b: a + b, es)
        att = num * (1.0 / den) * sg_ref[rows, :].astype(jnp.float32)
        att_ref[rows, :] = att.astype(att_ref.dtype)
        return carry

    lax.fori_loop(0, seq // Q_BLOCK, merge, 0)


def _prompt_attn(q_st, k_st, v_st, sg_pm, *, head_dim):
    b, n_pairs, t, _ = sg_pm.shape
    dils = _dilations()
    for w, d in DILATED_PAIRS:
        assert w == d * Q_BLOCK and t % (d * Q_BLOCK) == 0
    ins, specs = [], []
    for i, d in enumerate(dils):
        for a in (q_st[i], k_st[i], v_st[i]):
            assert a.shape == (b, n_pairs, d, t // d, LANES)
            ins.append(a)
            specs.append(pl.BlockSpec((None, None, d, t // d, LANES), lambda i, j: (i, j, 0, 0, 0)))
    pair_spec = pl.BlockSpec((None, None, t, LANES), lambda i, j: (i, j, 0, 0))
    kern = functools.partial(_prompt_attn_kernel, seq=t, dils=dils, head_dim=head_dim)
    vmem = 2 * (3 * len(dils) + 2) * t * LANES * 2 + 2 * len(dils) * t * LANES * 4 + 16 * MIB
    return pl.pallas_call(
        kern,
        grid=(b, n_pairs),
        in_specs=specs + [pair_spec],
        out_specs=pair_spec,
        out_shape=jax.ShapeDtypeStruct((b, n_pairs, t, LANES), jnp.bfloat16),
        scratch_shapes=[pltpu.VMEM((t, LANES), jnp.float32) for _ in range(2 * len(dils))],
        compiler_params=pltpu.CompilerParams(
            dimension_semantics=("parallel", "parallel"), vmem_limit_bytes=int(vmem)),
        name="prompt_attention",
    )(*ins, sg_pm)


def _outproj_kernel(att_ref, z_ref, x_ref, w_ref, g_ref, b_ref, y_ref, *, n_pairs):
    parts = [att_ref[0, p].astype(jnp.bfloat16) for p in range(n_pairs)]
    mix = jnp.concatenate(parts + [z_ref[0].astype(jnp.bfloat16)], axis=-1)
    out = jnp.dot(mix, w_ref[...], preferred_element_type=jnp.float32)
    h = DEEPNORM_ALPHA * x_ref[0] + out
    mu = jnp.mean(h, axis=-1, keepdims=True)
    c = h - mu
    var = jnp.mean(c * c, axis=-1, keepdims=True)
    y_ref[0] = c * lax.rsqrt(var + LN_EPS) * g_ref[...] + b_ref[...]


def _outproj(att_pm, z, x, w_out_bf, ln_g, ln_b, *, tm):
    b, t, d = x.shape
    n_pairs = att_pm.shape[1]
    d_conv = z.shape[-1]
    d_mix = w_out_bf.shape[0]
    vmem = (2 * d_mix * d * 2 + 4 * tm * d * 4 + 2 * tm * n_pairs * LANES * att_pm.dtype.itemsize
            + 2 * tm * d_conv * 2 + 6 * tm * d * 4)
    return pl.pallas_call(
        functools.partial(_outproj_kernel, n_pairs=n_pairs),
        grid=(b, t // tm),
        in_specs=[
            pl.BlockSpec((1, n_pairs, tm, LANES), lambda i, j: (i, 0, j, 0)),
            pl.BlockSpec((1, tm, d_conv), lambda i, j: (i, j, 0)),
            pl.BlockSpec((1, tm, d), lambda i, j: (i, j, 0)),
            pl.BlockSpec((d_mix, d), lambda i, j: (0, 0)),
            pl.BlockSpec((1, d), lambda i, j: (0, 0)),
            pl.BlockSpec((1, d), lambda i, j: (0, 0)),
        ],
        out_specs=pl.BlockSpec((1, tm, d), lambda i, j: (i, j, 0)),
        out_shape=jax.ShapeDtypeStruct((b, t, d), jnp.float32),
        compiler_params=pltpu.CompilerParams(
            dimension_semantics=("parallel", "parallel"), vmem_limit_bytes=int(vmem)),
        name="outproj_layernorm",
    )(att_pm, z, x, w_out_bf, ln_g.reshape(1, d), ln_b.reshape(1, d))


def _sample_inproj_kernel(x_ref, w_ref, cw_ref, st_ref,
                          q_ref, k_ref, v_ref, sg_ref, z_ref, nc_ref, u_scr,
                          *, nb, ts, d_att, d_conv, q_scale):
    x = x_ref[...].astype(jnp.bfloat16)

    def proj(c0, width):
        return jnp.dot(x, w_ref[:, c0:c0 + width], preferred_element_type=jnp.float32)

    q_ref[...] = proj(0, d_att) * q_scale
    k_ref[...] = proj(d_att, d_att)
    v_ref[...] = proj(2 * d_att, d_att)
    sg_ref[...] = _silu(proj(3 * d_att, d_att))
    c0 = 4 * d_att
    gb = proj(c0, d_conv)
    gc = proj(c0 + d_conv, d_conv)
    hh = proj(c0 + 2 * d_conv, d_conv)
    g_conv = proj(c0 + 3 * d_conv, d_conv)
    u = (gc * hh).reshape(nb, ts, d_conv)
    u_scr[:, SUBLANES - 2:SUBLANES, :] = st_ref[...]
    u_scr[:, SUBLANES:SUBLANES + ts, :] = u
    um2 = u_scr[:, SUBLANES - 2:SUBLANES - 2 + ts, :]
    um1 = u_scr[:, SUBLANES - 1:SUBLANES - 1 + ts, :]
    cw = cw_ref[...]
    y_conv = cw[0:1, :][None] * um2 + cw[1:2, :][None] * um1 + cw[2:3, :][None] * u
    z = gb * y_conv.reshape(nb * ts, d_conv) * _silu(g_conv)
    z_ref[...] = z.astype(jnp.bfloat16)
    nc_ref[...] = u_scr[:, ts + SUBLANES - 2:ts + SUBLANES, :]


def _sample_inproj(x2, w_in_bf, conv_w, state, *, nb, ts, d_att, d_conv, q_scale):
    rows, d = x2.shape
    f32 = lambda c: jax.ShapeDtypeStruct((rows, c), jnp.float32)
    kern = functools.partial(_sample_inproj_kernel, nb=nb, ts=ts, d_att=d_att, d_conv=d_conv, q_scale=q_scale)
    vmem = d * w_in_bf.shape[1] * 2 + 16 * rows * d * 4 + 8 * MIB
    return pl.pallas_call(
        kern,
        out_shape=[f32(d_att), f32(d_att), f32(d_att), f32(d_att),
                   jax.ShapeDtypeStruct((rows, d_conv), jnp.bfloat16),
                   jax.ShapeDtypeStruct((nb, 2, d_conv), jnp.float32)],
        scratch_shapes=[pltpu.VMEM((nb, ts + SUBLANES, d_conv), jnp.float32)],
        compiler_params=pltpu.CompilerParams(vmem_limit_bytes=int(vmem)),
        name="sample_inproj",
    )(x2, w_in_bf, conv_w, state)


def _branch_multiplicity(n_q, n_cache):
    i = np.arange(n_q)[:, None]
    pos = np.arange(n_cache + n_q)[None, :]
    dist = n_cache + i - pos
    mult = np.zeros(dist.shape, np.float32)
    for window, dil in DILATED_PAIRS:
        mult += (dist >= 0) & (dist <= window) & (dist % dil == 0)
    return mult


def _sample_attn_kernel(ck_ref, cv_ref, q_ref, kn_ref, vn_ref, sg_ref, mc_ref, mn_ref,
                        nk_ref, nv_ref, att_ref, *, n_heads, head_dim, ts, n_cache, n_pairs):
    d_att = n_heads * head_dim
    rows = n_heads * ts
    head_of_lane = lax.broadcasted_iota(jnp.int32, (rows, d_att), 1) // head_dim
    head_of_row = lax.broadcasted_iota(jnp.int32, (rows, d_att), 0) // ts
    own = head_of_lane == head_of_row

    q = q_ref[0]
    qx = jnp.where(own, jnp.concatenate([q] * n_heads, axis=0), 0.0).astype(jnp.bfloat16)
    kct = ck_ref[0]
    vct = cv_ref[0]
    pad = jnp.zeros((LANES - ts, d_att), jnp.float32)
    knt = jnp.concatenate([pad, kn_ref[0]], axis=0).T
    vnt = jnp.concatenate([pad, vn_ref[0]], axis=0).T
    nt = (((1,), (1,)), ((), ()))
    s_c = jnp.dot(qx, kct.astype(jnp.bfloat16), preferred_element_type=jnp.float32)
    s_n = jnp.dot(qx, knt.astype(jnp.bfloat16), preferred_element_type=jnp.float32)
    mc = mc_ref[...]
    mn = mn_ref[...]
    s_c = jnp.where(mc > 0, s_c, NEG_BIG)
    s_n = jnp.where(mn > 0, s_n, NEG_BIG)
    m = jnp.maximum(jnp.max(s_c, axis=-1, keepdims=True), jnp.max(s_n, axis=-1, keepdims=True))
    e_c = jnp.exp(s_c - m) * mc
    e_n = jnp.exp(s_n - m) * mn
    l = jnp.sum(e_c, axis=-1, keepdims=True) + jnp.sum(e_n, axis=-1, keepdims=True)
    ox = (lax.dot_general(e_c.astype(jnp.bfloat16), vct.astype(jnp.bfloat16), nt,
                          preferred_element_type=jnp.float32)
          + lax.dot_general(e_n.astype(jnp.bfloat16), vnt.astype(jnp.bfloat16), nt,
                            preferred_element_type=jnp.float32))
    ox = jnp.where(own, ox * (1.0 / l), 0.0)
    o = ox[0:ts]
    for h in range(1, n_heads):
        o = o + ox[h * ts:(h + 1) * ts]
    att = o * sg_ref[0]
    for p in range(n_pairs):
        att_ref[0, p] = att[:, p * LANES:(p + 1) * LANES]

    main = n_cache - LANES
    is_old = lax.broadcasted_iota(jnp.int32, (d_att, LANES), 1) < LANES - ts
    for src, new_t, dst in ((kct, knt, nk_ref), (vct, vnt, nv_ref)):
        rolled = pltpu.roll(src, n_cache - ts, axis=1)
        dst[0, :, 0:main] = rolled[:, 0:main]
        dst[0, :, main:n_cache] = jnp.where(is_old, rolled[:, main:n_cache], new_t)


def _sample_attn(cache_kt, cache_vt, q, kn, vn, sg, *, n_heads, head_dim):
    nb, d_att, n_cache = cache_kt.shape
    ts = q.shape[1]
    n_pairs = d_att // LANES
    rows = n_heads * ts
    mult = np.tile(_branch_multiplicity(ts, n_cache), (n_heads, 1))
    mult_c = jnp.asarray(mult[:, :n_cache])
    mult_n = jnp.asarray(np.pad(mult[:, n_cache:], ((0, 0), (LANES - ts, 0))))
    cache_spec = pl.BlockSpec((1, d_att, n_cache), lambda i: (i, 0, 0))
    row_spec = pl.BlockSpec((1, ts, d_att), lambda i: (i, 0, 0))
    kern = functools.partial(_sample_attn_kernel, n_heads=n_heads, head_dim=head_dim, ts=ts,
                             n_cache=n_cache, n_pairs=n_pairs)
    vmem = 10 * n_cache * d_att * 4 + 2 * n_cache * d_att * 2 + 8 * rows * n_cache * 4 + 8 * MIB
    return pl.pallas_call(
        kern,
        grid=(nb,),
        in_specs=[cache_spec, cache_spec, row_spec, row_spec, row_spec, row_spec,
                  pl.BlockSpec((rows, n_cache), lambda i: (0, 0)),
                  pl.BlockSpec((rows, LANES), lambda i: (0, 0))],
        out_specs=[cache_spec, cache_spec,
                   pl.BlockSpec((1, n_pairs, ts, LANES), lambda i: (0, 0, i, 0))],
        out_shape=[jax.ShapeDtypeStruct((nb, d_att, n_cache), jnp.float32),
                   jax.ShapeDtypeStruct((nb, d_att, n_cache), jnp.float32),
                   jax.ShapeDtypeStruct((1, n_pairs, nb * ts, LANES), jnp.float32)],
        compiler_params=pltpu.CompilerParams(
            dimension_semantics=("parallel",), vmem_limit_bytes=int(vmem)),
        name="sample_attention",
    )(cache_kt, cache_vt, q, kn, vn, sg, mult_c, mult_n)


def kernel(x_prompt, x_sample, cache_k, cache_v, state_conv, w_in, conv_w, w_out, ln_g, ln_b):
    b, t, d = x_prompt.shape
    nb, ts, _ = x_sample.shape
    _, n_cache, n_heads, head_dim = cache_k.shape
    d_att = n_heads * head_dim
    d_conv = conv_w.shape[1]
    assert w_in.shape == (d, 4 * d_att + 4 * d_conv) and w_out.shape == (d_att + d_conv, d)
    assert d_att % LANES == 0 and LANES == 2 * head_dim and conv_w.shape[0] == 3
    max_window = max(w for w, _ in DILATED_PAIRS)
    assert n_cache == max_window and t <= max_window
    q_scale = head_dim ** -0.5
    assert math.log2(q_scale).is_integer()

    w_in_bf = w_in.astype(jnp.bfloat16)
    w_out_bf = w_out.astype(jnp.bfloat16)

    def to_time_minor(a):
        return jnp.transpose(a, (0, 2, 3, 1)).reshape(a.shape[0], d_att, a.shape[1])

    def from_time_minor(a):
        return jnp.transpose(a.reshape(a.shape[0], n_heads, head_dim, a.shape[2]), (0, 3, 1, 2))

    conv0 = jnp.zeros((b, 2, d_conv), jnp.float32)
    outs = _inproj(x_prompt, w_in_bf, conv_w, conv0, tm=512, d_att=d_att, d_conv=d_conv, q_scale=q_scale)
    n_d = len(DILATED_PAIRS)
    q_st, k_st, v_st = outs[:n_d], outs[n_d:2 * n_d], outs[2 * n_d:3 * n_d]
    kt_new, vt_new, sg_pm, z, new_conv_prompt = outs[3 * n_d:]
    att_pm = _prompt_attn(q_st, k_st, v_st, sg_pm, head_dim=head_dim)
    y_prompt = _outproj(att_pm, z, x_prompt, w_out_bf, ln_g, ln_b, tm=512)

    q_s, k_s, v_s, sg_s, z_s, new_conv_sample = _sample_inproj(
        x_sample.reshape(nb * ts, d), w_in_bf, conv_w, state_conv, nb=nb, ts=ts, d_att=d_att, d_conv=d_conv,
        q_scale=q_scale)
    three = lambda a: a.reshape(nb, ts, d_att)
    new_kt_s, new_vt_s, att_s = _sample_attn(
        to_time_minor(cache_k), to_time_minor(cache_v),
        three(q_s), three(k_s), three(v_s), three(sg_s), n_heads=n_heads, head_dim=head_dim)
    y_sample = _outproj(att_s, z_s.reshape(1, nb * ts, d_conv), x_sample.reshape(1, nb * ts, d),
                        w_out_bf, ln_g, ln_b, tm=nb * ts)

    return (y_prompt, y_sample.reshape(nb, ts, d),
            from_time_minor(kt_new), from_time_minor(vt_new), new_conv_prompt,
            from_time_minor(new_kt_s), from_time_minor(new_vt_s), new_conv_sample)
```

```python
import functools
import math

import numpy as np
import jax
import jax.numpy as jnp
from jax import lax
from jax.experimental import pallas as pl
from jax.experimental.pallas import tpu as pltpu

DILATED_PAIRS = ((128, 1), (512, 4), (2048, 16))
LN_EPS = 1e-5
DEPTH = 1
DEEPNORM_ALPHA = (2.0 * DEPTH) ** 0.25

LANES = 128
SUBLANES = 8
Q_BLOCK = 128
NEG_BIG = -1e30
LOOP_UNROLL = 15
OUTPROJ_CHUNKS = 2

MIB = 1024 * 1024


def _dilations():
    dils = tuple(sorted(d for _, d in DILATED_PAIRS))
    assert dils[0] == 1 and all(b % a == 0 for a, b in zip(dils, dils[1:]))
    return dils


def _silu(x):
    return x * (1.0 / (1.0 + jnp.exp(-x)))


def _emit_streams(val, refs, dils, scr, tm):
    n_pairs = val.shape[1] // LANES
    for p in range(n_pairs):
        col = val[:, p * LANES:(p + 1) * LANES]
        refs[0][0, p, 0] = col.astype(jnp.bfloat16)
        if len(dils) > 1:
            scr[0][...] = col
        for lvl in range(1, len(dils)):
            d_prev = dils[lvl - 1]
            f = dils[lvl] // d_prev
            n_prev = tm // d_prev
            n = n_prev // f
            src, dst = scr[(lvl - 1) % 2], scr[lvl % 2]
            for s in range(d_prev):
                for r2 in range(f):
                    r = s + d_prev * r2
                    rows = src[pl.ds(s * n_prev + r2, n, stride=f), :]
                    refs[lvl][0, p, r] = rows.astype(jnp.bfloat16)
                    if lvl + 1 < len(dils):
                        dst[r * n:(r + 1) * n, :] = rows


def _inproj_kernel(*refs, tm, d_att, d_conv, n_pairs, q_scale, dils):
    x_ref, w_ref, cw_ref, hist_ref = refs[:4]
    n_d = len(dils)
    q_refs = refs[4:4 + n_d]
    k_refs = refs[4 + n_d:4 + 2 * n_d]
    v_refs = refs[4 + 2 * n_d:4 + 3 * n_d]
    kt_ref, vt_ref, sg_ref, z_ref, nc_ref, u_scr, s_a, s_b = refs[4 + 3 * n_d:]
    @pl.when(pl.program_id(1) == 0)
    def _():
        u_scr[SUBLANES - 2:SUBLANES, :] = hist_ref[0]

    x = x_ref[0].astype(jnp.bfloat16)

    def proj(c0, width):
        return jnp.dot(x, w_ref[:, c0:c0 + width], preferred_element_type=jnp.float32)

    c0 = 4 * d_att
    gc = proj(c0 + d_conv, d_conv)
    hh = proj(c0 + 2 * d_conv, d_conv)
    u = gc * hh
    u_scr[SUBLANES:SUBLANES + tm, :] = u
    um2 = u_scr[SUBLANES - 2:SUBLANES - 2 + tm, :]
    um1 = u_scr[SUBLANES - 1:SUBLANES - 1 + tm, :]
    y_conv = cw_ref[0:1, :] * um2 + cw_ref[1:2, :] * um1 + cw_ref[2:3, :] * u
    tail = u[tm - 2:tm, :]
    u_scr[SUBLANES - 2:SUBLANES, :] = tail
    nc_ref[0] = tail
    gb = proj(c0, d_conv)
    g_conv = proj(c0 + 3 * d_conv, d_conv)
    z_ref[0] = (gb * y_conv * _silu(g_conv)).astype(jnp.bfloat16)

    sg = _silu(proj(3 * d_att, d_att)).astype(jnp.bfloat16)
    for p in range(n_pairs):
        sg_ref[0, p] = sg[:, p * LANES:(p + 1) * LANES]
    k = proj(d_att, d_att)
    kt_ref[0] = k.T
    _emit_streams(k, k_refs, dils, (s_a, s_b), tm)
    v = proj(2 * d_att, d_att)
    vt_ref[0] = v.T
    _emit_streams(v, v_refs, dils, (s_a, s_b), tm)
    q = proj(0, d_att) * q_scale
    _emit_streams(q, q_refs, dils, (s_a, s_b), tm)


def _inproj(x, w_in_bf, conv_w, hist, *, tm, d_att, d_conv, q_scale):
    b, t, d = x.shape
    n_pairs = d_att // LANES
    n_out = w_in_bf.shape[1]
    dils = _dilations()
    assert tm % (dils[-1] * 2 * SUBLANES) == 0
    kern = functools.partial(_inproj_kernel, tm=tm, d_att=d_att, d_conv=d_conv,
                             n_pairs=n_pairs, q_scale=q_scale, dils=dils)
    st_shapes = [jax.ShapeDtypeStruct((b, n_pairs, dd, t // dd, LANES), jnp.bfloat16) for dd in dils]
    st_specs = [pl.BlockSpec((1, n_pairs, dd, tm // dd, LANES), lambda i, j: (i, 0, 0, j, 0)) for dd in dils]
    pm = jax.ShapeDtypeStruct((b, n_pairs, t, LANES), jnp.bfloat16)
    pm_spec = pl.BlockSpec((1, n_pairs, tm, LANES), lambda i, j: (i, 0, j, 0))
    row_spec = lambda c: pl.BlockSpec((1, tm, c), lambda i, j: (i, j, 0))
    col_spec = pl.BlockSpec((1, d_att, tm), lambda i, j: (i, 0, j))
    vmem = (2 * d * n_out * 2 + 2 * tm * d * 4
            + 2 * tm * ((3 * len(dils) + 1) * d_att * 2 + 2 * d_att * 4 + d_conv * 2)
            + (tm + SUBLANES) * d_conv * 4 + 2 * tm * LANES * 4 + 12 * tm * d_att * 4)
    return pl.pallas_call(
        kern,
        grid=(b, t // tm),
        in_specs=[
            row_spec(d),
            pl.BlockSpec((d, n_out), lambda i, j: (0, 0)),
            pl.BlockSpec(conv_w.shape, lambda i, j: (0, 0)),
            pl.BlockSpec((1, 2, d_conv), lambda i, j: (i, 0, 0)),
        ],
        out_specs=st_specs * 3 + [col_spec, col_spec, pm_spec,
                                  row_spec(d_conv), pl.BlockSpec((1, 2, d_conv), lambda i, j: (i, 0, 0))],
        out_shape=st_shapes * 3 + [
            jax.ShapeDtypeStruct((b, d_att, t), jnp.float32),
            jax.ShapeDtypeStruct((b, d_att, t), jnp.float32),
            pm,
            jax.ShapeDtypeStruct((b, t, d_conv), jnp.bfloat16),
            jax.ShapeDtypeStruct((b, 2, d_conv), jnp.float32)],
        scratch_shapes=[pltpu.VMEM((tm + SUBLANES, d_conv), jnp.float32),
                        pltpu.VMEM((tm, LANES), jnp.float32),
                        pltpu.VMEM((tm, LANES), jnp.float32)],
        compiler_params=pltpu.CompilerParams(
            dimension_semantics=("parallel", "arbitrary"), vmem_limit_bytes=int(vmem)),
        name="prompt_inproj",
    )(x, w_in_bf, conv_w, hist)


def _attn_block(q, kwin, vwin, bias, lo):
    zero = jnp.zeros_like(q)
    q2 = jnp.concatenate([jnp.where(lo, q, zero), jnp.where(lo, zero, q)], axis=0)
    s = lax.dot_general(q2, kwin, (((1,), (1,)), ((), ())), preferred_element_type=jnp.float32)
    ps, ls, lses = [], [], []
    for h in range(2):
        sh = s[h * Q_BLOCK:(h + 1) * Q_BLOCK] + bias
        m = jnp.max(sh, axis=-1, keepdims=True)
        p = jnp.exp(sh - m)
        l = jnp.sum(p, axis=-1, keepdims=True)
        ps.append(p.astype(jnp.bfloat16))
        ls.append(l)
        lses.append(m + jnp.log(l))
    pv = jnp.dot(jnp.concatenate(ps, axis=0), vwin, preferred_element_type=jnp.float32)
    o = jnp.where(lo, pv[:Q_BLOCK] * (1.0 / ls[0]), pv[Q_BLOCK:] * (1.0 / ls[1]))
    lse = jnp.where(lo, lses[0], lses[1])
    return o, lse


def _prompt_attn_kernel(*refs, seq, dils, head_dim):
    n_br = len(dils)
    qkv = [refs[3 * i:3 * i + 3] for i in range(n_br)]
    sg_ref = refs[3 * n_br]
    att_ref = refs[3 * n_br + 1]
    scr = refs[3 * n_br + 2:]
    o_scr = scr[0::2]
    l_scr = scr[1::2]

    lane = lax.broadcasted_iota(jnp.int32, (Q_BLOCK, LANES), 1)
    lo = lane < head_dim
    row = lax.broadcasted_iota(jnp.int32, (Q_BLOCK, 2 * Q_BLOCK), 0)
    col = lax.broadcasted_iota(jnp.int32, (Q_BLOCK, 2 * Q_BLOCK), 1)
    band = jnp.where((col >= row) & (col <= row + Q_BLOCK), 0.0, NEG_BIG).astype(jnp.float32)
    row1 = lax.broadcasted_iota(jnp.int32, (Q_BLOCK, Q_BLOCK), 0)
    col1 = lax.broadcasted_iota(jnp.int32, (Q_BLOCK, Q_BLOCK), 1)
    causal = jnp.where(col1 <= row1, 0.0, NEG_BIG).astype(jnp.float32)

    for (q_ref, k_ref, v_ref), d, o_ref, l_ref in zip(qkv, dils, o_scr, l_scr):
        n_blk = seq // d // Q_BLOCK
        for r in range(d):

            def store(row0, o, lse, d=d, r=r, o_ref=o_ref, l_ref=l_ref):
                if d == 1:
                    idx = pl.ds(row0, Q_BLOCK)
                else:
                    idx = pl.ds(r + d * row0, Q_BLOCK, stride=d)
                o_ref[idx, :] = o
                l_ref[idx, :] = lse

            q0 = q_ref[r, 0:Q_BLOCK, :]
            o, lse = _attn_block(q0, k_ref[r, 0:Q_BLOCK, :], v_ref[r, 0:Q_BLOCK, :], causal, lo)
            store(0, o, lse)

            if n_blk > 1:
                def body(qb, carry, q_ref=q_ref, k_ref=k_ref, v_ref=v_ref, r=r, store=store):
                    row0 = pl.multiple_of(qb * Q_BLOCK, Q_BLOCK)
                    k0 = pl.multiple_of(row0 - Q_BLOCK, Q_BLOCK)
                    q = q_ref[r, pl.ds(row0, Q_BLOCK), :]
                    kwin = k_ref[r, pl.ds(k0, 2 * Q_BLOCK), :]
                    vwin = v_ref[r, pl.ds(k0, 2 * Q_BLOCK), :]
                    o, lse = _attn_block(q, kwin, vwin, band, lo)
                    store(row0, o, lse)
                    return carry

                lax.fori_loop(1, n_blk, body, 0, unroll=min(n_blk - 1, LOOP_UNROLL))

    def merge(i, carry):
        rows = pl.ds(pl.multiple_of(i * Q_BLOCK, Q_BLOCK), Q_BLOCK)
        lses = [l_ref[rows, :] for l_ref in l_scr]
        top = functools.reduce(jnp.maximum, lses)
        es = [jnp.exp(x - top) for x in lses]
        num = functools.reduce(lambda a, b: a + b, [e * o_ref[rows, :] for e, o_ref in zip(es, o_scr)])
        den = functools.reduce(lambda a, b: a + b, es)
        att = num * (1.0 / den) * sg_ref[rows, :].astype(jnp.float32)
        att_ref[rows, :] = att.astype(att_ref.dtype)
        return carry

    lax.fori_loop(0, seq // Q_BLOCK, merge, 0)


def _prompt_attn(q_st, k_st, v_st, sg_pm, *, head_dim):
    b, n_pairs, t, _ = sg_pm.shape
    dils = _dilations()
    for w, d in DILATED_PAIRS:
        assert w == d * Q_BLOCK and t % (d * Q_BLOCK) == 0
    ins, specs = [], []
    for i, d in enumerate(dils):
        for a in (q_st[i], k_st[i], v_st[i]):
            assert a.shape == (b, n_pairs, d, t // d, LANES)
            ins.append(a)
            specs.append(pl.BlockSpec((None, None, d, t // d, LANES), lambda i, j: (i, j, 0, 0, 0)))
    pair_spec = pl.BlockSpec((None, None, t, LANES), lambda i, j: (i, j, 0, 0))
    kern = functools.partial(_prompt_attn_kernel, seq=t, dils=dils, head_dim=head_dim)
    vmem = 2 * (3 * len(dils) + 2) * t * LANES * 2 + 2 * len(dils) * t * LANES * 4 + 16 * MIB
    return pl.pallas_call(
        kern,
        grid=(b, n_pairs),
        in_specs=specs + [pair_spec],
        out_specs=pair_spec,
        out_shape=jax.ShapeDtypeStruct((b, n_pairs, t, LANES), jnp.bfloat16),
        scratch_shapes=[pltpu.VMEM((t, LANES), jnp.float32) for _ in range(2 * len(dils))],
        compiler_params=pltpu.CompilerParams(
            dimension_semantics=("parallel", "parallel"), vmem_limit_bytes=int(vmem)),
        name="prompt_attention",
    )(*ins, sg_pm)


def _outproj_kernel(att_ref, z_ref, x_ref, w_ref, g_ref, b_ref, y_ref, *, n_pairs, n_chunks):
    tm = x_ref.shape[1]
    rows = tm // n_chunks
    for c in range(n_chunks):
        sl = slice(c * rows, (c + 1) * rows)
        parts = [att_ref[0, p, sl, :].astype(jnp.bfloat16) for p in range(n_pairs)]
        mix = jnp.concatenate(parts + [z_ref[0, sl, :].astype(jnp.bfloat16)], axis=-1)
        out = jnp.dot(mix, w_ref[...], preferred_element_type=jnp.float32)
        h = DEEPNORM_ALPHA * x_ref[0, sl, :] + out
        mu = jnp.mean(h, axis=-1, keepdims=True)
        d = h - mu
        var = jnp.mean(d * d, axis=-1, keepdims=True)
        y_ref[0, sl, :] = d * lax.rsqrt(var + LN_EPS) * g_ref[...] + b_ref[...]


def _outproj(att_pm, z, x, w_out_bf, ln_g, ln_b, *, tm):
    b, t, d = x.shape
    n_pairs = att_pm.shape[1]
    d_conv = z.shape[-1]
    d_mix = w_out_bf.shape[0]
    vmem = (2 * d_mix * d * 2 + 4 * tm * d * 4 + 2 * tm * n_pairs * LANES * att_pm.dtype.itemsize
            + 2 * tm * d_conv * 2 + 6 * tm * d * 4)
    return pl.pallas_call(
        functools.partial(_outproj_kernel, n_pairs=n_pairs, n_chunks=OUTPROJ_CHUNKS),
        grid=(b, t // tm),
        in_specs=[
            pl.BlockSpec((1, n_pairs, tm, LANES), lambda i, j: (i, 0, j, 0)),
            pl.BlockSpec((1, tm, d_conv), lambda i, j: (i, j, 0)),
            pl.BlockSpec((1, tm, d), lambda i, j: (i, j, 0)),
            pl.BlockSpec((d_mix, d), lambda i, j: (0, 0)),
            pl.BlockSpec((1, d), lambda i, j: (0, 0)),
            pl.BlockSpec((1, d), lambda i, j: (0, 0)),
        ],
        out_specs=pl.BlockSpec((1, tm, d), lambda i, j: (i, j, 0)),
        out_shape=jax.ShapeDtypeStruct((b, t, d), jnp.float32),
        compiler_params=pltpu.CompilerParams(
            dimension_semantics=("parallel", "parallel"), vmem_limit_bytes=int(vmem)),
        name="outproj_layernorm",
    )(att_pm, z, x, w_out_bf, ln_g.reshape(1, d), ln_b.reshape(1, d))


def _sample_inproj_kernel(x_ref, w_ref, cw_ref, st_ref,
                          q_ref, k_ref, v_ref, sg_ref, z_ref, nc_ref, u_scr,
                          *, nb, ts, d_att, d_conv, q_scale):
    x = x_ref[...].astype(jnp.bfloat16)

    def proj(c0, width):
        return jnp.dot(x, w_ref[:, c0:c0 + width], preferred_element_type=jnp.float32)

    q_ref[...] = proj(0, d_att) * q_scale
    k_ref[...] = proj(d_att, d_att)
    v_ref[...] = proj(2 * d_att, d_att)
    sg_ref[...] = _silu(proj(3 * d_att, d_att))
    c0 = 4 * d_att
    gb = proj(c0, d_conv)
    gc = proj(c0 + d_conv, d_conv)
    hh = proj(c0 + 2 * d_conv, d_conv)
    g_conv = proj(c0 + 3 * d_conv, d_conv)
    u = (gc * hh).reshape(nb, ts, d_conv)
    u_scr[:, SUBLANES - 2:SUBLANES, :] = st_ref[...]
    u_scr[:, SUBLANES:SUBLANES + ts, :] = u
    um2 = u_scr[:, SUBLANES - 2:SUBLANES - 2 + ts, :]
    um1 = u_scr[:, SUBLANES - 1:SUBLANES - 1 + ts, :]
    cw = cw_ref[...]
    y_conv = cw[0:1, :][None] * um2 + cw[1:2, :][None] * um1 + cw[2:3, :][None] * u
    z = gb * y_conv.reshape(nb * ts, d_conv) * _silu(g_conv)
    z_ref[...] = z.astype(jnp.bfloat16)
    nc_ref[...] = u_scr[:, ts + SUBLANES - 2:ts + SUBLANES, :]


def _sample_inproj(x2, w_in_bf, conv_w, state, *, nb, ts, d_att, d_conv, q_scale):
    rows, d = x2.shape
    f32 = lambda c: jax.ShapeDtypeStruct((rows, c), jnp.float32)
    kern = functools.partial(_sample_inproj_kernel, nb=nb, ts=ts, d_att=d_att, d_conv=d_conv, q_scale=q_scale)
    vmem = d * w_in_bf.shape[1] * 2 + 16 * rows * d * 4 + 8 * MIB
    return pl.pallas_call(
        kern,
        out_shape=[f32(d_att), f32(d_att), f32(d_att), f32(d_att),
                   jax.ShapeDtypeStruct((rows, d_conv), jnp.bfloat16),
                   jax.ShapeDtypeStruct((nb, 2, d_conv), jnp.float32)],
        scratch_shapes=[pltpu.VMEM((nb, ts + SUBLANES, d_conv), jnp.float32)],
        compiler_params=pltpu.CompilerParams(vmem_limit_bytes=int(vmem)),
        name="sample_inproj",
    )(x2, w_in_bf, conv_w, state)


def _branch_multiplicity(n_q, n_cache):
    i = np.arange(n_q)[:, None]
    pos = np.arange(n_cache + n_q)[None, :]
    dist = n_cache + i - pos
    mult = np.zeros(dist.shape, np.float32)
    for window, dil in DILATED_PAIRS:
        mult += (dist >= 0) & (dist <= window) & (dist % dil == 0)
    return mult


def _sample_attn_kernel(ck_ref, cv_ref, q_ref, kn_ref, vn_ref, sg_ref, mc_ref, mn_ref,
                        nk_ref, nv_ref, att_ref, *, n_heads, head_dim, ts, n_cache, n_pairs):
    d_att = n_heads * head_dim
    rows = n_heads * ts
    head_of_lane = lax.broadcasted_iota(jnp.int32, (rows, d_att), 1) // head_dim
    head_of_row = lax.broadcasted_iota(jnp.int32, (rows, d_att), 0) // ts
    own = head_of_lane == head_of_row

    q = q_ref[0]
    qx = jnp.where(own, jnp.concatenate([q] * n_heads, axis=0), 0.0).astype(jnp.bfloat16)
    kct = ck_ref[0]
    vct = cv_ref[0]
    pad = jnp.zeros((LANES - ts, d_att), jnp.float32)
    knt = jnp.concatenate([pad, kn_ref[0]], axis=0).T
    vnt = jnp.concatenate([pad, vn_ref[0]], axis=0).T
    nt = (((1,), (1,)), ((), ()))
    s_c = jnp.dot(qx, kct.astype(jnp.bfloat16), preferred_element_type=jnp.float32)
    s_n = jnp.dot(qx, knt.astype(jnp.bfloat16), preferred_element_type=jnp.float32)
    mc = mc_ref[...]
    mn = mn_ref[...]
    s_c = jnp.where(mc > 0, s_c, NEG_BIG)
    s_n = jnp.where(mn > 0, s_n, NEG_BIG)
    m = jnp.maximum(jnp.max(s_c, axis=-1, keepdims=True), jnp.max(s_n, axis=-1, keepdims=True))
    e_c = jnp.exp(s_c - m) * mc
    e_n = jnp.exp(s_n - m) * mn
    l = jnp.sum(e_c, axis=-1, keepdims=True) + jnp.sum(e_n, axis=-1, keepdims=True)
    ox = (lax.dot_general(e_c.astype(jnp.bfloat16), vct.astype(jnp.bfloat16), nt,
                          preferred_element_type=jnp.float32)
          + lax.dot_general(e_n.astype(jnp.bfloat16), vnt.astype(jnp.bfloat16), nt,
                            preferred_element_type=jnp.float32))
    ox = jnp.where(own, ox * (1.0 / l), 0.0)
    o = ox[0:ts]
    for h in range(1, n_heads):
        o = o + ox[h * ts:(h + 1) * ts]
    att = o * sg_ref[0]
    for p in range(n_pairs):
        att_ref[0, p] = att[:, p * LANES:(p + 1) * LANES]

    main = n_cache - LANES
    is_old = lax.broadcasted_iota(jnp.int32, (d_att, LANES), 1) < LANES - ts
    for src, new_t, dst in ((kct, knt, nk_ref), (vct, vnt, nv_ref)):
        rolled = pltpu.roll(src, n_cache - ts, axis=1)
        dst[0, :, 0:main] = rolled[:, 0:main]
        dst[0, :, main:n_cache] = jnp.where(is_old, rolled[:, main:n_cache], new_t)


def _sample_attn(cache_kt, cache_vt, q, kn, vn, sg, *, n_heads, head_dim):
    nb, d_att, n_cache = cache_kt.shape
    ts = q.shape[1]
    n_pairs = d_att // LANES
    rows = n_heads * ts
    mult = np.tile(_branch_multiplicity(ts, n_cache), (n_heads, 1))
    mult_c = jnp.asarray(mult[:, :n_cache])
    mult_n = jnp.asarray(np.pad(mult[:, n_cache:], ((0, 0), (LANES - ts, 0))))
    cache_spec = pl.BlockSpec((1, d_att, n_cache), lambda i: (i, 0, 0))
    row_spec = pl.BlockSpec((1, ts, d_att), lambda i: (i, 0, 0))
    kern = functools.partial(_sample_attn_kernel, n_heads=n_heads, head_dim=head_dim, ts=ts,
                             n_cache=n_cache, n_pairs=n_pairs)
    vmem = 10 * n_cache * d_att * 4 + 2 * n_cache * d_att * 2 + 8 * rows * n_cache * 4 + 8 * MIB
    return pl.pallas_call(
        kern,
        grid=(nb,),
        in_specs=[cache_spec, cache_spec, row_spec, row_spec, row_spec, row_spec,
                  pl.BlockSpec((rows, n_cache), lambda i: (0, 0)),
                  pl.BlockSpec((rows, LANES), lambda i: (0, 0))],
        out_specs=[cache_spec, cache_spec,
                   pl.BlockSpec((1, n_pairs, ts, LANES), lambda i: (0, 0, i, 0))],
        out_shape=[jax.ShapeDtypeStruct((nb, d_att, n_cache), jnp.float32),
                   jax.ShapeDtypeStruct((nb, d_att, n_cache), jnp.float32),
                   jax.ShapeDtypeStruct((1, n_pairs, nb * ts, LANES), jnp.float32)],
        compiler_params=pltpu.CompilerParams(
            dimension_semantics=("parallel",), vmem_limit_bytes=int(vmem)),
        name="sample_attention",
    )(cache_kt, cache_vt, q, kn, vn, sg, mult_c, mult_n)


def kernel(x_prompt, x_sample, cache_k, cache_v, state_conv, w_in, conv_w, w_out, ln_g, ln_b):
    b, t, d = x_prompt.shape
    nb, ts, _ = x_sample.shape
    _, n_cache, n_heads, head_dim = cache_k.shape
    d_att = n_heads * head_dim
    d_conv = conv_w.shape[1]
    assert w_in.shape == (d, 4 * d_att + 4 * d_conv) and w_out.shape == (d_att + d_conv, d)
    assert d_att % LANES == 0 and LANES == 2 * head_dim and conv_w.shape[0] == 3
    max_window = max(w for w, _ in DILATED_PAIRS)
    assert n_cache == max_window and t <= max_window
    q_scale = head_dim ** -0.5
    assert math.log2(q_scale).is_integer()

    w_in_bf = w_in.astype(jnp.bfloat16)
    w_out_bf = w_out.astype(jnp.bfloat16)

    def to_time_minor(a):
        return jnp.transpose(a, (0, 2, 3, 1)).reshape(a.shape[0], d_att, a.shape[1])

    def from_time_minor(a):
        return jnp.transpose(a.reshape(a.shape[0], n_heads, head_dim, a.shape[2]), (0, 3, 1, 2))

    conv0 = jnp.zeros((b, 2, d_conv), jnp.float32)
    outs = _inproj(x_prompt, w_in_bf, conv_w, conv0, tm=512, d_att=d_att, d_conv=d_conv, q_scale=q_scale)
    n_d = len(DILATED_PAIRS)
    q_st, k_st, v_st = outs[:n_d], outs[n_d:2 * n_d], outs[2 * n_d:3 * n_d]
    kt_new, vt_new, sg_pm, z, new_conv_prompt = outs[3 * n_d:]
    att_pm = _prompt_attn(q_st, k_st, v_st, sg_pm, head_dim=head_dim)
    y_prompt = _outproj(att_pm, z, x_prompt, w_out_bf, ln_g, ln_b, tm=1024)

    q_s, k_s, v_s, sg_s, z_s, new_conv_sample = _sample_inproj(
        x_sample.reshape(nb * ts, d), w_in_bf, conv_w, state_conv, nb=nb, ts=ts, d_att=d_att, d_conv=d_conv,
        q_scale=q_scale)
    three = lambda a: a.reshape(nb, ts, d_att)
    new_kt_s, new_vt_s, att_s = _sample_attn(
        to_time_minor(cache_k), to_time_minor(cache_v),
        three(q_s), three(k_s), three(v_s), three(sg_s), n_heads=n_heads, head_dim=head_dim)
    y_sample = _outproj(att_s, z_s.reshape(1, nb * ts, d_conv), x_sample.reshape(1, nb * ts, d),
                        w_out_bf, ln_g, ln_b, tm=nb * ts)

    return (y_prompt, y_sample.reshape(nb, ts, d),
            from_time_minor(kt_new), from_time_minor(vt_new), new_conv_prompt,
            from_time_minor(new_kt_s), from_time_minor(new_vt_s), new_conv_sample)
```

```python
import functools
import math

import numpy as np
import jax
import jax.numpy as jnp
from jax import lax
from jax.experimental import pallas as pl
from jax.experimental.pallas import tpu as pltpu

DILATED_PAIRS = ((128, 1), (512, 4), (2048, 16))
LN_EPS = 1e-5
DEPTH = 1
DEEPNORM_ALPHA = (2.0 * DEPTH) ** 0.25

LANES = 128
SUBLANES = 8
Q_BLOCK = 128
NEG_BIG = -1e30
ATTN_PARTS = 2
OUTPROJ_CHUNKS = 2

MIB = 1024 * 1024


def _dilations():
    dils = tuple(sorted(d for _, d in DILATED_PAIRS))
    assert dils[0] == 1 and all(b % a == 0 for a, b in zip(dils, dils[1:]))
    return dils


def _silu(x):
    return x * (1.0 / (1.0 + jnp.exp(-x)))


def _emit_streams(val, refs, dils, scr, tm):
    n_pairs = val.shape[1] // LANES
    for p in range(n_pairs):
        col = val[:, p * LANES:(p + 1) * LANES]
        refs[0][0, p, 0] = col.astype(jnp.bfloat16)
        if len(dils) > 1:
            scr[0][...] = col
        for lvl in range(1, len(dils)):
            d_prev = dils[lvl - 1]
            f = dils[lvl] // d_prev
            n_prev = tm // d_prev
            n = n_prev // f
            src, dst = scr[(lvl - 1) % 2], scr[lvl % 2]
            for s in range(d_prev):
                for r2 in range(f):
                    r = s + d_prev * r2
                    rows = src[pl.ds(s * n_prev + r2, n, stride=f), :]
                    refs[lvl][0, p, r] = rows.astype(jnp.bfloat16)
                    if lvl + 1 < len(dils):
                        dst[r * n:(r + 1) * n, :] = rows


def _inproj_kernel(*refs, tm, d_att, d_conv, n_pairs, q_scale, dils):
    x_ref, w_ref, cw_ref, hist_ref = refs[:4]
    n_d = len(dils)
    q_refs = refs[4:4 + n_d]
    k_refs = refs[4 + n_d:4 + 2 * n_d]
    v_refs = refs[4 + 2 * n_d:4 + 3 * n_d]
    kt_ref, vt_ref, sg_ref, z_ref, nc_ref, u_scr, s_a, s_b = refs[4 + 3 * n_d:]

    @pl.when(pl.program_id(1) == 0)
    def _():
        u_scr[SUBLANES - 2:SUBLANES, :] = hist_ref[0]

    x = x_ref[0].astype(jnp.bfloat16)

    def proj(c0, width):
        return jnp.dot(x, w_ref[:, c0:c0 + width], preferred_element_type=jnp.float32)

    c0 = 4 * d_att
    gc = proj(c0 + d_conv, d_conv)
    hh = proj(c0 + 2 * d_conv, d_conv)
    u = gc * hh
    u_scr[SUBLANES:SUBLANES + tm, :] = u
    um2 = u_scr[SUBLANES - 2:SUBLANES - 2 + tm, :]
    um1 = u_scr[SUBLANES - 1:SUBLANES - 1 + tm, :]
    y_conv = cw_ref[0:1, :] * um2 + cw_ref[1:2, :] * um1 + cw_ref[2:3, :] * u
    tail = u[tm - 2:tm, :]
    u_scr[SUBLANES - 2:SUBLANES, :] = tail
    nc_ref[0] = tail
    gb = proj(c0, d_conv)
    g_conv = proj(c0 + 3 * d_conv, d_conv)
    z_ref[0] = (gb * y_conv * _silu(g_conv)).astype(jnp.bfloat16)

    sg = _silu(proj(3 * d_att, d_att)).astype(jnp.bfloat16)
    for p in range(n_pairs):
        sg_ref[0, p] = sg[:, p * LANES:(p + 1) * LANES]
    k = proj(d_att, d_att)
    kt_ref[0] = k.T
    _emit_streams(k, k_refs, dils, (s_a, s_b), tm)
    v = proj(2 * d_att, d_att)
    vt_ref[0] = v.T
    _emit_streams(v, v_refs, dils, (s_a, s_b), tm)
    q = proj(0, d_att) * q_scale
    _emit_streams(q, q_refs, dils, (s_a, s_b), tm)


def _inproj(x, w_in_bf, conv_w, hist, *, tm, d_att, d_conv, q_scale):
    b, t, d = x.shape
    n_pairs = d_att // LANES
    n_out = w_in_bf.shape[1]
    dils = _dilations()
    assert tm % (dils[-1] * 2 * SUBLANES) == 0
    kern = functools.partial(_inproj_kernel, tm=tm, d_att=d_att, d_conv=d_conv,
                             n_pairs=n_pairs, q_scale=q_scale, dils=dils)
    st_shapes = [jax.ShapeDtypeStruct((b, n_pairs, dd, t // dd, LANES), jnp.bfloat16) for dd in dils]
    st_specs = [pl.BlockSpec((1, n_pairs, dd, tm // dd, LANES), lambda i, j: (i, 0, 0, j, 0)) for dd in dils]
    pm = jax.ShapeDtypeStruct((b, n_pairs, t, LANES), jnp.bfloat16)
    pm_spec = pl.BlockSpec((1, n_pairs, tm, LANES), lambda i, j: (i, 0, j, 0))
    row_spec = lambda c: pl.BlockSpec((1, tm, c), lambda i, j: (i, j, 0))
    col_spec = pl.BlockSpec((1, d_att, tm), lambda i, j: (i, 0, j))
    vmem = (2 * d * n_out * 2 + 2 * tm * d * 4
            + 2 * tm * ((3 * len(dils) + 1) * d_att * 2 + 2 * d_att * 4 + d_conv * 2)
            + (tm + SUBLANES) * d_conv * 4 + 2 * tm * LANES * 4 + 12 * tm * d_att * 4)
    return pl.pallas_call(
        kern,
        grid=(b, t // tm),
        in_specs=[
            row_spec(d),
            pl.BlockSpec((d, n_out), lambda i, j: (0, 0)),
            pl.BlockSpec(conv_w.shape, lambda i, j: (0, 0)),
            pl.BlockSpec((1, 2, d_conv), lambda i, j: (i, 0, 0)),
        ],
        out_specs=st_specs * 3 + [col_spec, col_spec, pm_spec,
                                  row_spec(d_conv), pl.BlockSpec((1, 2, d_conv), lambda i, j: (i, 0, 0))],
        out_shape=st_shapes * 3 + [
            jax.ShapeDtypeStruct((b, d_att, t), jnp.float32),
            jax.ShapeDtypeStruct((b, d_att, t), jnp.float32),
            pm,
            jax.ShapeDtypeStruct((b, t, d_conv), jnp.bfloat16),
            jax.ShapeDtypeStruct((b, 2, d_conv), jnp.float32)],
        scratch_shapes=[pltpu.VMEM((tm + SUBLANES, d_conv), jnp.float32),
                        pltpu.VMEM((tm, LANES), jnp.float32),
                        pltpu.VMEM((tm, LANES), jnp.float32)],
        compiler_params=pltpu.CompilerParams(
            dimension_semantics=("parallel", "arbitrary"), vmem_limit_bytes=int(vmem)),
        name="prompt_inproj",
    )(x, w_in_bf, conv_w, hist)


def _attn_block(q, kwin, vwin, bias, lo):
    zero = jnp.zeros_like(q)
    q2 = jnp.concatenate([jnp.where(lo, q, zero), jnp.where(lo, zero, q)], axis=0)
    s = lax.dot_general(q2, kwin, (((1,), (1,)), ((), ())), preferred_element_type=jnp.float32)
    ps, ls, lses = [], [], []
    for h in range(2):
        sh = s[h * Q_BLOCK:(h + 1) * Q_BLOCK] + bias
        m = jnp.max(sh, axis=-1, keepdims=True)
        p = jnp.exp(sh - m)
        l = jnp.sum(p, axis=-1, keepdims=True)
        ps.append(p.astype(jnp.bfloat16))
        ls.append(l)
        lses.append(m + jnp.log(l))
    pv = jnp.dot(jnp.concatenate(ps, axis=0), vwin, preferred_element_type=jnp.float32)
    o = jnp.where(lo, pv[:Q_BLOCK] * (1.0 / ls[0]), pv[Q_BLOCK:] * (1.0 / ls[1]))
    lse = jnp.where(lo, lses[0], lses[1])
    return o, lse


def _branch_multiplicity(n_q, n_cache):
    i = np.arange(n_q)[:, None]
    pos = np.arange(n_cache + n_q)[None, :]
    dist = n_cache + i - pos
    mult = np.zeros(dist.shape, np.float32)
    for window, dil in DILATED_PAIRS:
        mult += (dist >= 0) & (dist <= window) & (dist % dil == 0)
    return mult


def _sample_unit(ck_ref, cv_ref, q_ref, kn_ref, vn_ref, sg_ref, mc_ref, mn_ref,
                 nk_ref, nv_ref, att_ref, *, n_heads, head_dim, ts, n_cache, n_pairs):
    d_att = n_heads * head_dim
    rows = n_heads * ts
    head_of_lane = lax.broadcasted_iota(jnp.int32, (rows, d_att), 1) // head_dim
    head_of_row = lax.broadcasted_iota(jnp.int32, (rows, d_att), 0) // ts
    own = head_of_lane == head_of_row

    q = q_ref[0]
    qx = jnp.where(own, jnp.concatenate([q] * n_heads, axis=0), 0.0).astype(jnp.bfloat16)
    kct = ck_ref[0]
    vct = cv_ref[0]
    pad = jnp.zeros((LANES - ts, d_att), jnp.float32)
    knt = jnp.concatenate([pad, kn_ref[0]], axis=0).T
    vnt = jnp.concatenate([pad, vn_ref[0]], axis=0).T
    nt = (((1,), (1,)), ((), ()))
    s_c = jnp.dot(qx, kct.astype(jnp.bfloat16), preferred_element_type=jnp.float32)
    s_n = jnp.dot(qx, knt.astype(jnp.bfloat16), preferred_element_type=jnp.float32)
    mc = mc_ref[...]
    mn = mn_ref[...]
    s_c = jnp.where(mc > 0, s_c, NEG_BIG)
    s_n = jnp.where(mn > 0, s_n, NEG_BIG)
    m = jnp.maximum(jnp.max(s_c, axis=-1, keepdims=True), jnp.max(s_n, axis=-1, keepdims=True))
    e_c = jnp.exp(s_c - m) * mc
    e_n = jnp.exp(s_n - m) * mn
    l = jnp.sum(e_c, axis=-1, keepdims=True) + jnp.sum(e_n, axis=-1, keepdims=True)
    ox = (lax.dot_general(e_c.astype(jnp.bfloat16), vct.astype(jnp.bfloat16), nt,
                          preferred_element_type=jnp.float32)
          + lax.dot_general(e_n.astype(jnp.bfloat16), vnt.astype(jnp.bfloat16), nt,
                            preferred_element_type=jnp.float32))
    ox = jnp.where(own, ox * (1.0 / l), 0.0)
    o = ox[0:ts]
    for h in range(1, n_heads):
        o = o + ox[h * ts:(h + 1) * ts]
    att = o * sg_ref[0]
    for p in range(n_pairs):
        att_ref[0, p] = att[:, p * LANES:(p + 1) * LANES]

    main = n_cache - LANES
    is_old = lax.broadcasted_iota(jnp.int32, (d_att, LANES), 1) < LANES - ts
    for src, new_t, dst in ((kct, knt, nk_ref), (vct, vnt, nv_ref)):
        rolled = pltpu.roll(src, n_cache - ts, axis=1)
        dst[0, :, 0:main] = rolled[:, 0:main]
        dst[0, :, main:n_cache] = jnp.where(is_old, rolled[:, main:n_cache], new_t)


def _split_streams(dils, seq, n_parts):
    items = sorted(((seq // d // Q_BLOCK, i, r) for i, d in enumerate(dils) for r in range(d)), reverse=True)
    parts, loads = [[] for _ in range(n_parts)], [0] * n_parts
    for cost, i, r in items:
        j = loads.index(min(loads))
        parts[j].append((i, r))
        loads[j] += cost
    return parts


def _attention_kernel(*refs, seq, dils, head_dim, parts, sample_cfg):
    n_br = len(dils)
    qkv = [refs[3 * i:3 * i + 3] for i in range(n_br)]
    sg_ref = refs[3 * n_br]
    sample_in = refs[3 * n_br + 1:3 * n_br + 9]
    att_ref = refs[3 * n_br + 9]
    sample_out = refs[3 * n_br + 10:3 * n_br + 13]
    scr = refs[3 * n_br + 13:]
    o_scr = scr[0::2]
    l_scr = scr[1::2]
    part = pl.program_id(2)

    lane = lax.broadcasted_iota(jnp.int32, (Q_BLOCK, LANES), 1)
    lo = lane < head_dim
    row = lax.broadcasted_iota(jnp.int32, (Q_BLOCK, 2 * Q_BLOCK), 0)
    col = lax.broadcasted_iota(jnp.int32, (Q_BLOCK, 2 * Q_BLOCK), 1)
    band = jnp.where((col >= row) & (col <= row + Q_BLOCK), 0.0, NEG_BIG).astype(jnp.float32)
    row1 = lax.broadcasted_iota(jnp.int32, (Q_BLOCK, Q_BLOCK), 0)
    col1 = lax.broadcasted_iota(jnp.int32, (Q_BLOCK, Q_BLOCK), 1)
    causal = jnp.where(col1 <= row1, 0.0, NEG_BIG).astype(jnp.float32)

    def stream(i, r):
        q_ref, k_ref, v_ref = qkv[i]
        d, o_ref, l_ref = dils[i], o_scr[i], l_scr[i]
        n_blk = seq // d // Q_BLOCK

        def store(row0, o, lse):
            idx = pl.ds(row0, Q_BLOCK) if d == 1 else pl.ds(r + d * row0, Q_BLOCK, stride=d)
            o_ref[idx, :] = o
            l_ref[idx, :] = lse

        o, lse = _attn_block(q_ref[r, 0:Q_BLOCK, :], k_ref[r, 0:Q_BLOCK, :], v_ref[r, 0:Q_BLOCK, :],
                             causal, lo)
        store(0, o, lse)
        for qb in range(1, n_blk):
            row0 = qb * Q_BLOCK
            rows2 = slice(row0 - Q_BLOCK, row0 + Q_BLOCK)
            o, lse = _attn_block(q_ref[r, row0:row0 + Q_BLOCK, :], k_ref[r, rows2, :], v_ref[r, rows2, :],
                                 band, lo)
            store(row0, o, lse)

    def merge(i, carry):
        rows = pl.ds(pl.multiple_of(i * Q_BLOCK, Q_BLOCK), Q_BLOCK)
        lses = [l_ref[rows, :] for l_ref in l_scr]
        top = functools.reduce(jnp.maximum, lses)
        es = [jnp.exp(x - top) for x in lses]
        num = functools.reduce(lambda a, b: a + b, [e * o_ref[rows, :] for e, o_ref in zip(es, o_scr)])
        den = functools.reduce(lambda a, b: a + b, es)
        att = num * (1.0 / den) * sg_ref[rows, :].astype(jnp.float32)
        att_ref[rows, :] = att.astype(att_ref.dtype)
        return carry

    for idx, items in enumerate(parts):
        @pl.when(part == idx)
        def _(idx=idx, items=items):
            _sample_unit(*sample_in, *sample_out, **sample_cfg)
            for i, r in items:
                stream(i, r)
            if idx == len(parts) - 1:
                lax.fori_loop(0, seq // Q_BLOCK, merge, 0)


def _attention(q_st, k_st, v_st, sg_pm, cache_kt, cache_vt, q_s, kn_s, vn_s, sg_s, *, n_heads, head_dim):
    b, n_pairs, t, _ = sg_pm.shape
    nb, d_att, n_cache = cache_kt.shape
    ts = q_s.shape[1]
    dils = _dilations()
    for w, d in DILATED_PAIRS:
        assert w == d * Q_BLOCK and t % (d * Q_BLOCK) == 0
    n_parts = ATTN_PARTS
    steps = b * n_pairs * n_parts
    assert steps % nb == 0 and n_heads % (steps // nb) == 0
    groups = steps // nb
    heads_u = n_heads // groups
    d_u = heads_u * head_dim
    assert d_u % LANES == 0
    pairs_u = d_u // LANES
    rows_u = heads_u * ts

    def unit(i, j, k):
        u = (i * n_pairs + j) * n_parts + k
        return u // groups, u % groups

    ins, specs = [], []
    for i, d in enumerate(dils):
        for a in (q_st[i], k_st[i], v_st[i]):
            assert a.shape == (b, n_pairs, d, t // d, LANES)
            ins.append(a)
            specs.append(pl.BlockSpec((None, None, d, t // d, LANES), lambda i, j, k: (i, j, 0, 0, 0)))
    pair_spec = pl.BlockSpec((None, None, t, LANES), lambda i, j, k: (i, j, 0, 0))
    cache_spec = pl.BlockSpec((1, d_u, n_cache), lambda i, j, k: (unit(i, j, k)[0], unit(i, j, k)[1], 0))
    row_spec = pl.BlockSpec((1, ts, d_u), lambda i, j, k: (unit(i, j, k)[0], 0, unit(i, j, k)[1]))
    mult = np.tile(_branch_multiplicity(ts, n_cache), (heads_u, 1))
    mult_c = jnp.asarray(mult[:, :n_cache])
    mult_n = jnp.asarray(np.pad(mult[:, n_cache:], ((0, 0), (LANES - ts, 0))))
    const = lambda shape: pl.BlockSpec(shape, lambda i, j, k: (0, 0))
    sample_cfg = dict(n_heads=heads_u, head_dim=head_dim, ts=ts, n_cache=n_cache, n_pairs=pairs_u)
    kern = functools.partial(_attention_kernel, seq=t, dils=dils, head_dim=head_dim,
                             parts=_split_streams(dils, t, n_parts), sample_cfg=sample_cfg)
    vmem = (2 * (3 * len(dils) + 2) * t * LANES * 2 + 2 * len(dils) * t * LANES * 4
            + 14 * d_u * n_cache * 4 + 8 * rows_u * n_cache * 4 + 12 * MIB)
    return pl.pallas_call(
        kern,
        grid=(b, n_pairs, n_parts),
        in_specs=specs + [pair_spec, cache_spec, cache_spec, row_spec, row_spec, row_spec, row_spec,
                          const((rows_u, n_cache)), const((rows_u, LANES))],
        out_specs=[pair_spec, cache_spec, cache_spec,
                   pl.BlockSpec((1, pairs_u, ts, LANES),
                                lambda i, j, k: (0, unit(i, j, k)[1], unit(i, j, k)[0], 0))],
        out_shape=[jax.ShapeDtypeStruct((b, n_pairs, t, LANES), jnp.bfloat16),
                   jax.ShapeDtypeStruct((nb, d_att, n_cache), jnp.float32),
                   jax.ShapeDtypeStruct((nb, d_att, n_cache), jnp.float32),
                   jax.ShapeDtypeStruct((1, d_att // LANES, nb * ts, LANES), jnp.float32)],
        scratch_shapes=[pltpu.VMEM((t, LANES), jnp.float32) for _ in range(2 * len(dils))],
        compiler_params=pltpu.CompilerParams(
            dimension_semantics=("parallel", "parallel", "arbitrary"), vmem_limit_bytes=int(vmem)),
        name="attention",
    )(*ins, sg_pm, cache_kt, cache_vt, q_s, kn_s, vn_s, sg_s, mult_c, mult_n)


def _outproj_kernel(att_ref, z_ref, x_ref, w_ref, g_ref, b_ref, y_ref, *, n_pairs, n_chunks):
    tm = x_ref.shape[1]
    rows = tm // n_chunks
    for c in range(n_chunks):
        sl = slice(c * rows, (c + 1) * rows)
        parts = [att_ref[0, p, sl, :].astype(jnp.bfloat16) for p in range(n_pairs)]
        mix = jnp.concatenate(parts + [z_ref[0, sl, :].astype(jnp.bfloat16)], axis=-1)
        out = jnp.dot(mix, w_ref[...], preferred_element_type=jnp.float32)
        h = DEEPNORM_ALPHA * x_ref[0, sl, :] + out
        mu = jnp.mean(h, axis=-1, keepdims=True)
        d = h - mu
        var = jnp.mean(d * d, axis=-1, keepdims=True)
        y_ref[0, sl, :] = d * lax.rsqrt(var + LN_EPS) * g_ref[...] + b_ref[...]


def _outproj(att_pm, z, x, w_out_bf, ln_g, ln_b, *, tm):
    b, t, d = x.shape
    n_pairs = att_pm.shape[1]
    d_conv = z.shape[-1]
    d_mix = w_out_bf.shape[0]
    vmem = (2 * d_mix * d * 2 + 4 * tm * d * 4 + 2 * tm * n_pairs * LANES * att_pm.dtype.itemsize
            + 2 * tm * d_conv * 2 + 6 * tm * d * 4)
    return pl.pallas_call(
        functools.partial(_outproj_kernel, n_pairs=n_pairs, n_chunks=OUTPROJ_CHUNKS),
        grid=(b, t // tm),
        in_specs=[
            pl.BlockSpec((1, n_pairs, tm, LANES), lambda i, j: (i, 0, j, 0)),
            pl.BlockSpec((1, tm, d_conv), lambda i, j: (i, j, 0)),
            pl.BlockSpec((1, tm, d), lambda i, j: (i, j, 0)),
            pl.BlockSpec((d_mix, d), lambda i, j: (0, 0)),
            pl.BlockSpec((1, d), lambda i, j: (0, 0)),
            pl.BlockSpec((1, d), lambda i, j: (0, 0)),
        ],
        out_specs=pl.BlockSpec((1, tm, d), lambda i, j: (i, j, 0)),
        out_shape=jax.ShapeDtypeStruct((b, t, d), jnp.float32),
        compiler_params=pltpu.CompilerParams(
            dimension_semantics=("parallel", "parallel"), vmem_limit_bytes=int(vmem)),
        name="outproj_layernorm",
    )(att_pm, z, x, w_out_bf, ln_g.reshape(1, d), ln_b.reshape(1, d))


def _sample_inproj_kernel(x_ref, w_ref, cw_ref, st_ref,
                          q_ref, k_ref, v_ref, sg_ref, z_ref, nc_ref, u_scr,
                          *, nb, ts, d_att, d_conv, q_scale):
    x = x_ref[...].astype(jnp.bfloat16)

    def proj(c0, width):
        return jnp.dot(x, w_ref[:, c0:c0 + width], preferred_element_type=jnp.float32)

    q_ref[...] = proj(0, d_att) * q_scale
    k_ref[...] = proj(d_att, d_att)
    v_ref[...] = proj(2 * d_att, d_att)
    sg_ref[...] = _silu(proj(3 * d_att, d_att))
    c0 = 4 * d_att
    gb = proj(c0, d_conv)
    gc = proj(c0 + d_conv, d_conv)
    hh = proj(c0 + 2 * d_conv, d_conv)
    g_conv = proj(c0 + 3 * d_conv, d_conv)
    u = (gc * hh).reshape(nb, ts, d_conv)
    u_scr[:, SUBLANES - 2:SUBLANES, :] = st_ref[...]
    u_scr[:, SUBLANES:SUBLANES + ts, :] = u
    um2 = u_scr[:, SUBLANES - 2:SUBLANES - 2 + ts, :]
    um1 = u_scr[:, SUBLANES - 1:SUBLANES - 1 + ts, :]
    cw = cw_ref[...]
    y_conv = cw[0:1, :][None] * um2 + cw[1:2, :][None] * um1 + cw[2:3, :][None] * u
    z = gb * y_conv.reshape(nb * ts, d_conv) * _silu(g_conv)
    z_ref[...] = z.astype(jnp.bfloat16)
    nc_ref[...] = u_scr[:, ts + SUBLANES - 2:ts + SUBLANES, :]


def _sample_inproj(x2, w_in_bf, conv_w, state, *, nb, ts, d_att, d_conv, q_scale):
    rows, d = x2.shape
    f32 = lambda c: jax.ShapeDtypeStruct((rows, c), jnp.float32)
    kern = functools.partial(_sample_inproj_kernel, nb=nb, ts=ts, d_att=d_att, d_conv=d_conv, q_scale=q_scale)
    vmem = d * w_in_bf.shape[1] * 2 + 16 * rows * d * 4 + 8 * MIB
    return pl.pallas_call(
        kern,
        out_shape=[f32(d_att), f32(d_att), f32(d_att), f32(d_att),
                   jax.ShapeDtypeStruct((rows, d_conv), jnp.bfloat16),
                   jax.ShapeDtypeStruct((nb, 2, d_conv), jnp.float32)],
        scratch_shapes=[pltpu.VMEM((nb, ts + SUBLANES, d_conv), jnp.float32)],
        compiler_params=pltpu.CompilerParams(vmem_limit_bytes=int(vmem)),
        name="sample_inproj",
    )(x2, w_in_bf, conv_w, state)


def kernel(x_prompt, x_sample, cache_k, cache_v, state_conv, w_in, conv_w, w_out, ln_g, ln_b):
    b, t, d = x_prompt.shape
    nb, ts, _ = x_sample.shape
    _, n_cache, n_heads, head_dim = cache_k.shape
    d_att = n_heads * head_dim
    d_conv = conv_w.shape[1]
    assert w_in.shape == (d, 4 * d_att + 4 * d_conv) and w_out.shape == (d_att + d_conv, d)
    assert d_att % LANES == 0 and LANES == 2 * head_dim and conv_w.shape[0] == 3
    max_window = max(w for w, _ in DILATED_PAIRS)
    assert n_cache == max_window and t <= max_window
    q_scale = head_dim ** -0.5
    assert math.log2(q_scale).is_integer()

    w_in_bf = w_in.astype(jnp.bfloat16)
    w_out_bf = w_out.astype(jnp.bfloat16)

    def to_time_minor(a):
        return jnp.transpose(a, (0, 2, 3, 1)).reshape(a.shape[0], d_att, a.shape[1])

    def from_time_minor(a):
        return jnp.transpose(a.reshape(a.shape[0], n_heads, head_dim, a.shape[2]), (0, 3, 1, 2))

    conv0 = jnp.zeros((b, 2, d_conv), jnp.float32)
    outs = _inproj(x_prompt, w_in_bf, conv_w, conv0, tm=512, d_att=d_att, d_conv=d_conv, q_scale=q_scale)
    n_d = len(DILATED_PAIRS)
    q_st, k_st, v_st = outs[:n_d], outs[n_d:2 * n_d], outs[2 * n_d:3 * n_d]
    kt_new, vt_new, sg_pm, z, new_conv_prompt = outs[3 * n_d:]

    q_s, k_s, v_s, sg_s, z_s, new_conv_sample = _sample_inproj(
        x_sample.reshape(nb * ts, d), w_in_bf, conv_w, state_conv, nb=nb, ts=ts, d_att=d_att, d_conv=d_conv,
        q_scale=q_scale)
    three = lambda a: a.reshape(nb, ts, d_att)

    att_pm, new_kt_s, new_vt_s, att_s = _attention(
        q_st, k_st, v_st, sg_pm, to_time_minor(cache_k), to_time_minor(cache_v),
        three(q_s), three(k_s), three(v_s), three(sg_s), n_heads=n_heads, head_dim=head_dim)

    y_prompt = _outproj(att_pm, z, x_prompt, w_out_bf, ln_g, ln_b, tm=1024)
    y_sample = _outproj(att_s, z_s.reshape(1, nb * ts, d_conv), x_sample.reshape(1, nb * ts, d),
                        w_out_bf, ln_g, ln_b, tm=nb * ts)

    return (y_prompt, y_sample.reshape(nb, ts, d),
            from_time_minor(kt_new), from_time_minor(vt_new), new_conv_prompt,
            from_time_minor(new_kt_s), from_time_minor(new_vt_s), new_conv_sample)
```

```python
import functools
import math

import numpy as np
import jax
import jax.numpy as jnp
from jax import lax
from jax.experimental import pallas as pl
from jax.experimental.pallas import tpu as pltpu

DILATED_PAIRS = ((128, 1), (512, 4), (2048, 16))
LN_EPS = 1e-5
DEPTH = 1
DEEPNORM_ALPHA = (2.0 * DEPTH) ** 0.25

LANES = 128
SUBLANES = 8
Q_BLOCK = 128
NEG_BIG = -1e30
ATTN_PARTS = 2
OUTPROJ_CHUNKS = 2

MIB = 1024 * 1024


def _dilations():
    dils = tuple(sorted(d for _, d in DILATED_PAIRS))
    assert dils[0] == 1 and all(b % a == 0 for a, b in zip(dils, dils[1:]))
    return dils


def _silu(x):
    return x * (1.0 / (1.0 + jnp.exp(-x)))


def _emit_streams(val, refs, dils, scr, tm):
    n_pairs = val.shape[1] // LANES
    for p in range(n_pairs):
        col = val[:, p * LANES:(p + 1) * LANES]
        refs[0][0, p, 0] = col.astype(jnp.bfloat16)
        if len(dils) > 1:
            scr[0][...] = col
        for lvl in range(1, len(dils)):
            d_prev = dils[lvl - 1]
            f = dils[lvl] // d_prev
            n_prev = tm // d_prev
            n = n_prev // f
            src, dst = scr[(lvl - 1) % 2], scr[lvl % 2]
            for s in range(d_prev):
                for r2 in range(f):
                    r = s + d_prev * r2
                    rows = src[pl.ds(s * n_prev + r2, n, stride=f), :]
                    refs[lvl][0, p, r] = rows.astype(jnp.bfloat16)
                    if lvl + 1 < len(dils):
                        dst[r * n:(r + 1) * n, :] = rows


def _inproj_kernel(*refs, tm, d_att, d_conv, n_pairs, q_scale, dils):
    x_ref, w_ref, cw_ref, hist_ref = refs[:4]
    n_d = len(dils)
    q_refs = refs[4:4 + n_d]
    k_refs = refs[4 + n_d:4 + 2 * n_d]
    v_refs = refs[4 + 2 * n_d:4 + 3 * n_d]
    kt_ref, vt_ref, sg_ref, z_ref, nc_ref, u_scr, s_a, s_b = refs[4 + 3 * n_d:]

    @pl.when(pl.program_id(1) == 0)
    def _():
        u_scr[SUBLANES - 2:SUBLANES, :] = hist_ref[0]

    x = x_ref[0].astype(jnp.bfloat16)

    def proj(c0, width):
        return jnp.dot(x, w_ref[:, c0:c0 + width], preferred_element_type=jnp.float32)

    c0 = 4 * d_att
    gc = proj(c0 + d_conv, d_conv)
    hh = proj(c0 + 2 * d_conv, d_conv)
    u = gc * hh
    u_scr[SUBLANES:SUBLANES + tm, :] = u
    um2 = u_scr[SUBLANES - 2:SUBLANES - 2 + tm, :]
    um1 = u_scr[SUBLANES - 1:SUBLANES - 1 + tm, :]
    y_conv = cw_ref[0:1, :] * um2 + cw_ref[1:2, :] * um1 + cw_ref[2:3, :] * u
    tail = u[tm - 2:tm, :]
    u_scr[SUBLANES - 2:SUBLANES, :] = tail
    nc_ref[0] = tail
    gb = proj(c0, d_conv)
    g_conv = proj(c0 + 3 * d_conv, d_conv)
    z_ref[0] = (gb * y_conv * _silu(g_conv)).astype(jnp.bfloat16)

    sg = _silu(proj(3 * d_att, d_att)).astype(jnp.bfloat16)
    for p in range(n_pairs):
        sg_ref[0, p] = sg[:, p * LANES:(p + 1) * LANES]
    k = proj(d_att, d_att)
    kt_ref[0] = k.T
    _emit_streams(k, k_refs, dils, (s_a, s_b), tm)
    v = proj(2 * d_att, d_att)
    vt_ref[0] = v.T
    _emit_streams(v, v_refs, dils, (s_a, s_b), tm)
    q = proj(0, d_att) * q_scale
    _emit_streams(q, q_refs, dils, (s_a, s_b), tm)


def _inproj(x, w_in_bf, conv_w, hist, *, tm, d_att, d_conv, q_scale):
    b, t, d = x.shape
    n_pairs = d_att // LANES
    n_out = w_in_bf.shape[1]
    dils = _dilations()
    assert tm % (dils[-1] * 2 * SUBLANES) == 0
    kern = functools.partial(_inproj_kernel, tm=tm, d_att=d_att, d_conv=d_conv,
                             n_pairs=n_pairs, q_scale=q_scale, dils=dils)
    st_shapes = [jax.ShapeDtypeStruct((b, n_pairs, dd, t // dd, LANES), jnp.bfloat16) for dd in dils]
    st_specs = [pl.BlockSpec((1, n_pairs, dd, tm // dd, LANES), lambda i, j: (i, 0, 0, j, 0)) for dd in dils]
    pm = jax.ShapeDtypeStruct((b, n_pairs, t, LANES), jnp.bfloat16)
    pm_spec = pl.BlockSpec((1, n_pairs, tm, LANES), lambda i, j: (i, 0, j, 0))
    row_spec = lambda c: pl.BlockSpec((1, tm, c), lambda i, j: (i, j, 0))
    col_spec = pl.BlockSpec((1, d_att, tm), lambda i, j: (i, 0, j))
    vmem = (2 * d * n_out * 2 + 2 * tm * d * 4
            + 2 * tm * ((3 * len(dils) + 1) * d_att * 2 + 2 * d_att * 4 + d_conv * 2)
            + (tm + SUBLANES) * d_conv * 4 + 2 * tm * LANES * 4 + 12 * tm * d_att * 4)
    return pl.pallas_call(
        kern,
        grid=(b, t // tm),
        in_specs=[
            row_spec(d),
            pl.BlockSpec((d, n_out), lambda i, j: (0, 0)),
            pl.BlockSpec(conv_w.shape, lambda i, j: (0, 0)),
            pl.BlockSpec((1, 2, d_conv), lambda i, j: (i, 0, 0)),
        ],
        out_specs=st_specs * 3 + [col_spec, col_spec, pm_spec,
                                  row_spec(d_conv), pl.BlockSpec((1, 2, d_conv), lambda i, j: (i, 0, 0))],
        out_shape=st_shapes * 3 + [
            jax.ShapeDtypeStruct((b, d_att, t), jnp.float32),
            jax.ShapeDtypeStruct((b, d_att, t), jnp.float32),
            pm,
            jax.ShapeDtypeStruct((b, t, d_conv), jnp.bfloat16),
            jax.ShapeDtypeStruct((b, 2, d_conv), jnp.float32)],
        scratch_shapes=[pltpu.VMEM((tm + SUBLANES, d_conv), jnp.float32),
                        pltpu.VMEM((tm, LANES), jnp.float32),
                        pltpu.VMEM((tm, LANES), jnp.float32)],
        compiler_params=pltpu.CompilerParams(
            dimension_semantics=("parallel", "arbitrary"), vmem_limit_bytes=int(vmem)),
        name="prompt_inproj",
    )(x, w_in_bf, conv_w, hist)


def _attn_block(q, kwin, vwin, bias, lo):
    zero = jnp.zeros_like(q)
    q2 = jnp.concatenate([jnp.where(lo, q, zero), jnp.where(lo, zero, q)], axis=0)
    s = lax.dot_general(q2, kwin, (((1,), (1,)), ((), ())), preferred_element_type=jnp.float32)
    ps, ms = [], []
    for h in range(2):
        sh = s[h * Q_BLOCK:(h + 1) * Q_BLOCK] + bias
        m = jnp.max(sh, axis=-1, keepdims=True)
        ps.append(jnp.exp2(sh - m).astype(jnp.bfloat16))
        ms.append(m)
    lo_v = lax.broadcasted_iota(jnp.int32, vwin.shape, 1) < LANES // 2
    one = jnp.ones_like(vwin)
    pv0 = jnp.dot(ps[0], jnp.where(lo_v, vwin, one), preferred_element_type=jnp.float32)
    pv1 = jnp.dot(ps[1], jnp.where(lo_v, one, vwin), preferred_element_type=jnp.float32)
    acc = jnp.where(lo, pv0, pv1)
    l = pltpu.roll(jnp.where(lo, pv1, pv0), LANES // 2, axis=1)
    return acc, jnp.where(lo, ms[0], ms[1]), l


def _branch_multiplicity(n_q, n_cache):
    i = np.arange(n_q)[:, None]
    pos = np.arange(n_cache + n_q)[None, :]
    dist = n_cache + i - pos
    mult = np.zeros(dist.shape, np.float32)
    for window, dil in DILATED_PAIRS:
        mult += (dist >= 0) & (dist <= window) & (dist % dil == 0)
    return mult


def _sample_unit(ck_ref, cv_ref, q_ref, kn_ref, vn_ref, sg_ref, mc_ref, mn_ref,
                 nk_ref, nv_ref, att_ref, *, n_heads, head_dim, ts, n_cache, n_pairs):
    d_att = n_heads * head_dim
    rows = n_heads * ts
    head_of_lane = lax.broadcasted_iota(jnp.int32, (rows, d_att), 1) // head_dim
    head_of_row = lax.broadcasted_iota(jnp.int32, (rows, d_att), 0) // ts
    own = head_of_lane == head_of_row

    q = q_ref[0]
    qx = jnp.where(own, jnp.concatenate([q] * n_heads, axis=0), 0.0).astype(jnp.bfloat16)
    kct = ck_ref[0]
    vct = cv_ref[0]
    pad = jnp.zeros((LANES - ts, d_att), jnp.float32)
    knt = jnp.concatenate([pad, kn_ref[0]], axis=0).T
    vnt = jnp.concatenate([pad, vn_ref[0]], axis=0).T
    nt = (((1,), (1,)), ((), ()))
    s_c = jnp.dot(qx, kct.astype(jnp.bfloat16), preferred_element_type=jnp.float32)
    s_n = jnp.dot(qx, knt.astype(jnp.bfloat16), preferred_element_type=jnp.float32)
    mc = mc_ref[...]
    mn = mn_ref[...]
    s_c = jnp.where(mc > 0, s_c, NEG_BIG)
    s_n = jnp.where(mn > 0, s_n, NEG_BIG)
    m = jnp.maximum(jnp.max(s_c, axis=-1, keepdims=True), jnp.max(s_n, axis=-1, keepdims=True))
    e_c = jnp.exp2(s_c - m) * mc
    e_n = jnp.exp2(s_n - m) * mn
    l = jnp.sum(e_c, axis=-1, keepdims=True) + jnp.sum(e_n, axis=-1, keepdims=True)
    ox = (lax.dot_general(e_c.astype(jnp.bfloat16), vct.astype(jnp.bfloat16), nt,
                          preferred_element_type=jnp.float32)
          + lax.dot_general(e_n.astype(jnp.bfloat16), vnt.astype(jnp.bfloat16), nt,
                            preferred_element_type=jnp.float32))
    ox = jnp.where(own, ox * (1.0 / l), 0.0)
    o = ox[0:ts]
    for h in range(1, n_heads):
        o = o + ox[h * ts:(h + 1) * ts]
    att = o * sg_ref[0]
    for p in range(n_pairs):
        att_ref[0, p] = att[:, p * LANES:(p + 1) * LANES]

    main = n_cache - LANES
    is_old = lax.broadcasted_iota(jnp.int32, (d_att, LANES), 1) < LANES - ts
    for src, new_t, dst in ((kct, knt, nk_ref), (vct, vnt, nv_ref)):
        rolled = pltpu.roll(src, n_cache - ts, axis=1)
        dst[0, :, 0:main] = rolled[:, 0:main]
        dst[0, :, main:n_cache] = jnp.where(is_old, rolled[:, main:n_cache], new_t)


def _split_streams(dils, seq, n_parts):
    items = sorted(((seq // d // Q_BLOCK, i, r) for i, d in enumerate(dils) for r in range(d)), reverse=True)
    parts, loads = [[] for _ in range(n_parts)], [0] * n_parts
    for cost, i, r in items:
        j = loads.index(min(loads))
        parts[j].append((i, r))
        loads[j] += cost
    return parts


def _attention_kernel(*refs, seq, dils, head_dim, parts, sample_cfg):
    n_br = len(dils)
    qkv = [refs[3 * i:3 * i + 3] for i in range(n_br)]
    sg_ref = refs[3 * n_br]
    sample_in = refs[3 * n_br + 1:3 * n_br + 9]
    att_ref = refs[3 * n_br + 9]
    sample_out = refs[3 * n_br + 10:3 * n_br + 13]
    scr = refs[3 * n_br + 13:]
    a_scr, m_scr, l_scr = scr[0::3], scr[1::3], scr[2::3]
    part = pl.program_id(2)

    lane = lax.broadcasted_iota(jnp.int32, (Q_BLOCK, LANES), 1)
    lo = lane < head_dim
    row = lax.broadcasted_iota(jnp.int32, (Q_BLOCK, 2 * Q_BLOCK), 0)
    col = lax.broadcasted_iota(jnp.int32, (Q_BLOCK, 2 * Q_BLOCK), 1)
    band = jnp.where((col >= row) & (col <= row + Q_BLOCK), 0.0, NEG_BIG).astype(jnp.float32)
    row1 = lax.broadcasted_iota(jnp.int32, (Q_BLOCK, Q_BLOCK), 0)
    col1 = lax.broadcasted_iota(jnp.int32, (Q_BLOCK, Q_BLOCK), 1)
    causal = jnp.where(col1 <= row1, 0.0, NEG_BIG).astype(jnp.float32)

    def stream(i, r):
        q_ref, k_ref, v_ref = qkv[i]
        d = dils[i]
        n_blk = seq // d // Q_BLOCK

        def store(row0, vals):
            idx = pl.ds(row0, Q_BLOCK) if d == 1 else pl.ds(r + d * row0, Q_BLOCK, stride=d)
            for ref, val in zip((a_scr[i], m_scr[i], l_scr[i]), vals):
                ref[idx, :] = val

        store(0, _attn_block(q_ref[r, 0:Q_BLOCK, :], k_ref[r, 0:Q_BLOCK, :], v_ref[r, 0:Q_BLOCK, :],
                             causal, lo))
        for qb in range(1, n_blk):
            row0 = qb * Q_BLOCK
            rows2 = slice(row0 - Q_BLOCK, row0 + Q_BLOCK)
            store(row0, _attn_block(q_ref[r, row0:row0 + Q_BLOCK, :], k_ref[r, rows2, :], v_ref[r, rows2, :],
                                    band, lo))

    def merge(i, carry):
        rows = pl.ds(pl.multiple_of(i * Q_BLOCK, Q_BLOCK), Q_BLOCK)
        ms = [m_ref[rows, :] for m_ref in m_scr]
        top = functools.reduce(jnp.maximum, ms)
        ws = [jnp.exp2(m - top) for m in ms]
        num = functools.reduce(lambda a, b: a + b, [w * a_ref[rows, :] for w, a_ref in zip(ws, a_scr)])
        den = functools.reduce(lambda a, b: a + b, [w * l_ref[rows, :] for w, l_ref in zip(ws, l_scr)])
        att = num * (1.0 / den) * sg_ref[rows, :].astype(jnp.float32)
        att_ref[rows, :] = att.astype(att_ref.dtype)
        return carry

    for idx, items in enumerate(parts):
        @pl.when(part == idx)
        def _(idx=idx, items=items):
            _sample_unit(*sample_in, *sample_out, **sample_cfg)
            for i, r in items:
                stream(i, r)
            if idx == len(parts) - 1:
                lax.fori_loop(0, seq // Q_BLOCK, merge, 0)


def _attention(q_st, k_st, v_st, sg_pm, cache_kt, cache_vt, q_s, kn_s, vn_s, sg_s, *, n_heads, head_dim):
    b, n_pairs, t, _ = sg_pm.shape
    nb, d_att, n_cache = cache_kt.shape
    ts = q_s.shape[1]
    dils = _dilations()
    for w, d in DILATED_PAIRS:
        assert w == d * Q_BLOCK and t % (d * Q_BLOCK) == 0
    n_parts = ATTN_PARTS
    steps = b * n_pairs * n_parts
    assert steps % nb == 0 and n_heads % (steps // nb) == 0
    groups = steps // nb
    heads_u = n_heads // groups
    d_u = heads_u * head_dim
    assert d_u % LANES == 0
    pairs_u = d_u // LANES
    rows_u = heads_u * ts

    def unit(i, j, k):
        u = (i * n_pairs + j) * n_parts + k
        return u // groups, u % groups

    ins, specs = [], []
    for i, d in enumerate(dils):
        for a in (q_st[i], k_st[i], v_st[i]):
            assert a.shape == (b, n_pairs, d, t // d, LANES)
            ins.append(a)
            specs.append(pl.BlockSpec((None, None, d, t // d, LANES), lambda i, j, k: (i, j, 0, 0, 0)))
    pair_spec = pl.BlockSpec((None, None, t, LANES), lambda i, j, k: (i, j, 0, 0))
    cache_spec = pl.BlockSpec((1, d_u, n_cache), lambda i, j, k: (unit(i, j, k)[0], unit(i, j, k)[1], 0))
    row_spec = pl.BlockSpec((1, ts, d_u), lambda i, j, k: (unit(i, j, k)[0], 0, unit(i, j, k)[1]))
    mult = np.tile(_branch_multiplicity(ts, n_cache), (heads_u, 1))
    mult_c = jnp.asarray(mult[:, :n_cache])
    mult_n = jnp.asarray(np.pad(mult[:, n_cache:], ((0, 0), (LANES - ts, 0))))
    const = lambda shape: pl.BlockSpec(shape, lambda i, j, k: (0, 0))
    sample_cfg = dict(n_heads=heads_u, head_dim=head_dim, ts=ts, n_cache=n_cache, n_pairs=pairs_u)
    kern = functools.partial(_attention_kernel, seq=t, dils=dils, head_dim=head_dim,
                             parts=_split_streams(dils, t, n_parts), sample_cfg=sample_cfg)
    vmem = (2 * (3 * len(dils) + 2) * t * LANES * 2 + 3 * len(dils) * t * LANES * 4
            + 14 * d_u * n_cache * 4 + 8 * rows_u * n_cache * 4 + 12 * MIB)
    return pl.pallas_call(
        kern,
        grid=(b, n_pairs, n_parts),
        in_specs=specs + [pair_spec, cache_spec, cache_spec, row_spec, row_spec, row_spec, row_spec,
                          const((rows_u, n_cache)), const((rows_u, LANES))],
        out_specs=[pair_spec, cache_spec, cache_spec,
                   pl.BlockSpec((1, pairs_u, ts, LANES),
                                lambda i, j, k: (0, unit(i, j, k)[1], unit(i, j, k)[0], 0))],
        out_shape=[jax.ShapeDtypeStruct((b, n_pairs, t, LANES), jnp.bfloat16),
                   jax.ShapeDtypeStruct((nb, d_att, n_cache), jnp.float32),
                   jax.ShapeDtypeStruct((nb, d_att, n_cache), jnp.float32),
                   jax.ShapeDtypeStruct((1, d_att // LANES, nb * ts, LANES), jnp.float32)],
        scratch_shapes=[pltpu.VMEM((t, LANES), jnp.float32) for _ in range(3 * len(dils))],
        compiler_params=pltpu.CompilerParams(
            dimension_semantics=("parallel", "parallel", "arbitrary"), vmem_limit_bytes=int(vmem)),
        name="attention",
    )(*ins, sg_pm, cache_kt, cache_vt, q_s, kn_s, vn_s, sg_s, mult_c, mult_n)


def _outproj_kernel(att_ref, z_ref, x_ref, w_ref, g_ref, b_ref, y_ref, *, n_pairs, n_chunks):
    tm = x_ref.shape[1]
    rows = tm // n_chunks
    for c in range(n_chunks):
        sl = slice(c * rows, (c + 1) * rows)
        parts = [att_ref[0, p, sl, :].astype(jnp.bfloat16) for p in range(n_pairs)]
        mix = jnp.concatenate(parts + [z_ref[0, sl, :].astype(jnp.bfloat16)], axis=-1)
        out = jnp.dot(mix, w_ref[...], preferred_element_type=jnp.float32)
        h = DEEPNORM_ALPHA * x_ref[0, sl, :] + out
        mu = jnp.mean(h, axis=-1, keepdims=True)
        d = h - mu
        var = jnp.mean(d * d, axis=-1, keepdims=True)
        y_ref[0, sl, :] = d * lax.rsqrt(var + LN_EPS) * g_ref[...] + b_ref[...]


def _outproj(att_pm, z, x, w_out_bf, ln_g, ln_b, *, tm):
    b, t, d = x.shape
    n_pairs = att_pm.shape[1]
    d_conv = z.shape[-1]
    d_mix = w_out_bf.shape[0]
    vmem = (2 * d_mix * d * 2 + 4 * tm * d * 4 + 2 * tm * n_pairs * LANES * att_pm.dtype.itemsize
            + 2 * tm * d_conv * 2 + 6 * tm * d * 4)
    return pl.pallas_call(
        functools.partial(_outproj_kernel, n_pairs=n_pairs, n_chunks=OUTPROJ_CHUNKS),
        grid=(b, t // tm),
        in_specs=[
            pl.BlockSpec((1, n_pairs, tm, LANES), lambda i, j: (i, 0, j, 0)),
            pl.BlockSpec((1, tm, d_conv), lambda i, j: (i, j, 0)),
            pl.BlockSpec((1, tm, d), lambda i, j: (i, j, 0)),
            pl.BlockSpec((d_mix, d), lambda i, j: (0, 0)),
            pl.BlockSpec((1, d), lambda i, j: (0, 0)),
            pl.BlockSpec((1, d), lambda i, j: (0, 0)),
        ],
        out_specs=pl.BlockSpec((1, tm, d), lambda i, j: (i, j, 0)),
        out_shape=jax.ShapeDtypeStruct((b, t, d), jnp.float32),
        compiler_params=pltpu.CompilerParams(
            dimension_semantics=("parallel", "parallel"), vmem_limit_bytes=int(vmem)),
        name="outproj_layernorm",
    )(att_pm, z, x, w_out_bf, ln_g.reshape(1, d), ln_b.reshape(1, d))


def _sample_inproj_kernel(x_ref, w_ref, cw_ref, st_ref,
                          q_ref, k_ref, v_ref, sg_ref, z_ref, nc_ref, u_scr,
                          *, nb, ts, d_att, d_conv, q_scale):
    x = x_ref[...].astype(jnp.bfloat16)

    def proj(c0, width):
        return jnp.dot(x, w_ref[:, c0:c0 + width], preferred_element_type=jnp.float32)

    q_ref[...] = proj(0, d_att) * q_scale
    k_ref[...] = proj(d_att, d_att)
    v_ref[...] = proj(2 * d_att, d_att)
    sg_ref[...] = _silu(proj(3 * d_att, d_att))
    c0 = 4 * d_att
    gb = proj(c0, d_conv)
    gc = proj(c0 + d_conv, d_conv)
    hh = proj(c0 + 2 * d_conv, d_conv)
    g_conv = proj(c0 + 3 * d_conv, d_conv)
    u = (gc * hh).reshape(nb, ts, d_conv)
    u_scr[:, SUBLANES - 2:SUBLANES, :] = st_ref[...]
    u_scr[:, SUBLANES:SUBLANES + ts, :] = u
    um2 = u_scr[:, SUBLANES - 2:SUBLANES - 2 + ts, :]
    um1 = u_scr[:, SUBLANES - 1:SUBLANES - 1 + ts, :]
    cw = cw_ref[...]
    y_conv = cw[0:1, :][None] * um2 + cw[1:2, :][None] * um1 + cw[2:3, :][None] * u
    z = gb * y_conv.reshape(nb * ts, d_conv) * _silu(g_conv)
    z_ref[...] = z.astype(jnp.bfloat16)
    nc_ref[...] = u_scr[:, ts + SUBLANES - 2:ts + SUBLANES, :]


def _sample_inproj(x2, w_in_bf, conv_w, state, *, nb, ts, d_att, d_conv, q_scale):
    rows, d = x2.shape
    f32 = lambda c: jax.ShapeDtypeStruct((rows, c), jnp.float32)
    kern = functools.partial(_sample_inproj_kernel, nb=nb, ts=ts, d_att=d_att, d_conv=d_conv, q_scale=q_scale)
    vmem = d * w_in_bf.shape[1] * 2 + 16 * rows * d * 4 + 8 * MIB
    return pl.pallas_call(
        kern,
        out_shape=[f32(d_att), f32(d_att), f32(d_att), f32(d_att),
                   jax.ShapeDtypeStruct((rows, d_conv), jnp.bfloat16),
                   jax.ShapeDtypeStruct((nb, 2, d_conv), jnp.float32)],
        scratch_shapes=[pltpu.VMEM((nb, ts + SUBLANES, d_conv), jnp.float32)],
        compiler_params=pltpu.CompilerParams(vmem_limit_bytes=int(vmem)),
        name="sample_inproj",
    )(x2, w_in_bf, conv_w, state)


def kernel(x_prompt, x_sample, cache_k, cache_v, state_conv, w_in, conv_w, w_out, ln_g, ln_b):
    b, t, d = x_prompt.shape
    nb, ts, _ = x_sample.shape
    _, n_cache, n_heads, head_dim = cache_k.shape
    d_att = n_heads * head_dim
    d_conv = conv_w.shape[1]
    assert w_in.shape == (d, 4 * d_att + 4 * d_conv) and w_out.shape == (d_att + d_conv, d)
    assert d_att % LANES == 0 and LANES == 2 * head_dim and conv_w.shape[0] == 3
    max_window = max(w for w, _ in DILATED_PAIRS)
    assert n_cache == max_window and t <= max_window
    q_scale = head_dim ** -0.5 * math.log2(math.e)

    w_in_bf = w_in.astype(jnp.bfloat16)
    w_out_bf = w_out.astype(jnp.bfloat16)

    def to_time_minor(a):
        return jnp.transpose(a, (0, 2, 3, 1)).reshape(a.shape[0], d_att, a.shape[1])

    def from_time_minor(a):
        return jnp.transpose(a.reshape(a.shape[0], n_heads, head_dim, a.shape[2]), (0, 3, 1, 2))

    conv0 = jnp.zeros((b, 2, d_conv), jnp.float32)
    outs = _inproj(x_prompt, w_in_bf, conv_w, conv0, tm=512, d_att=d_att, d_conv=d_conv, q_scale=q_scale)
    n_d = len(DILATED_PAIRS)
    q_st, k_st, v_st = outs[:n_d], outs[n_d:2 * n_d], outs[2 * n_d:3 * n_d]
    kt_new, vt_new, sg_pm, z, new_conv_prompt = outs[3 * n_d:]

    q_s, k_s, v_s, sg_s, z_s, new_conv_sample = _sample_inproj(
        x_sample.reshape(nb * ts, d), w_in_bf, conv_w, state_conv, nb=nb, ts=ts, d_att=d_att, d_conv=d_conv,
        q_scale=q_scale)
    three = lambda a: a.reshape(nb, ts, d_att)

    att_pm, new_kt_s, new_vt_s, att_s = _attention(
        q_st, k_st, v_st, sg_pm, to_time_minor(cache_k), to_time_minor(cache_v),
        three(q_s), three(k_s), three(v_s), three(sg_s), n_heads=n_heads, head_dim=head_dim)

    y_prompt = _outproj(att_pm, z, x_prompt, w_out_bf, ln_g, ln_b, tm=1024)
    y_sample = _outproj(att_s, z_s.reshape(1, nb * ts, d_conv), x_sample.reshape(1, nb * ts, d),
                        w_out_bf, ln_g, ln_b, tm=nb * ts)

    return (y_prompt, y_sample.reshape(nb, ts, d),
            from_time_minor(kt_new), from_time_minor(vt_new), new_conv_prompt,
            from_time_minor(new_kt_s), from_time_minor(new_vt_s), new_conv_sample)
```

```python
import functools
import math

import numpy as np
import jax
import jax.numpy as jnp
from jax import lax
from jax.experimental import pallas as pl
from jax.experimental.pallas import tpu as pltpu

DILATED_PAIRS = ((128, 1), (512, 4), (2048, 16))
LN_EPS = 1e-5
DEPTH = 1
DEEPNORM_ALPHA = (2.0 * DEPTH) ** 0.25

LANES = 128
SUBLANES = 8
Q_BLOCK = 128
NEG_BIG = -1e30
ATTN_PARTS = 2
OUTPROJ_CHUNKS = 2
CAST_ROWS = 128

MIB = 1024 * 1024


def _dilations():
    dils = tuple(sorted(d for _, d in DILATED_PAIRS))
    assert dils[0] == 1 and all(b % a == 0 for a, b in zip(dils, dils[1:]))
    return dils


def _silu(x):
    return x * (1.0 / (1.0 + jnp.exp(-x)))


def _cast_rows(src_ref, dst_ref):
    n = src_ref.shape[0] // CAST_ROWS

    def body(i, carry):
        rows = pl.ds(pl.multiple_of(i * CAST_ROWS, CAST_ROWS), CAST_ROWS)
        dst_ref[rows, :] = src_ref[rows, :].astype(dst_ref.dtype)
        return carry

    lax.fori_loop(0, n, body, 0)


def _emit_streams(val, refs, dils, scr, tm):
    n_pairs = val.shape[1] // LANES
    for p in range(n_pairs):
        col = val[:, p * LANES:(p + 1) * LANES]
        refs[0][0, p, 0] = col.astype(jnp.bfloat16)
        if len(dils) > 1:
            scr[0][...] = col
        for lvl in range(1, len(dils)):
            d_prev = dils[lvl - 1]
            f = dils[lvl] // d_prev
            n_prev = tm // d_prev
            n = n_prev // f
            src, dst = scr[(lvl - 1) % 2], scr[lvl % 2]
            for s in range(d_prev):
                for r2 in range(f):
                    r = s + d_prev * r2
                    rows = src[pl.ds(s * n_prev + r2, n, stride=f), :]
                    refs[lvl][0, p, r] = rows.astype(jnp.bfloat16)
                    if lvl + 1 < len(dils):
                        dst[r * n:(r + 1) * n, :] = rows


def _inproj_kernel(*refs, tm, d_att, d_conv, n_pairs, q_scale, dils):
    x_ref, w_ref, cw_ref, hist_ref = refs[:4]
    n_d = len(dils)
    q_refs = refs[4:4 + n_d]
    k_refs = refs[4 + n_d:4 + 2 * n_d]
    v_refs = refs[4 + 2 * n_d:4 + 3 * n_d]
    kt_ref, vt_ref, sg_ref, z_ref, nc_ref, u_scr, s_a, s_b, w_bf = refs[4 + 3 * n_d:]

    @pl.when((pl.program_id(0) == 0) & (pl.program_id(1) == 0))
    def _():
        _cast_rows(w_ref, w_bf)

    @pl.when(pl.program_id(1) == 0)
    def _():
        u_scr[SUBLANES - 2:SUBLANES, :] = hist_ref[0]

    x = x_ref[0].astype(jnp.bfloat16)

    def proj(c0, width):
        return jnp.dot(x, w_bf[:, c0:c0 + width], preferred_element_type=jnp.float32)

    c0 = 4 * d_att
    gc = proj(c0 + d_conv, d_conv)
    hh = proj(c0 + 2 * d_conv, d_conv)
    u = gc * hh
    u_scr[SUBLANES:SUBLANES + tm, :] = u
    um2 = u_scr[SUBLANES - 2:SUBLANES - 2 + tm, :]
    um1 = u_scr[SUBLANES - 1:SUBLANES - 1 + tm, :]
    y_conv = cw_ref[0:1, :] * um2 + cw_ref[1:2, :] * um1 + cw_ref[2:3, :] * u
    tail = u[tm - 2:tm, :]
    u_scr[SUBLANES - 2:SUBLANES, :] = tail
    nc_ref[0] = tail
    gb = proj(c0, d_conv)
    g_conv = proj(c0 + 3 * d_conv, d_conv)
    z_ref[0] = (gb * y_conv * _silu(g_conv)).astype(jnp.bfloat16)

    sg = _silu(proj(3 * d_att, d_att)).astype(jnp.bfloat16)
    for p in range(n_pairs):
        sg_ref[0, p] = sg[:, p * LANES:(p + 1) * LANES]
    k = proj(d_att, d_att)
    kt_ref[0] = k.T
    _emit_streams(k, k_refs, dils, (s_a, s_b), tm)
    v = proj(2 * d_att, d_att)
    vt_ref[0] = v.T
    _emit_streams(v, v_refs, dils, (s_a, s_b), tm)
    q = proj(0, d_att) * q_scale
    _emit_streams(q, q_refs, dils, (s_a, s_b), tm)


def _inproj(x, w_in, conv_w, hist, *, tm, d_att, d_conv, q_scale):
    b, t, d = x.shape
    n_pairs = d_att // LANES
    n_out = w_in.shape[1]
    dils = _dilations()
    assert tm % (dils[-1] * 2 * SUBLANES) == 0
    kern = functools.partial(_inproj_kernel, tm=tm, d_att=d_att, d_conv=d_conv,
                             n_pairs=n_pairs, q_scale=q_scale, dils=dils)
    st_shapes = [jax.ShapeDtypeStruct((b, n_pairs, dd, t // dd, LANES), jnp.bfloat16) for dd in dils]
    st_specs = [pl.BlockSpec((1, n_pairs, dd, tm // dd, LANES), lambda i, j: (i, 0, 0, j, 0)) for dd in dils]
    pm = jax.ShapeDtypeStruct((b, n_pairs, t, LANES), jnp.bfloat16)
    pm_spec = pl.BlockSpec((1, n_pairs, tm, LANES), lambda i, j: (i, 0, j, 0))
    row_spec = lambda c: pl.BlockSpec((1, tm, c), lambda i, j: (i, j, 0))
    col_spec = pl.BlockSpec((1, d_att, tm), lambda i, j: (i, 0, j))
    vmem = (d * n_out * (4 + 2) + 2 * tm * d * 4
            + 2 * tm * ((3 * len(dils) + 1) * d_att * 2 + 2 * d_att * 4 + d_conv * 2)
            + (tm + SUBLANES) * d_conv * 4 + 2 * tm * LANES * 4 + 12 * tm * d_att * 4)
    return pl.pallas_call(
        kern,
        grid=(b, t // tm),
        in_specs=[
            row_spec(d),
            pl.BlockSpec((d, n_out), lambda i, j: (0, 0), pipeline_mode=pl.Buffered(1)),
            pl.BlockSpec(conv_w.shape, lambda i, j: (0, 0)),
            pl.BlockSpec((1, 2, d_conv), lambda i, j: (i, 0, 0)),
        ],
        out_specs=st_specs * 3 + [col_spec, col_spec, pm_spec,
                                  row_spec(d_conv), pl.BlockSpec((1, 2, d_conv), lambda i, j: (i, 0, 0))],
        out_shape=st_shapes * 3 + [
            jax.ShapeDtypeStruct((b, d_att, t), jnp.float32),
            jax.ShapeDtypeStruct((b, d_att, t), jnp.float32),
            pm,
            jax.ShapeDtypeStruct((b, t, d_conv), jnp.bfloat16),
            jax.ShapeDtypeStruct((b, 2, d_conv), jnp.float32)],
        scratch_shapes=[pltpu.VMEM((tm + SUBLANES, d_conv), jnp.float32),
                        pltpu.VMEM((tm, LANES), jnp.float32),
                        pltpu.VMEM((tm, LANES), jnp.float32),
                        pltpu.VMEM((d, n_out), jnp.bfloat16)],
        compiler_params=pltpu.CompilerParams(
            dimension_semantics=("arbitrary", "arbitrary"), vmem_limit_bytes=int(vmem)),
        name="prompt_inproj",
    )(x, w_in, conv_w, hist)


def _attn_block(q, kwin, vwin, bias, lo):
    zero = jnp.zeros_like(q)
    q2 = jnp.concatenate([jnp.where(lo, q, zero), jnp.where(lo, zero, q)], axis=0)
    s = lax.dot_general(q2, kwin, (((1,), (1,)), ((), ())), preferred_element_type=jnp.float32)
    ps, ms = [], []
    for h in range(2):
        sh = s[h * Q_BLOCK:(h + 1) * Q_BLOCK] + bias
        m = jnp.max(sh, axis=-1, keepdims=True)
        ps.append(jnp.exp2(sh - m).astype(jnp.bfloat16))
        ms.append(m)
    lo_v = lax.broadcasted_iota(jnp.int32, vwin.shape, 1) < LANES // 2
    one = jnp.ones_like(vwin)
    pv0 = jnp.dot(ps[0], jnp.where(lo_v, vwin, one), preferred_element_type=jnp.float32)
    pv1 = jnp.dot(ps[1], jnp.where(lo_v, one, vwin), preferred_element_type=jnp.float32)
    acc = jnp.where(lo, pv0, pv1)
    l = pltpu.roll(jnp.where(lo, pv1, pv0), LANES // 2, axis=1)
    return acc, jnp.where(lo, ms[0], ms[1]), l


def _branch_multiplicity(n_q, n_cache):
    i = np.arange(n_q)[:, None]
    pos = np.arange(n_cache + n_q)[None, :]
    dist = n_cache + i - pos
    mult = np.zeros(dist.shape, np.float32)
    for window, dil in DILATED_PAIRS:
        mult += (dist >= 0) & (dist <= window) & (dist % dil == 0)
    return mult


def _sample_unit(ck_ref, cv_ref, q_ref, kn_ref, vn_ref, sg_ref, mc_ref, mn_ref,
                 nk_ref, nv_ref, att_ref, *, n_heads, head_dim, ts, n_cache, n_pairs):
    d_att = n_heads * head_dim
    rows = n_heads * ts
    head_of_lane = lax.broadcasted_iota(jnp.int32, (rows, d_att), 1) // head_dim
    head_of_row = lax.broadcasted_iota(jnp.int32, (rows, d_att), 0) // ts
    own = head_of_lane == head_of_row

    q = q_ref[0]
    qx = jnp.where(own, jnp.concatenate([q] * n_heads, axis=0), 0.0).astype(jnp.bfloat16)
    kct = ck_ref[0]
    vct = cv_ref[0]
    pad = jnp.zeros((LANES - ts, d_att), jnp.float32)
    knt = jnp.concatenate([pad, kn_ref[0]], axis=0).T
    vnt = jnp.concatenate([pad, vn_ref[0]], axis=0).T
    nt = (((1,), (1,)), ((), ()))
    s_c = jnp.dot(qx, kct.astype(jnp.bfloat16), preferred_element_type=jnp.float32)
    s_n = jnp.dot(qx, knt.astype(jnp.bfloat16), preferred_element_type=jnp.float32)
    mc = mc_ref[...]
    mn = mn_ref[...]
    s_c = jnp.where(mc > 0, s_c, NEG_BIG)
    s_n = jnp.where(mn > 0, s_n, NEG_BIG)
    m = jnp.maximum(jnp.max(s_c, axis=-1, keepdims=True), jnp.max(s_n, axis=-1, keepdims=True))
    e_c = jnp.exp2(s_c - m) * mc
    e_n = jnp.exp2(s_n - m) * mn
    l = jnp.sum(e_c, axis=-1, keepdims=True) + jnp.sum(e_n, axis=-1, keepdims=True)
    ox = (lax.dot_general(e_c.astype(jnp.bfloat16), vct.astype(jnp.bfloat16), nt,
                          preferred_element_type=jnp.float32)
          + lax.dot_general(e_n.astype(jnp.bfloat16), vnt.astype(jnp.bfloat16), nt,
                            preferred_element_type=jnp.float32))
    ox = jnp.where(own, ox * (1.0 / l), 0.0)
    o = ox[0:ts]
    for h in range(1, n_heads):
        o = o + ox[h * ts:(h + 1) * ts]
    att = o * sg_ref[0]
    for p in range(n_pairs):
        att_ref[0, p] = att[:, p * LANES:(p + 1) * LANES]

    main = n_cache - LANES
    is_old = lax.broadcasted_iota(jnp.int32, (d_att, LANES), 1) < LANES - ts
    for src, new_t, dst in ((kct, knt, nk_ref), (vct, vnt, nv_ref)):
        rolled = pltpu.roll(src, n_cache - ts, axis=1)
        dst[0, :, 0:main] = rolled[:, 0:main]
        dst[0, :, main:n_cache] = jnp.where(is_old, rolled[:, main:n_cache], new_t)


def _split_streams(dils, seq, n_parts):
    items = sorted(((seq // d // Q_BLOCK, i, r) for i, d in enumerate(dils) for r in range(d)), reverse=True)
    parts, loads = [[] for _ in range(n_parts)], [0] * n_parts
    for cost, i, r in items:
        j = loads.index(min(loads))
        parts[j].append((i, r))
        loads[j] += cost
    return parts


def _attention_kernel(*refs, seq, dils, head_dim, parts, sample_cfg):
    n_br = len(dils)
    qkv = [refs[3 * i:3 * i + 3] for i in range(n_br)]
    sg_ref = refs[3 * n_br]
    sample_in = refs[3 * n_br + 1:3 * n_br + 9]
    att_ref = refs[3 * n_br + 9]
    sample_out = refs[3 * n_br + 10:3 * n_br + 13]
    scr = refs[3 * n_br + 13:]
    a_scr, m_scr, l_scr = scr[0::3], scr[1::3], scr[2::3]
    part = pl.program_id(2)

    lane = lax.broadcasted_iota(jnp.int32, (Q_BLOCK, LANES), 1)
    lo = lane < head_dim
    row = lax.broadcasted_iota(jnp.int32, (Q_BLOCK, 2 * Q_BLOCK), 0)
    col = lax.broadcasted_iota(jnp.int32, (Q_BLOCK, 2 * Q_BLOCK), 1)
    band = jnp.where((col >= row) & (col <= row + Q_BLOCK), 0.0, NEG_BIG).astype(jnp.float32)
    row1 = lax.broadcasted_iota(jnp.int32, (Q_BLOCK, Q_BLOCK), 0)
    col1 = lax.broadcasted_iota(jnp.int32, (Q_BLOCK, Q_BLOCK), 1)
    causal = jnp.where(col1 <= row1, 0.0, NEG_BIG).astype(jnp.float32)

    def stream(i, r):
        q_ref, k_ref, v_ref = qkv[i]
        d = dils[i]
        n_blk = seq // d // Q_BLOCK

        def store(row0, vals):
            idx = pl.ds(row0, Q_BLOCK) if d == 1 else pl.ds(r + d * row0, Q_BLOCK, stride=d)
            for ref, val in zip((a_scr[i], m_scr[i], l_scr[i]), vals):
                ref[idx, :] = val

        store(0, _attn_block(q_ref[r, 0:Q_BLOCK, :], k_ref[r, 0:Q_BLOCK, :], v_ref[r, 0:Q_BLOCK, :],
                             causal, lo))
        for qb in range(1, n_blk):
            row0 = qb * Q_BLOCK
            rows2 = slice(row0 - Q_BLOCK, row0 + Q_BLOCK)
            store(row0, _attn_block(q_ref[r, row0:row0 + Q_BLOCK, :], k_ref[r, rows2, :], v_ref[r, rows2, :],
                                    band, lo))

    def merge(i, carry):
        rows = pl.ds(pl.multiple_of(i * Q_BLOCK, Q_BLOCK), Q_BLOCK)
        ms = [m_ref[rows, :] for m_ref in m_scr]
        top = functools.reduce(jnp.maximum, ms)
        ws = [jnp.exp2(m - top) for m in ms]
        num = functools.reduce(lambda a, b: a + b, [w * a_ref[rows, :] for w, a_ref in zip(ws, a_scr)])
        den = functools.reduce(lambda a, b: a + b, [w * l_ref[rows, :] for w, l_ref in zip(ws, l_scr)])
        att = num * (1.0 / den) * sg_ref[rows, :].astype(jnp.float32)
        att_ref[rows, :] = att.astype(att_ref.dtype)
        return carry

    for idx, items in enumerate(parts):
        @pl.when(part == idx)
        def _(idx=idx, items=items):
            _sample_unit(*sample_in, *sample_out, **sample_cfg)
            for i, r in items:
                stream(i, r)
            if idx == len(parts) - 1:
                lax.fori_loop(0, seq // Q_BLOCK, merge, 0)


def _attention(q_st, k_st, v_st, sg_pm, cache_kt, cache_vt, q_s, kn_s, vn_s, sg_s, *, n_heads, head_dim):
    b, n_pairs, t, _ = sg_pm.shape
    nb, d_att, n_cache = cache_kt.shape
    ts = q_s.shape[1]
    dils = _dilations()
    for w, d in DILATED_PAIRS:
        assert w == d * Q_BLOCK and t % (d * Q_BLOCK) == 0
    n_parts = ATTN_PARTS
    steps = b * n_pairs * n_parts
    assert steps % nb == 0 and n_heads % (steps // nb) == 0
    groups = steps // nb
    heads_u = n_heads // groups
    d_u = heads_u * head_dim
    assert d_u % LANES == 0
    pairs_u = d_u // LANES
    rows_u = heads_u * ts

    def unit(i, j, k):
        u = (i * n_pairs + j) * n_parts + k
        return u // groups, u % groups

    ins, specs = [], []
    for i, d in enumerate(dils):
        for a in (q_st[i], k_st[i], v_st[i]):
            assert a.shape == (b, n_pairs, d, t // d, LANES)
            ins.append(a)
            specs.append(pl.BlockSpec((None, None, d, t // d, LANES), lambda i, j, k: (i, j, 0, 0, 0)))
    pair_spec = pl.BlockSpec((None, None, t, LANES), lambda i, j, k: (i, j, 0, 0))
    cache_spec = pl.BlockSpec((1, d_u, n_cache), lambda i, j, k: (unit(i, j, k)[0], unit(i, j, k)[1], 0))
    row_spec = pl.BlockSpec((1, ts, d_u), lambda i, j, k: (unit(i, j, k)[0], 0, unit(i, j, k)[1]))
    mult = np.tile(_branch_multiplicity(ts, n_cache), (heads_u, 1))
    mult_c = jnp.asarray(mult[:, :n_cache])
    mult_n = jnp.asarray(np.pad(mult[:, n_cache:], ((0, 0), (LANES - ts, 0))))
    const = lambda shape: pl.BlockSpec(shape, lambda i, j, k: (0, 0))
    sample_cfg = dict(n_heads=heads_u, head_dim=head_dim, ts=ts, n_cache=n_cache, n_pairs=pairs_u)
    kern = functools.partial(_attention_kernel, seq=t, dils=dils, head_dim=head_dim,
                             parts=_split_streams(dils, t, n_parts), sample_cfg=sample_cfg)
    vmem = (2 * (3 * len(dils) + 2) * t * LANES * 2 + 3 * len(dils) * t * LANES * 4
            + 14 * d_u * n_cache * 4 + 8 * rows_u * n_cache * 4 + 12 * MIB)
    return pl.pallas_call(
        kern,
        grid=(b, n_pairs, n_parts),
        in_specs=specs + [pair_spec, cache_spec, cache_spec, row_spec, row_spec, row_spec, row_spec,
                          const((rows_u, n_cache)), const((rows_u, LANES))],
        out_specs=[pair_spec, cache_spec, cache_spec,
                   pl.BlockSpec((1, pairs_u, ts, LANES),
                                lambda i, j, k: (0, unit(i, j, k)[1], unit(i, j, k)[0], 0))],
        out_shape=[jax.ShapeDtypeStruct((b, n_pairs, t, LANES), jnp.bfloat16),
                   jax.ShapeDtypeStruct((nb, d_att, n_cache), jnp.float32),
                   jax.ShapeDtypeStruct((nb, d_att, n_cache), jnp.float32),
                   jax.ShapeDtypeStruct((1, d_att // LANES, nb * ts, LANES), jnp.float32)],
        scratch_shapes=[pltpu.VMEM((t, LANES), jnp.float32) for _ in range(3 * len(dils))],
        compiler_params=pltpu.CompilerParams(
            dimension_semantics=("parallel", "parallel", "arbitrary"), vmem_limit_bytes=int(vmem)),
        name="attention",
    )(*ins, sg_pm, cache_kt, cache_vt, q_s, kn_s, vn_s, sg_s, mult_c, mult_n)


def _outproj_kernel(att_ref, z_ref, x_ref, w_ref, g_ref, b_ref, y_ref, w_bf, *, n_pairs, n_chunks):
    @pl.when((pl.program_id(0) == 0) & (pl.program_id(1) == 0))
    def _():
        _cast_rows(w_ref, w_bf)

    tm = x_ref.shape[1]
    rows = tm // n_chunks
    for c in range(n_chunks):
        sl = slice(c * rows, (c + 1) * rows)
        parts = [att_ref[0, p, sl, :].astype(jnp.bfloat16) for p in range(n_pairs)]
        mix = jnp.concatenate(parts + [z_ref[0, sl, :].astype(jnp.bfloat16)], axis=-1)
        out = jnp.dot(mix, w_bf[...], preferred_element_type=jnp.float32)
        h = DEEPNORM_ALPHA * x_ref[0, sl, :] + out
        mu = jnp.mean(h, axis=-1, keepdims=True)
        d = h - mu
        var = jnp.mean(d * d, axis=-1, keepdims=True)
        y_ref[0, sl, :] = d * lax.rsqrt(var + LN_EPS) * g_ref[...] + b_ref[...]


def _outproj(att_pm, z, x, w_out, ln_g, ln_b, *, tm):
    b, t, d = x.shape
    n_pairs = att_pm.shape[1]
    d_conv = z.shape[-1]
    d_mix = w_out.shape[0]
    vmem = (d_mix * d * (4 + 2) + 4 * tm * d * 4 + 2 * tm * n_pairs * LANES * att_pm.dtype.itemsize
            + 2 * tm * d_conv * 2 + 6 * tm * d * 4)
    return pl.pallas_call(
        functools.partial(_outproj_kernel, n_pairs=n_pairs, n_chunks=OUTPROJ_CHUNKS),
        grid=(b, t // tm),
        in_specs=[
            pl.BlockSpec((1, n_pairs, tm, LANES), lambda i, j: (i, 0, j, 0)),
            pl.BlockSpec((1, tm, d_conv), lambda i, j: (i, j, 0)),
            pl.BlockSpec((1, tm, d), lambda i, j: (i, j, 0)),
            pl.BlockSpec((d_mix, d), lambda i, j: (0, 0), pipeline_mode=pl.Buffered(1)),
            pl.BlockSpec((1, d), lambda i, j: (0, 0)),
            pl.BlockSpec((1, d), lambda i, j: (0, 0)),
        ],
        out_specs=pl.BlockSpec((1, tm, d), lambda i, j: (i, j, 0)),
        out_shape=jax.ShapeDtypeStruct((b, t, d), jnp.float32),
        scratch_shapes=[pltpu.VMEM((d_mix, d), jnp.bfloat16)],
        compiler_params=pltpu.CompilerParams(
            dimension_semantics=("arbitrary", "arbitrary"), vmem_limit_bytes=int(vmem)),
        name="outproj_layernorm",
    )(att_pm, z, x, w_out, ln_g.reshape(1, d), ln_b.reshape(1, d))


def _sample_inproj_kernel(x_ref, w_ref, cw_ref, st_ref,
                          q_ref, k_ref, v_ref, sg_ref, z_ref, nc_ref, u_scr,
                          *, nb, ts, d_att, d_conv, q_scale):
    x = x_ref[...].astype(jnp.bfloat16)

    def proj(c0, width):
        w = w_ref[:, c0:c0 + width].astype(jnp.bfloat16)
        return jnp.dot(x, w, preferred_element_type=jnp.float32)

    q_ref[...] = proj(0, d_att) * q_scale
    k_ref[...] = proj(d_att, d_att)
    v_ref[...] = proj(2 * d_att, d_att)
    sg_ref[...] = _silu(proj(3 * d_att, d_att))
    c0 = 4 * d_att
    gb = proj(c0, d_conv)
    gc = proj(c0 + d_conv, d_conv)
    hh = proj(c0 + 2 * d_conv, d_conv)
    g_conv = proj(c0 + 3 * d_conv, d_conv)
    u = (gc * hh).reshape(nb, ts, d_conv)
    u_scr[:, SUBLANES - 2:SUBLANES, :] = st_ref[...]
    u_scr[:, SUBLANES:SUBLANES + ts, :] = u
    um2 = u_scr[:, SUBLANES - 2:SUBLANES - 2 + ts, :]
    um1 = u_scr[:, SUBLANES - 1:SUBLANES - 1 + ts, :]
    cw = cw_ref[...]
    y_conv = cw[0:1, :][None] * um2 + cw[1:2, :][None] * um1 + cw[2:3, :][None] * u
    z = gb * y_conv.reshape(nb * ts, d_conv) * _silu(g_conv)
    z_ref[...] = z.astype(jnp.bfloat16)
    nc_ref[...] = u_scr[:, ts + SUBLANES - 2:ts + SUBLANES, :]


def _sample_inproj(x2, w_in, conv_w, state, *, nb, ts, d_att, d_conv, q_scale):
    rows, d = x2.shape
    f32 = lambda c: jax.ShapeDtypeStruct((rows, c), jnp.float32)
    kern = functools.partial(_sample_inproj_kernel, nb=nb, ts=ts, d_att=d_att, d_conv=d_conv, q_scale=q_scale)
    vmem = d * w_in.shape[1] * 4 + 16 * rows * d * 4 + 8 * MIB
    return pl.pallas_call(
        kern,
        out_shape=[f32(d_att), f32(d_att), f32(d_att), f32(d_att),
                   jax.ShapeDtypeStruct((rows, d_conv), jnp.bfloat16),
                   jax.ShapeDtypeStruct((nb, 2, d_conv), jnp.float32)],
        scratch_shapes=[pltpu.VMEM((nb, ts + SUBLANES, d_conv), jnp.float32)],
        compiler_params=pltpu.CompilerParams(vmem_limit_bytes=int(vmem)),
        name="sample_inproj",
    )(x2, w_in, conv_w, state)


def kernel(x_prompt, x_sample, cache_k, cache_v, state_conv, w_in, conv_w, w_out, ln_g, ln_b):
    b, t, d = x_prompt.shape
    nb, ts, _ = x_sample.shape
    _, n_cache, n_heads, head_dim = cache_k.shape
    d_att = n_heads * head_dim
    d_conv = conv_w.shape[1]
    assert w_in.shape == (d, 4 * d_att + 4 * d_conv) and w_out.shape == (d_att + d_conv, d)
    assert d_att % LANES == 0 and LANES == 2 * head_dim and conv_w.shape[0] == 3
    max_window = max(w for w, _ in DILATED_PAIRS)
    assert n_cache == max_window and t <= max_window
    q_scale = head_dim ** -0.5 * math.log2(math.e)

    def to_time_minor(a):
        return jnp.transpose(a, (0, 2, 3, 1)).reshape(a.shape[0], d_att, a.shape[1])

    def from_time_minor(a):
        return jnp.transpose(a.reshape(a.shape[0], n_heads, head_dim, a.shape[2]), (0, 3, 1, 2))

    conv0 = jnp.zeros((b, 2, d_conv), jnp.float32)
    outs = _inproj(x_prompt, w_in, conv_w, conv0, tm=512, d_att=d_att, d_conv=d_conv, q_scale=q_scale)
    n_d = len(DILATED_PAIRS)
    q_st, k_st, v_st = outs[:n_d], outs[n_d:2 * n_d], outs[2 * n_d:3 * n_d]
    kt_new, vt_new, sg_pm, z, new_conv_prompt = outs[3 * n_d:]

    q_s, k_s, v_s, sg_s, z_s, new_conv_sample = _sample_inproj(
        x_sample.reshape(nb * ts, d), w_in, conv_w, state_conv, nb=nb, ts=ts, d_att=d_att, d_conv=d_conv,
        q_scale=q_scale)
    three = lambda a: a.reshape(nb, ts, d_att)

    att_pm, new_kt_s, new_vt_s, att_s = _attention(
        q_st, k_st, v_st, sg_pm, to_time_minor(cache_k), to_time_minor(cache_v),
        three(q_s), three(k_s), three(v_s), three(sg_s), n_heads=n_heads, head_dim=head_dim)

    y_prompt = _outproj(att_pm, z, x_prompt, w_out, ln_g, ln_b, tm=1024)
    y_sample = _outproj(att_s, z_s.reshape(1, nb * ts, d_conv), x_sample.reshape(1, nb * ts, d),
                        w_out, ln_g, ln_b, tm=nb * ts)

    return (y_prompt, y_sample.reshape(nb, ts, d),
            from_time_minor(kt_new), from_time_minor(vt_new), new_conv_prompt,
            from_time_minor(new_kt_s), from_time_minor(new_vt_s), new_conv_sample)
```

```python
import functools
import math

import numpy as np
import jax
import jax.numpy as jnp
from jax import lax
from jax.experimental import pallas as pl
from jax.experimental.pallas import tpu as pltpu

DILATED_PAIRS = ((128, 1), (512, 4), (2048, 16))
LN_EPS = 1e-5
DEPTH = 1
DEEPNORM_ALPHA = (2.0 * DEPTH) ** 0.25

LANES = 128
SUBLANES = 8
Q_BLOCK = 128
NEG_BIG = -1e30
ATTN_PARTS = 2
OUTPROJ_CHUNKS = 2
CAST_ROWS = 128
BLOCK_GROUP = 8
L_FLOOR = 2.0 ** -100

MIB = 1024 * 1024


def _dilations():
    dils = tuple(sorted(d for _, d in DILATED_PAIRS))
    assert dils[0] == 1 and all(b % a == 0 for a, b in zip(dils, dils[1:]))
    return dils


def _silu(x):
    return x * (1.0 / (1.0 + jnp.exp(-x)))


def _cast_rows(src_ref, dst_ref):
    n = src_ref.shape[0] // CAST_ROWS

    def body(i, carry):
        rows = pl.ds(pl.multiple_of(i * CAST_ROWS, CAST_ROWS), CAST_ROWS)
        dst_ref[rows, :] = src_ref[rows, :].astype(dst_ref.dtype)
        return carry

    lax.fori_loop(0, n, body, 0)


def _emit_streams(val, refs, dils, scr, tm):
    n_pairs = val.shape[1] // LANES
    for p in range(n_pairs):
        col = val[:, p * LANES:(p + 1) * LANES]
        refs[0][0, p, 0] = col.astype(jnp.bfloat16)
        if len(dils) > 1:
            scr[0][...] = col
        for lvl in range(1, len(dils)):
            d_prev = dils[lvl - 1]
            f = dils[lvl] // d_prev
            n_prev = tm // d_prev
            n = n_prev // f
            src, dst = scr[(lvl - 1) % 2], scr[lvl % 2]
            for s in range(d_prev):
                for r2 in range(f):
                    r = s + d_prev * r2
                    rows = src[pl.ds(s * n_prev + r2, n, stride=f), :]
                    refs[lvl][0, p, r] = rows.astype(jnp.bfloat16)
                    if lvl + 1 < len(dils):
                        dst[r * n:(r + 1) * n, :] = rows


def _inproj_kernel(*refs, tm, d_att, d_conv, n_pairs, q_scale, dils):
    x_ref, w_ref, cw_ref, hist_ref = refs[:4]
    n_d = len(dils)
    q_refs = refs[4:4 + n_d]
    k_refs = refs[4 + n_d:4 + 2 * n_d]
    v_refs = refs[4 + 2 * n_d:4 + 3 * n_d]
    kt_ref, vt_ref, sg_ref, z_ref, nc_ref, u_scr, s_a, s_b, w_bf = refs[4 + 3 * n_d:]

    @pl.when((pl.program_id(0) == 0) & (pl.program_id(1) == 0))
    def _():
        _cast_rows(w_ref, w_bf)

    @pl.when(pl.program_id(1) == 0)
    def _():
        u_scr[SUBLANES - 2:SUBLANES, :] = hist_ref[0]

    x = x_ref[0].astype(jnp.bfloat16)

    def proj(c0, width):
        return jnp.dot(x, w_bf[:, c0:c0 + width], preferred_element_type=jnp.float32)

    c0 = 4 * d_att
    gc = proj(c0 + d_conv, d_conv)
    hh = proj(c0 + 2 * d_conv, d_conv)
    u = gc * hh
    u_scr[SUBLANES:SUBLANES + tm, :] = u
    um2 = u_scr[SUBLANES - 2:SUBLANES - 2 + tm, :]
    um1 = u_scr[SUBLANES - 1:SUBLANES - 1 + tm, :]
    y_conv = cw_ref[0:1, :] * um2 + cw_ref[1:2, :] * um1 + cw_ref[2:3, :] * u
    tail = u[tm - 2:tm, :]
    u_scr[SUBLANES - 2:SUBLANES, :] = tail
    nc_ref[0] = tail
    gb = proj(c0, d_conv)
    g_conv = proj(c0 + 3 * d_conv, d_conv)
    z_ref[0] = (gb * y_conv * _silu(g_conv)).astype(jnp.bfloat16)

    sg = _silu(proj(3 * d_att, d_att)).astype(jnp.bfloat16)
    for p in range(n_pairs):
        sg_ref[0, p] = sg[:, p * LANES:(p + 1) * LANES]
    k = proj(d_att, d_att)
    kt_ref[0] = k.T
    _emit_streams(k, k_refs, dils, (s_a, s_b), tm)
    v = proj(2 * d_att, d_att)
    vt_ref[0] = v.T
    _emit_streams(v, v_refs, dils, (s_a, s_b), tm)
    q = proj(0, d_att) * q_scale
    _emit_streams(q, q_refs, dils, (s_a, s_b), tm)


def _inproj(x, w_in, conv_w, hist, *, tm, d_att, d_conv, q_scale):
    b, t, d = x.shape
    n_pairs = d_att // LANES
    n_out = w_in.shape[1]
    dils = _dilations()
    assert tm % (dils[-1] * 2 * SUBLANES) == 0
    kern = functools.partial(_inproj_kernel, tm=tm, d_att=d_att, d_conv=d_conv,
                             n_pairs=n_pairs, q_scale=q_scale, dils=dils)
    st_shapes = [jax.ShapeDtypeStruct((b, n_pairs, dd, t // dd, LANES), jnp.bfloat16) for dd in dils]
    st_specs = [pl.BlockSpec((1, n_pairs, dd, tm // dd, LANES), lambda i, j: (i, 0, 0, j, 0)) for dd in dils]
    pm = jax.ShapeDtypeStruct((b, n_pairs, t, LANES), jnp.bfloat16)
    pm_spec = pl.BlockSpec((1, n_pairs, tm, LANES), lambda i, j: (i, 0, j, 0))
    row_spec = lambda c: pl.BlockSpec((1, tm, c), lambda i, j: (i, j, 0))
    col_spec = pl.BlockSpec((1, d_att, tm), lambda i, j: (i, 0, j))
    vmem = (d * n_out * (4 + 2) + 2 * tm * d * 4
            + 2 * tm * ((3 * len(dils) + 1) * d_att * 2 + 2 * d_att * 4 + d_conv * 2)
            + (tm + SUBLANES) * d_conv * 4 + 2 * tm * LANES * 4 + 12 * tm * d_att * 4)
    return pl.pallas_call(
        kern,
        grid=(b, t // tm),
        in_specs=[
            row_spec(d),
            pl.BlockSpec((d, n_out), lambda i, j: (0, 0), pipeline_mode=pl.Buffered(1)),
            pl.BlockSpec(conv_w.shape, lambda i, j: (0, 0)),
            pl.BlockSpec((1, 2, d_conv), lambda i, j: (i, 0, 0)),
        ],
        out_specs=st_specs * 3 + [col_spec, col_spec, pm_spec,
                                  row_spec(d_conv), pl.BlockSpec((1, 2, d_conv), lambda i, j: (i, 0, 0))],
        out_shape=st_shapes * 3 + [
            jax.ShapeDtypeStruct((b, d_att, t), jnp.float32),
            jax.ShapeDtypeStruct((b, d_att, t), jnp.float32),
            pm,
            jax.ShapeDtypeStruct((b, t, d_conv), jnp.bfloat16),
            jax.ShapeDtypeStruct((b, 2, d_conv), jnp.float32)],
        scratch_shapes=[pltpu.VMEM((tm + SUBLANES, d_conv), jnp.float32),
                        pltpu.VMEM((tm, LANES), jnp.float32),
                        pltpu.VMEM((tm, LANES), jnp.float32),
                        pltpu.VMEM((d, n_out), jnp.bfloat16)],
        compiler_params=pltpu.CompilerParams(
            dimension_semantics=("arbitrary", "arbitrary"), vmem_limit_bytes=int(vmem)),
        name="prompt_inproj",
    )(x, w_in, conv_w, hist)


def _block_scores(q, kwin, lo):
    zero = jnp.zeros_like(q)
    q2 = jnp.concatenate([jnp.where(lo, q, zero), jnp.where(lo, zero, q)], axis=0)
    return lax.dot_general(q2, kwin, (((1,), (1,)), ((), ())), preferred_element_type=jnp.float32)


def _block_probs(s, bias, row_exact):
    ps, ms = [], []
    for h in range(2):
        sh = s[h * Q_BLOCK:(h + 1) * Q_BLOCK] + bias
        if row_exact:
            m = jnp.max(sh, axis=-1, keepdims=True)
        else:
            m = jnp.max(jnp.max(sh, axis=0, keepdims=True), axis=1, keepdims=True)
        ps.append(jnp.exp2(sh - m).astype(jnp.bfloat16))
        ms.append(jnp.broadcast_to(m, (Q_BLOCK, 1)))
    return ps, ms


def _block_values(ps, ms, vwin, lo):
    lo_v = lax.broadcasted_iota(jnp.int32, vwin.shape, 1) < LANES // 2
    one = jnp.ones_like(vwin)
    pv0 = jnp.dot(ps[0], jnp.where(lo_v, vwin, one), preferred_element_type=jnp.float32)
    pv1 = jnp.dot(ps[1], jnp.where(lo_v, one, vwin), preferred_element_type=jnp.float32)
    acc = jnp.where(lo, pv0, pv1)
    l = pltpu.roll(jnp.where(lo, pv1, pv0), LANES // 2, axis=1)
    return acc, jnp.where(lo, ms[0], ms[1]), l


def _branch_multiplicity(n_q, n_cache):
    i = np.arange(n_q)[:, None]
    pos = np.arange(n_cache + n_q)[None, :]
    dist = n_cache + i - pos
    mult = np.zeros(dist.shape, np.float32)
    for window, dil in DILATED_PAIRS:
        mult += (dist >= 0) & (dist <= window) & (dist % dil == 0)
    return mult


def _sample_unit(ck_ref, cv_ref, q_ref, kn_ref, vn_ref, sg_ref, mc_ref, mn_ref,
                 nk_ref, nv_ref, att_ref, *, n_heads, head_dim, ts, n_cache, n_pairs):
    d_att = n_heads * head_dim
    rows = n_heads * ts
    head_of_lane = lax.broadcasted_iota(jnp.int32, (rows, d_att), 1) // head_dim
    head_of_row = lax.broadcasted_iota(jnp.int32, (rows, d_att), 0) // ts
    own = head_of_lane == head_of_row

    q = q_ref[0]
    qx = jnp.where(own, jnp.concatenate([q] * n_heads, axis=0), 0.0).astype(jnp.bfloat16)
    kct = ck_ref[0]
    vct = cv_ref[0]
    pad = jnp.zeros((LANES - ts, d_att), jnp.float32)
    knt = jnp.concatenate([pad, kn_ref[0]], axis=0).T
    vnt = jnp.concatenate([pad, vn_ref[0]], axis=0).T
    nt = (((1,), (1,)), ((), ()))
    s_c = jnp.dot(qx, kct.astype(jnp.bfloat16), preferred_element_type=jnp.float32)
    s_n = jnp.dot(qx, knt.astype(jnp.bfloat16), preferred_element_type=jnp.float32)
    mc = mc_ref[...]
    mn = mn_ref[...]
    s_c = jnp.where(mc > 0, s_c, NEG_BIG)
    s_n = jnp.where(mn > 0, s_n, NEG_BIG)
    m = jnp.maximum(jnp.max(s_c, axis=-1, keepdims=True), jnp.max(s_n, axis=-1, keepdims=True))
    e_c = jnp.exp2(s_c - m) * mc
    e_n = jnp.exp2(s_n - m) * mn
    l = jnp.sum(e_c, axis=-1, keepdims=True) + jnp.sum(e_n, axis=-1, keepdims=True)
    ox = (lax.dot_general(e_c.astype(jnp.bfloat16), vct.astype(jnp.bfloat16), nt,
                          preferred_element_type=jnp.float32)
          + lax.dot_general(e_n.astype(jnp.bfloat16), vnt.astype(jnp.bfloat16), nt,
                            preferred_element_type=jnp.float32))
    ox = jnp.where(own, ox * (1.0 / l), 0.0)
    o = ox[0:ts]
    for h in range(1, n_heads):
        o = o + ox[h * ts:(h + 1) * ts]
    att = o * sg_ref[0]
    for p in range(n_pairs):
        att_ref[0, p] = att[:, p * LANES:(p + 1) * LANES]

    main = n_cache - LANES
    is_old = lax.broadcasted_iota(jnp.int32, (d_att, LANES), 1) < LANES - ts
    for src, new_t, dst in ((kct, knt, nk_ref), (vct, vnt, nv_ref)):
        rolled = pltpu.roll(src, n_cache - ts, axis=1)
        dst[0, :, 0:main] = rolled[:, 0:main]
        dst[0, :, main:n_cache] = jnp.where(is_old, rolled[:, main:n_cache], new_t)


def _split_streams(dils, seq, n_parts):
    items = sorted(((seq // d // Q_BLOCK, i, r) for i, d in enumerate(dils) for r in range(d) if i > 0),
                   reverse=True)
    parts, loads = [[] for _ in range(n_parts)], [0] * n_parts
    loads[-1] = seq // dils[0] // Q_BLOCK
    for cost, i, r in items:
        j = loads.index(min(loads))
        parts[j].append((i, r))
        loads[j] += cost
    parts[-1].append((0, 0))
    return parts


def _attention_kernel(*refs, seq, dils, head_dim, parts, sample_cfg):
    n_br = len(dils)
    qkv = [refs[3 * i:3 * i + 3] for i in range(n_br)]
    sg_ref = refs[3 * n_br]
    sample_in = refs[3 * n_br + 1:3 * n_br + 9]
    att_ref = refs[3 * n_br + 9]
    sample_out = refs[3 * n_br + 10:3 * n_br + 13]
    scr = refs[3 * n_br + 13:]
    a_scr, m_scr, l_scr = scr[0::3], scr[1::3], scr[2::3]
    part = pl.program_id(2)

    lane = lax.broadcasted_iota(jnp.int32, (Q_BLOCK, LANES), 1)
    lo = lane < head_dim
    row = lax.broadcasted_iota(jnp.int32, (Q_BLOCK, 2 * Q_BLOCK), 0)
    col = lax.broadcasted_iota(jnp.int32, (Q_BLOCK, 2 * Q_BLOCK), 1)
    band = jnp.where((col >= row) & (col <= row + Q_BLOCK), 0.0, NEG_BIG).astype(jnp.float32)
    row1 = lax.broadcasted_iota(jnp.int32, (Q_BLOCK, Q_BLOCK), 0)
    col1 = lax.broadcasted_iota(jnp.int32, (Q_BLOCK, Q_BLOCK), 1)
    causal = jnp.where(col1 <= row1, 0.0, NEG_BIG).astype(jnp.float32)

    def merge(row0):
        rows = slice(row0, row0 + Q_BLOCK)
        ms = [m_ref[rows, :] for m_ref in m_scr]
        top = functools.reduce(jnp.maximum, ms)
        ws = [jnp.exp2(m - top) for m in ms]
        num = functools.reduce(lambda a, b: a + b, [w * a_ref[rows, :] for w, a_ref in zip(ws, a_scr)])
        den = functools.reduce(lambda a, b: a + b, [w * l_ref[rows, :] for w, l_ref in zip(ws, l_scr)])
        att = num * (1.0 / den) * sg_ref[rows, :].astype(jnp.float32)
        att_ref[rows, :] = att.astype(att_ref.dtype)

    def run_blocks(items, row_exact):
        blocks = [(i, r, qb) for i, r in items for qb in range(seq // dils[i] // Q_BLOCK)]
        l_min = None
        for g in range(0, len(blocks), BLOCK_GROUP):
            group = blocks[g:g + BLOCK_GROUP]
            keys = [slice(max(qb - 1, 0) * Q_BLOCK, (qb + 1) * Q_BLOCK) for _, _, qb in group]
            scores = [_block_scores(qkv[i][0][r, qb * Q_BLOCK:(qb + 1) * Q_BLOCK, :], qkv[i][1][r, ks, :], lo)
                      for (i, r, qb), ks in zip(group, keys)]
            probs = [_block_probs(s, causal if qb == 0 else band, row_exact)
                     for s, (_, _, qb) in zip(scores, group)]
            for (i, r, qb), ks, (ps, ms) in zip(group, keys, probs):
                vals = _block_values(ps, ms, qkv[i][2][r, ks, :], lo)
                d, row0 = dils[i], qb * Q_BLOCK
                idx = pl.ds(row0, Q_BLOCK) if d == 1 else pl.ds(r + d * row0, Q_BLOCK, stride=d)
                for ref, val in zip((a_scr[i], m_scr[i], l_scr[i]), vals):
                    ref[idx, :] = val
                l_min = vals[2] if l_min is None else jnp.minimum(l_min, vals[2])
            for i, _, qb in group:
                if i == 0:
                    merge(qb * Q_BLOCK)
        return l_min

    for idx, items in enumerate(parts):
        @pl.when(part == idx)
        def _(items=items):
            _sample_unit(*sample_in, *sample_out, **sample_cfg)
            l_min = run_blocks(items, False)

            @pl.when(jnp.min(l_min) < L_FLOOR)
            def _():
                run_blocks(items, True)


def _attention(q_st, k_st, v_st, sg_pm, cache_kt, cache_vt, q_s, kn_s, vn_s, sg_s, *, n_heads, head_dim):
    b, n_pairs, t, _ = sg_pm.shape
    nb, d_att, n_cache = cache_kt.shape
    ts = q_s.shape[1]
    dils = _dilations()
    for w, d in DILATED_PAIRS:
        assert w == d * Q_BLOCK and t % (d * Q_BLOCK) == 0
    n_parts = ATTN_PARTS
    steps = b * n_pairs * n_parts
    assert steps % nb == 0 and n_heads % (steps // nb) == 0
    groups = steps // nb
    heads_u = n_heads // groups
    d_u = heads_u * head_dim
    assert d_u % LANES == 0
    pairs_u = d_u // LANES
    rows_u = heads_u * ts

    def unit(i, j, k):
        u = (i * n_pairs + j) * n_parts + k
        return u // groups, u % groups

    ins, specs = [], []
    for i, d in enumerate(dils):
        for a in (q_st[i], k_st[i], v_st[i]):
            assert a.shape == (b, n_pairs, d, t // d, LANES)
            ins.append(a)
            specs.append(pl.BlockSpec((None, None, d, t // d, LANES), lambda i, j, k: (i, j, 0, 0, 0)))
    pair_spec = pl.BlockSpec((None, None, t, LANES), lambda i, j, k: (i, j, 0, 0))
    cache_spec = pl.BlockSpec((1, d_u, n_cache), lambda i, j, k: (unit(i, j, k)[0], unit(i, j, k)[1], 0))
    row_spec = pl.BlockSpec((1, ts, d_u), lambda i, j, k: (unit(i, j, k)[0], 0, unit(i, j, k)[1]))
    mult = np.tile(_branch_multiplicity(ts, n_cache), (heads_u, 1))
    mult_c = jnp.asarray(mult[:, :n_cache])
    mult_n = jnp.asarray(np.pad(mult[:, n_cache:], ((0, 0), (LANES - ts, 0))))
    const = lambda shape: pl.BlockSpec(shape, lambda i, j, k: (0, 0))
    sample_cfg = dict(n_heads=heads_u, head_dim=head_dim, ts=ts, n_cache=n_cache, n_pairs=pairs_u)
    kern = functools.partial(_attention_kernel, seq=t, dils=dils, head_dim=head_dim,
                             parts=_split_streams(dils, t, n_parts), sample_cfg=sample_cfg)
    vmem = (2 * (3 * len(dils) + 2) * t * LANES * 2 + 3 * len(dils) * t * LANES * 4
            + 14 * d_u * n_cache * 4 + 8 * rows_u * n_cache * 4 + 12 * MIB)
    return pl.pallas_call(
        kern,
        grid=(b, n_pairs, n_parts),
        in_specs=specs + [pair_spec, cache_spec, cache_spec, row_spec, row_spec, row_spec, row_spec,
                          const((rows_u, n_cache)), const((rows_u, LANES))],
        out_specs=[pair_spec, cache_spec, cache_spec,
                   pl.BlockSpec((1, pairs_u, ts, LANES),
                                lambda i, j, k: (0, unit(i, j, k)[1], unit(i, j, k)[0], 0))],
        out_shape=[jax.ShapeDtypeStruct((b, n_pairs, t, LANES), jnp.bfloat16),
                   jax.ShapeDtypeStruct((nb, d_att, n_cache), jnp.float32),
                   jax.ShapeDtypeStruct((nb, d_att, n_cache), jnp.float32),
                   jax.ShapeDtypeStruct((1, d_att // LANES, nb * ts, LANES), jnp.float32)],
        scratch_shapes=[pltpu.VMEM((t, LANES), jnp.float32) for _ in range(3 * len(dils))],
        compiler_params=pltpu.CompilerParams(
            dimension_semantics=("parallel", "parallel", "arbitrary"), vmem_limit_bytes=int(vmem)),
        name="attention",
    )(*ins, sg_pm, cache_kt, cache_vt, q_s, kn_s, vn_s, sg_s, mult_c, mult_n)


def _outproj_kernel(att_ref, z_ref, x_ref, w_ref, g_ref, b_ref, y_ref, w_bf, *, n_pairs, n_chunks):
    @pl.when((pl.program_id(0) == 0) & (pl.program_id(1) == 0))
    def _():
        _cast_rows(w_ref, w_bf)

    tm = x_ref.shape[1]
    rows = tm // n_chunks
    for c in range(n_chunks):
        sl = slice(c * rows, (c + 1) * rows)
        parts = [att_ref[0, p, sl, :].astype(jnp.bfloat16) for p in range(n_pairs)]
        mix = jnp.concatenate(parts + [z_ref[0, sl, :].astype(jnp.bfloat16)], axis=-1)
        out = jnp.dot(mix, w_bf[...], preferred_element_type=jnp.float32)
        h = DEEPNORM_ALPHA * x_ref[0, sl, :] + out
        mu = jnp.mean(h, axis=-1, keepdims=True)
        d = h - mu
        var = jnp.mean(d * d, axis=-1, keepdims=True)
        y_ref[0, sl, :] = d * lax.rsqrt(var + LN_EPS) * g_ref[...] + b_ref[...]


def _outproj(att_pm, z, x, w_out, ln_g, ln_b, *, tm):
    b, t, d = x.shape
    n_pairs = att_pm.shape[1]
    d_conv = z.shape[-1]
    d_mix = w_out.shape[0]
    vmem = (d_mix * d * (4 + 2) + 4 * tm * d * 4 + 2 * tm * n_pairs * LANES * att_pm.dtype.itemsize
            + 2 * tm * d_conv * 2 + 6 * tm * d * 4)
    return pl.pallas_call(
        functools.partial(_outproj_kernel, n_pairs=n_pairs, n_chunks=OUTPROJ_CHUNKS),
        grid=(b, t // tm),
        in_specs=[
            pl.BlockSpec((1, n_pairs, tm, LANES), lambda i, j: (i, 0, j, 0)),
            pl.BlockSpec((1, tm, d_conv), lambda i, j: (i, j, 0)),
            pl.BlockSpec((1, tm, d), lambda i, j: (i, j, 0)),
            pl.BlockSpec((d_mix, d), lambda i, j: (0, 0), pipeline_mode=pl.Buffered(1)),
            pl.BlockSpec((1, d), lambda i, j: (0, 0)),
            pl.BlockSpec((1, d), lambda i, j: (0, 0)),
        ],
        out_specs=pl.BlockSpec((1, tm, d), lambda i, j: (i, j, 0)),
        out_shape=jax.ShapeDtypeStruct((b, t, d), jnp.float32),
        scratch_shapes=[pltpu.VMEM((d_mix, d), jnp.bfloat16)],
        compiler_params=pltpu.CompilerParams(
            dimension_semantics=("arbitrary", "arbitrary"), vmem_limit_bytes=int(vmem)),
        name="outproj_layernorm",
    )(att_pm, z, x, w_out, ln_g.reshape(1, d), ln_b.reshape(1, d))


def _sample_inproj_kernel(x_ref, w_ref, cw_ref, st_ref,
                          q_ref, k_ref, v_ref, sg_ref, z_ref, nc_ref, u_scr,
                          *, nb, ts, d_att, d_conv, q_scale):
    x = x_ref[...].astype(jnp.bfloat16)

    def proj(c0, width):
        w = w_ref[:, c0:c0 + width].astype(jnp.bfloat16)
        return jnp.dot(x, w, preferred_element_type=jnp.float32)

    q_ref[...] = proj(0, d_att) * q_scale
    k_ref[...] = proj(d_att, d_att)
    v_ref[...] = proj(2 * d_att, d_att)
    sg_ref[...] = _silu(proj(3 * d_att, d_att))
    c0 = 4 * d_att
    gb = proj(c0, d_conv)
    gc = proj(c0 + d_conv, d_conv)
    hh = proj(c0 + 2 * d_conv, d_conv)
    g_conv = proj(c0 + 3 * d_conv, d_conv)
    u = (gc * hh).reshape(nb, ts, d_conv)
    u_scr[:, SUBLANES - 2:SUBLANES, :] = st_ref[...]
    u_scr[:, SUBLANES:SUBLANES + ts, :] = u
    um2 = u_scr[:, SUBLANES - 2:SUBLANES - 2 + ts, :]
    um1 = u_scr[:, SUBLANES - 1:SUBLANES - 1 + ts, :]
    cw = cw_ref[...]
    y_conv = cw[0:1, :][None] * um2 + cw[1:2, :][None] * um1 + cw[2:3, :][None] * u
    z = gb * y_conv.reshape(nb * ts, d_conv) * _silu(g_conv)
    z_ref[...] = z.astype(jnp.bfloat16)
    nc_ref[...] = u_scr[:, ts + SUBLANES - 2:ts + SUBLANES, :]


def _sample_inproj(x2, w_in, conv_w, state, *, nb, ts, d_att, d_conv, q_scale):
    rows, d = x2.shape
    f32 = lambda c: jax.ShapeDtypeStruct((rows, c), jnp.float32)
    kern = functools.partial(_sample_inproj_kernel, nb=nb, ts=ts, d_att=d_att, d_conv=d_conv, q_scale=q_scale)
    vmem = d * w_in.shape[1] * 4 + 16 * rows * d * 4 + 8 * MIB
    return pl.pallas_call(
        kern,
        out_shape=[f32(d_att), f32(d_att), f32(d_att), f32(d_att),
                   jax.ShapeDtypeStruct((rows, d_conv), jnp.bfloat16),
                   jax.ShapeDtypeStruct((nb, 2, d_conv), jnp.float32)],
        scratch_shapes=[pltpu.VMEM((nb, ts + SUBLANES, d_conv), jnp.float32)],
        compiler_params=pltpu.CompilerParams(vmem_limit_bytes=int(vmem)),
        name="sample_inproj",
    )(x2, w_in, conv_w, state)


def kernel(x_prompt, x_sample, cache_k, cache_v, state_conv, w_in, conv_w, w_out, ln_g, ln_b):
    b, t, d = x_prompt.shape
    nb, ts, _ = x_sample.shape
    _, n_cache, n_heads, head_dim = cache_k.shape
    d_att = n_heads * head_dim
    d_conv = conv_w.shape[1]
    assert w_in.shape == (d, 4 * d_att + 4 * d_conv) and w_out.shape == (d_att + d_conv, d)
    assert d_att % LANES == 0 and LANES == 2 * head_dim and conv_w.shape[0] == 3
    max_window = max(w for w, _ in DILATED_PAIRS)
    assert n_cache == max_window and t <= max_window
    q_scale = head_dim ** -0.5 * math.log2(math.e)

    def to_time_minor(a):
        return jnp.transpose(a, (0, 2, 3, 1)).reshape(a.shape[0], d_att, a.shape[1])

    def from_time_minor(a):
        return jnp.transpose(a.reshape(a.shape[0], n_heads, head_dim, a.shape[2]), (0, 3, 1, 2))

    conv0 = jnp.zeros((b, 2, d_conv), jnp.float32)
    outs = _inproj(x_prompt, w_in, conv_w, conv0, tm=512, d_att=d_att, d_conv=d_conv, q_scale=q_scale)
    n_d = len(DILATED_PAIRS)
    q_st, k_st, v_st = outs[:n_d], outs[n_d:2 * n_d], outs[2 * n_d:3 * n_d]
    kt_new, vt_new, sg_pm, z, new_conv_prompt = outs[3 * n_d:]

    q_s, k_s, v_s, sg_s, z_s, new_conv_sample = _sample_inproj(
        x_sample.reshape(nb * ts, d), w_in, conv_w, state_conv, nb=nb, ts=ts, d_att=d_att, d_conv=d_conv,
        q_scale=q_scale)
    three = lambda a: a.reshape(nb, ts, d_att)

    att_pm, new_kt_s, new_vt_s, att_s = _attention(
        q_st, k_st, v_st, sg_pm, to_time_minor(cache_k), to_time_minor(cache_v),
        three(q_s), three(k_s), three(v_s), three(sg_s), n_heads=n_heads, head_dim=head_dim)

    y_prompt = _outproj(att_pm, z, x_prompt, w_out, ln_g, ln_b, tm=1024)
    y_sample = _outproj(att_s, z_s.reshape(1, nb * ts, d_conv), x_sample.reshape(1, nb * ts, d),
                        w_out, ln_g, ln_b, tm=nb * ts)

    return (y_prompt, y_sample.reshape(nb, ts, d),
            from_time_minor(kt_new), from_time_minor(vt_new), new_conv_prompt,
            from_time_minor(new_kt_s), from_time_minor(new_vt_s), new_conv_sample)
```

```python
import functools
import math

import numpy as np
import jax
import jax.numpy as jnp
from jax import lax
from jax.experimental import pallas as pl
from jax.experimental.pallas import tpu as pltpu

DILATED_PAIRS = ((128, 1), (512, 4), (2048, 16))
LN_EPS = 1e-5
DEPTH = 1
DEEPNORM_ALPHA = (2.0 * DEPTH) ** 0.25

LANES = 128
SUBLANES = 8
Q_BLOCK = 128
NEG_BIG = -1e30
ATTN_PARTS = 2
OUTPROJ_CHUNKS = 2
CAST_ROWS = 128
BLOCK_GROUP = 8
L_FLOOR = 2.0 ** -100

MIB = 1024 * 1024


def _dilations():
    dils = tuple(sorted(d for _, d in DILATED_PAIRS))
    assert dils[0] == 1 and all(b % a == 0 for a, b in zip(dils, dils[1:]))
    return dils


def _silu(x):
    return x * (1.0 / (1.0 + jnp.exp(-x)))


def _cast_rows(src_ref, dst_ref):
    n = src_ref.shape[0] // CAST_ROWS

    def body(i, carry):
        rows = pl.ds(pl.multiple_of(i * CAST_ROWS, CAST_ROWS), CAST_ROWS)
        dst_ref[rows, :] = src_ref[rows, :].astype(dst_ref.dtype)
        return carry

    lax.fori_loop(0, n, body, 0)


def _emit_streams(val, refs, which, dils, scr, tm):
    n_pairs = val.shape[1] // LANES
    for p in range(n_pairs):
        col = val[:, p * LANES:(p + 1) * LANES]
        refs[0][0, p, which, 0] = col.astype(jnp.bfloat16)
        if len(dils) > 1:
            scr[0][...] = col
        for lvl in range(1, len(dils)):
            d_prev = dils[lvl - 1]
            f = dils[lvl] // d_prev
            n_prev = tm // d_prev
            n = n_prev // f
            src, dst = scr[(lvl - 1) % 2], scr[lvl % 2]
            for s in range(d_prev):
                for r2 in range(f):
                    r = s + d_prev * r2
                    rows = src[pl.ds(s * n_prev + r2, n, stride=f), :]
                    refs[lvl][0, p, which, r] = rows.astype(jnp.bfloat16)
                    if lvl + 1 < len(dils):
                        dst[r * n:(r + 1) * n, :] = rows


def _inproj_kernel(*refs, tm, d_att, d_conv, n_pairs, q_scale, dils):
    x_ref, w_ref, cw_ref, hist_ref = refs[:4]
    n_d = len(dils)
    st_refs = refs[4:4 + n_d]
    kt_ref, vt_ref, sg_ref, z_ref, nc_ref, u_scr, s_a, s_b, w_bf = refs[4 + n_d:]

    @pl.when((pl.program_id(0) == 0) & (pl.program_id(1) == 0))
    def _():
        _cast_rows(w_ref, w_bf)

    @pl.when(pl.program_id(1) == 0)
    def _():
        u_scr[SUBLANES - 2:SUBLANES, :] = hist_ref[0]

    x = x_ref[0].astype(jnp.bfloat16)

    def proj(c0, width):
        return jnp.dot(x, w_bf[:, c0:c0 + width], preferred_element_type=jnp.float32)

    c0 = 4 * d_att
    gc = proj(c0 + d_conv, d_conv)
    hh = proj(c0 + 2 * d_conv, d_conv)
    u = gc * hh
    u_scr[SUBLANES:SUBLANES + tm, :] = u
    um2 = u_scr[SUBLANES - 2:SUBLANES - 2 + tm, :]
    um1 = u_scr[SUBLANES - 1:SUBLANES - 1 + tm, :]
    y_conv = cw_ref[0:1, :] * um2 + cw_ref[1:2, :] * um1 + cw_ref[2:3, :] * u
    tail = u[tm - 2:tm, :]
    u_scr[SUBLANES - 2:SUBLANES, :] = tail
    nc_ref[0] = tail
    gb = proj(c0, d_conv)
    g_conv = proj(c0 + 3 * d_conv, d_conv)
    z_ref[0] = (gb * y_conv * _silu(g_conv)).astype(jnp.bfloat16)

    sg = _silu(proj(3 * d_att, d_att)).astype(jnp.bfloat16)
    for p in range(n_pairs):
        sg_ref[0, p] = sg[:, p * LANES:(p + 1) * LANES]
    k = proj(d_att, d_att)
    kt_ref[0] = k.T
    _emit_streams(k, st_refs, 1, dils, (s_a, s_b), tm)
    v = proj(2 * d_att, d_att)
    vt_ref[0] = v.T
    _emit_streams(v, st_refs, 2, dils, (s_a, s_b), tm)
    q = proj(0, d_att) * q_scale
    _emit_streams(q, st_refs, 0, dils, (s_a, s_b), tm)


def _inproj(x, w_in, conv_w, hist, *, tm, d_att, d_conv, q_scale):
    b, t, d = x.shape
    n_pairs = d_att // LANES
    n_out = w_in.shape[1]
    dils = _dilations()
    assert tm % (dils[-1] * 2 * SUBLANES) == 0
    kern = functools.partial(_inproj_kernel, tm=tm, d_att=d_att, d_conv=d_conv,
                             n_pairs=n_pairs, q_scale=q_scale, dils=dils)
    st_shapes = [jax.ShapeDtypeStruct((b, n_pairs, 3, dd, t // dd, LANES), jnp.bfloat16) for dd in dils]
    st_specs = [pl.BlockSpec((1, n_pairs, 3, dd, tm // dd, LANES), lambda i, j: (i, 0, 0, 0, j, 0))
                for dd in dils]
    pm = jax.ShapeDtypeStruct((b, n_pairs, t, LANES), jnp.bfloat16)
    pm_spec = pl.BlockSpec((1, n_pairs, tm, LANES), lambda i, j: (i, 0, j, 0))
    row_spec = lambda c: pl.BlockSpec((1, tm, c), lambda i, j: (i, j, 0))
    col_spec = pl.BlockSpec((1, d_att, tm), lambda i, j: (i, 0, j))
    vmem = (d * n_out * (4 + 2) + 2 * tm * d * 4
            + 2 * tm * ((3 * len(dils) + 1) * d_att * 2 + 2 * d_att * 4 + d_conv * 2)
            + (tm + SUBLANES) * d_conv * 4 + 2 * tm * LANES * 4 + 12 * tm * d_att * 4)
    return pl.pallas_call(
        kern,
        grid=(b, t // tm),
        in_specs=[
            row_spec(d),
            pl.BlockSpec((d, n_out), lambda i, j: (0, 0), pipeline_mode=pl.Buffered(1)),
            pl.BlockSpec(conv_w.shape, lambda i, j: (0, 0)),
            pl.BlockSpec((1, 2, d_conv), lambda i, j: (i, 0, 0)),
        ],
        out_specs=st_specs + [col_spec, col_spec, pm_spec,
                              row_spec(d_conv), pl.BlockSpec((1, 2, d_conv), lambda i, j: (i, 0, 0))],
        out_shape=st_shapes + [
            jax.ShapeDtypeStruct((b, d_att, t), jnp.float32),
            jax.ShapeDtypeStruct((b, d_att, t), jnp.float32),
            pm,
            jax.ShapeDtypeStruct((b, t, d_conv), jnp.bfloat16),
            jax.ShapeDtypeStruct((b, 2, d_conv), jnp.float32)],
        scratch_shapes=[pltpu.VMEM((tm + SUBLANES, d_conv), jnp.float32),
                        pltpu.VMEM((tm, LANES), jnp.float32),
                        pltpu.VMEM((tm, LANES), jnp.float32),
                        pltpu.VMEM((d, n_out), jnp.bfloat16)],
        compiler_params=pltpu.CompilerParams(
            dimension_semantics=("arbitrary", "arbitrary"), vmem_limit_bytes=int(vmem)),
        name="prompt_inproj",
    )(x, w_in, conv_w, hist)


def _block_scores(q, kwin, lo):
    zero = jnp.zeros_like(q)
    q2 = jnp.concatenate([jnp.where(lo, q, zero), jnp.where(lo, zero, q)], axis=0)
    return lax.dot_general(q2, kwin, (((1,), (1,)), ((), ())), preferred_element_type=jnp.float32)


def _block_probs(s, bias, row_exact):
    ps, ms = [], []
    for h in range(2):
        sh = s[h * Q_BLOCK:(h + 1) * Q_BLOCK] + bias
        if row_exact:
            m = jnp.max(sh, axis=-1, keepdims=True)
        else:
            m = jnp.max(jnp.max(sh, axis=0, keepdims=True), axis=1, keepdims=True)
        ps.append(jnp.exp2(sh - m).astype(jnp.bfloat16))
        ms.append(jnp.broadcast_to(m, (Q_BLOCK, 1)))
    return ps, ms


def _block_values(ps, ms, vwin, lo):
    lo_v = lax.broadcasted_iota(jnp.int32, vwin.shape, 1) < LANES // 2
    one = jnp.ones_like(vwin)
    pv0 = jnp.dot(ps[0], jnp.where(lo_v, vwin, one), preferred_element_type=jnp.float32)
    pv1 = jnp.dot(ps[1], jnp.where(lo_v, one, vwin), preferred_element_type=jnp.float32)
    acc = jnp.where(lo, pv0, pv1)
    l = pltpu.roll(jnp.where(lo, pv1, pv0), LANES // 2, axis=1)
    return acc, jnp.where(lo, ms[0], ms[1]), l


def _branch_multiplicity(n_q, n_cache):
    i = np.arange(n_q)[:, None]
    pos = np.arange(n_cache + n_q)[None, :]
    dist = n_cache + i - pos
    mult = np.zeros(dist.shape, np.float32)
    for window, dil in DILATED_PAIRS:
        mult += (dist >= 0) & (dist <= window) & (dist % dil == 0)
    return mult


def _sample_unit(ck_ref, cv_ref, q_ref, kn_ref, vn_ref, sg_ref, mc_ref, mn_ref,
                 nk_ref, nv_ref, att_ref, *, n_heads, head_dim, ts, n_cache, n_pairs):
    d_att = n_heads * head_dim
    rows = n_heads * ts
    head_of_lane = lax.broadcasted_iota(jnp.int32, (rows, d_att), 1) // head_dim
    head_of_row = lax.broadcasted_iota(jnp.int32, (rows, d_att), 0) // ts
    own = head_of_lane == head_of_row

    q = q_ref[0]
    qx = jnp.where(own, jnp.concatenate([q] * n_heads, axis=0), 0.0).astype(jnp.bfloat16)
    kct = ck_ref[0]
    vct = cv_ref[0]
    pad = jnp.zeros((LANES - ts, d_att), jnp.float32)
    knt = jnp.concatenate([pad, kn_ref[0]], axis=0).T
    vnt = jnp.concatenate([pad, vn_ref[0]], axis=0).T
    nt = (((1,), (1,)), ((), ()))
    s_c = jnp.dot(qx, kct.astype(jnp.bfloat16), preferred_element_type=jnp.float32)
    s_n = jnp.dot(qx, knt.astype(jnp.bfloat16), preferred_element_type=jnp.float32)
    mc = mc_ref[...]
    mn = mn_ref[...]
    s_c = jnp.where(mc > 0, s_c, NEG_BIG)
    s_n = jnp.where(mn > 0, s_n, NEG_BIG)
    m = jnp.maximum(jnp.max(s_c, axis=-1, keepdims=True), jnp.max(s_n, axis=-1, keepdims=True))
    e_c = jnp.exp2(s_c - m) * mc
    e_n = jnp.exp2(s_n - m) * mn
    l = jnp.sum(e_c, axis=-1, keepdims=True) + jnp.sum(e_n, axis=-1, keepdims=True)
    ox = (lax.dot_general(e_c.astype(jnp.bfloat16), vct.astype(jnp.bfloat16), nt,
                          preferred_element_type=jnp.float32)
          + lax.dot_general(e_n.astype(jnp.bfloat16), vnt.astype(jnp.bfloat16), nt,
                            preferred_element_type=jnp.float32))
    ox = jnp.where(own, ox * (1.0 / l), 0.0)
    o = ox[0:ts]
    for h in range(1, n_heads):
        o = o + ox[h * ts:(h + 1) * ts]
    att = o * sg_ref[0]
    for p in range(n_pairs):
        att_ref[0, p] = att[:, p * LANES:(p + 1) * LANES]

    main = n_cache - LANES
    is_old = lax.broadcasted_iota(jnp.int32, (d_att, LANES), 1) < LANES - ts
    for src, new_t, dst in ((kct, knt, nk_ref), (vct, vnt, nv_ref)):
        rolled = pltpu.roll(src, n_cache - ts, axis=1)
        dst[0, :, 0:main] = rolled[:, 0:main]
        dst[0, :, main:n_cache] = jnp.where(is_old, rolled[:, main:n_cache], new_t)


def _split_streams(dils, seq, n_parts):
    items = sorted(((seq // d // Q_BLOCK, i, r) for i, d in enumerate(dils) for r in range(d) if i > 0),
                   reverse=True)
    parts, loads = [[] for _ in range(n_parts)], [0] * n_parts
    loads[-1] = seq // dils[0] // Q_BLOCK
    for cost, i, r in items:
        j = loads.index(min(loads))
        parts[j].append((i, r))
        loads[j] += cost
    parts[-1].append((0, 0))
    return parts


def _attention_kernel(*refs, seq, dils, head_dim, parts, sample_cfg):
    n_br = len(dils)
    qkv = refs[:n_br]
    sg_ref = refs[n_br]
    sample_in = refs[n_br + 1:n_br + 9]
    att_ref = refs[n_br + 9]
    sample_out = refs[n_br + 10:n_br + 13]
    scr = refs[n_br + 13:]
    a_scr, m_scr, l_scr = scr[0::3], scr[1::3], scr[2::3]
    part = pl.program_id(2)

    lane = lax.broadcasted_iota(jnp.int32, (Q_BLOCK, LANES), 1)
    lo = lane < head_dim
    row = lax.broadcasted_iota(jnp.int32, (Q_BLOCK, 2 * Q_BLOCK), 0)
    col = lax.broadcasted_iota(jnp.int32, (Q_BLOCK, 2 * Q_BLOCK), 1)
    band = jnp.where((col >= row) & (col <= row + Q_BLOCK), 0.0, NEG_BIG).astype(jnp.float32)
    row1 = lax.broadcasted_iota(jnp.int32, (Q_BLOCK, Q_BLOCK), 0)
    col1 = lax.broadcasted_iota(jnp.int32, (Q_BLOCK, Q_BLOCK), 1)
    causal = jnp.where(col1 <= row1, 0.0, NEG_BIG).astype(jnp.float32)

    def merge(row0):
        rows = slice(row0, row0 + Q_BLOCK)
        ms = [m_ref[rows, :] for m_ref in m_scr]
        top = functools.reduce(jnp.maximum, ms)
        ws = [jnp.exp2(m - top) for m in ms]
        num = functools.reduce(lambda a, b: a + b, [w * a_ref[rows, :] for w, a_ref in zip(ws, a_scr)])
        den = functools.reduce(lambda a, b: a + b, [w * l_ref[rows, :] for w, l_ref in zip(ws, l_scr)])
        att = num * (1.0 / den) * sg_ref[rows, :].astype(jnp.float32)
        att_ref[rows, :] = att.astype(att_ref.dtype)

    def run_blocks(items, row_exact):
        blocks = [(i, r, qb) for i, r in items for qb in range(seq // dils[i] // Q_BLOCK)]
        l_min = None
        for g in range(0, len(blocks), BLOCK_GROUP):
            group = blocks[g:g + BLOCK_GROUP]
            keys = [slice(max(qb - 1, 0) * Q_BLOCK, (qb + 1) * Q_BLOCK) for _, _, qb in group]
            scores = [_block_scores(qkv[i][0, r, qb * Q_BLOCK:(qb + 1) * Q_BLOCK, :], qkv[i][1, r, ks, :], lo)
                      for (i, r, qb), ks in zip(group, keys)]
            probs = [_block_probs(s, causal if qb == 0 else band, row_exact)
                     for s, (_, _, qb) in zip(scores, group)]
            for (i, r, qb), ks, (ps, ms) in zip(group, keys, probs):
                vals = _block_values(ps, ms, qkv[i][2, r, ks, :], lo)
                d, row0 = dils[i], qb * Q_BLOCK
                idx = pl.ds(row0, Q_BLOCK) if d == 1 else pl.ds(r + d * row0, Q_BLOCK, stride=d)
                for ref, val in zip((a_scr[i], m_scr[i], l_scr[i]), vals):
                    ref[idx, :] = val
                l_min = vals[2] if l_min is None else jnp.minimum(l_min, vals[2])
            for i, _, qb in group:
                if i == 0:
                    merge(qb * Q_BLOCK)
        return l_min

    for idx, items in enumerate(parts):
        @pl.when(part == idx)
        def _(items=items):
            _sample_unit(*sample_in, *sample_out, **sample_cfg)
            l_min = run_blocks(items, False)

            @pl.when(jnp.min(l_min) < L_FLOOR)
            def _():
                run_blocks(items, True)


def _attention(qkv_st, sg_pm, cache_kt, cache_vt, q_s, kn_s, vn_s, sg_s, *, n_heads, head_dim):
    b, n_pairs, t, _ = sg_pm.shape
    nb, d_att, n_cache = cache_kt.shape
    ts = q_s.shape[1]
    dils = _dilations()
    for w, d in DILATED_PAIRS:
        assert w == d * Q_BLOCK and t % (d * Q_BLOCK) == 0
    n_parts = ATTN_PARTS
    steps = b * n_pairs * n_parts
    assert steps % nb == 0 and n_heads % (steps // nb) == 0
    groups = steps // nb
    heads_u = n_heads // groups
    d_u = heads_u * head_dim
    assert d_u % LANES == 0
    pairs_u = d_u // LANES
    rows_u = heads_u * ts

    def unit(i, j, k):
        u = (i * n_pairs + j) * n_parts + k
        return u // groups, u % groups

    ins, specs = [], []
    for a, d in zip(qkv_st, dils):
        assert a.shape == (b, n_pairs, 3, d, t // d, LANES)
        ins.append(a)
        specs.append(pl.BlockSpec((None, None, 3, d, t // d, LANES), lambda i, j, k: (i, j, 0, 0, 0, 0)))
    pair_spec = pl.BlockSpec((None, None, t, LANES), lambda i, j, k: (i, j, 0, 0))
    cache_spec = pl.BlockSpec((1, d_u, n_cache), lambda i, j, k: (unit(i, j, k)[0], unit(i, j, k)[1], 0))
    row_spec = pl.BlockSpec((1, ts, d_u), lambda i, j, k: (unit(i, j, k)[0], 0, unit(i, j, k)[1]))
    mult = np.tile(_branch_multiplicity(ts, n_cache), (heads_u, 1))
    mult_c = jnp.asarray(mult[:, :n_cache])
    mult_n = jnp.asarray(np.pad(mult[:, n_cache:], ((0, 0), (LANES - ts, 0))))
    const = lambda shape: pl.BlockSpec(shape, lambda i, j, k: (0, 0))
    sample_cfg = dict(n_heads=heads_u, head_dim=head_dim, ts=ts, n_cache=n_cache, n_pairs=pairs_u)
    kern = functools.partial(_attention_kernel, seq=t, dils=dils, head_dim=head_dim,
                             parts=_split_streams(dils, t, n_parts), sample_cfg=sample_cfg)
    vmem = (2 * (3 * len(dils) + 2) * t * LANES * 2 + 3 * len(dils) * t * LANES * 4
            + 14 * d_u * n_cache * 4 + 8 * rows_u * n_cache * 4 + 12 * MIB)
    return pl.pallas_call(
        kern,
        grid=(b, n_pairs, n_parts),
        in_specs=specs + [pair_spec, cache_spec, cache_spec, row_spec, row_spec, row_spec, row_spec,
                          const((rows_u, n_cache)), const((rows_u, LANES))],
        out_specs=[pair_spec, cache_spec, cache_spec,
                   pl.BlockSpec((1, pairs_u, ts, LANES),
                                lambda i, j, k: (0, unit(i, j, k)[1], unit(i, j, k)[0], 0))],
        out_shape=[jax.ShapeDtypeStruct((b, n_pairs, t, LANES), jnp.bfloat16),
                   jax.ShapeDtypeStruct((nb, d_att, n_cache), jnp.float32),
                   jax.ShapeDtypeStruct((nb, d_att, n_cache), jnp.float32),
                   jax.ShapeDtypeStruct((1, d_att // LANES, nb * ts, LANES), jnp.float32)],
        scratch_shapes=[pltpu.VMEM((t, LANES), jnp.float32) for _ in range(3 * len(dils))],
        compiler_params=pltpu.CompilerParams(
            dimension_semantics=("parallel", "parallel", "arbitrary"), vmem_limit_bytes=int(vmem)),
        name="attention",
    )(*ins, sg_pm, cache_kt, cache_vt, q_s, kn_s, vn_s, sg_s, mult_c, mult_n)


def _outproj_kernel(att_ref, z_ref, x_ref, w_ref, g_ref, b_ref, y_ref, w_bf, *, n_pairs, n_chunks):
    @pl.when((pl.program_id(0) == 0) & (pl.program_id(1) == 0))
    def _():
        _cast_rows(w_ref, w_bf)

    tm = x_ref.shape[1]
    rows = tm // n_chunks
    for c in range(n_chunks):
        sl = slice(c * rows, (c + 1) * rows)
        parts = [att_ref[0, p, sl, :].astype(jnp.bfloat16) for p in range(n_pairs)]
        mix = jnp.concatenate(parts + [z_ref[0, sl, :].astype(jnp.bfloat16)], axis=-1)
        out = jnp.dot(mix, w_bf[...], preferred_element_type=jnp.float32)
        h = DEEPNORM_ALPHA * x_ref[0, sl, :] + out
        mu = jnp.mean(h, axis=-1, keepdims=True)
        d = h - mu
        var = jnp.mean(d * d, axis=-1, keepdims=True)
        y_ref[0, sl, :] = d * lax.rsqrt(var + LN_EPS) * g_ref[...] + b_ref[...]


def _outproj(att_pm, z, x, w_out, ln_g, ln_b, *, tm):
    b, t, d = x.shape
    n_pairs = att_pm.shape[1]
    d_conv = z.shape[-1]
    d_mix = w_out.shape[0]
    vmem = (d_mix * d * (4 + 2) + 4 * tm * d * 4 + 2 * tm * n_pairs * LANES * att_pm.dtype.itemsize
            + 2 * tm * d_conv * 2 + 6 * tm * d * 4)
    return pl.pallas_call(
        functools.partial(_outproj_kernel, n_pairs=n_pairs, n_chunks=OUTPROJ_CHUNKS),
        grid=(b, t // tm),
        in_specs=[
            pl.BlockSpec((1, n_pairs, tm, LANES), lambda i, j: (i, 0, j, 0)),
            pl.BlockSpec((1, tm, d_conv), lambda i, j: (i, j, 0)),
            pl.BlockSpec((1, tm, d), lambda i, j: (i, j, 0)),
            pl.BlockSpec((d_mix, d), lambda i, j: (0, 0), pipeline_mode=pl.Buffered(1)),
            pl.BlockSpec((1, d), lambda i, j: (0, 0)),
            pl.BlockSpec((1, d), lambda i, j: (0, 0)),
        ],
        out_specs=pl.BlockSpec((1, tm, d), lambda i, j: (i, j, 0)),
        out_shape=jax.ShapeDtypeStruct((b, t, d), jnp.float32),
        scratch_shapes=[pltpu.VMEM((d_mix, d), jnp.bfloat16)],
        compiler_params=pltpu.CompilerParams(
            dimension_semantics=("arbitrary", "arbitrary"), vmem_limit_bytes=int(vmem)),
        name="outproj_layernorm",
    )(att_pm, z, x, w_out, ln_g.reshape(1, d), ln_b.reshape(1, d))


def _sample_inproj_kernel(x_ref, w_ref, cw_ref, st_ref,
                          q_ref, k_ref, v_ref, sg_ref, z_ref, nc_ref, u_scr,
                          *, nb, ts, d_att, d_conv, q_scale):
    x = x_ref[...].astype(jnp.bfloat16)

    def proj(c0, width):
        w = w_ref[:, c0:c0 + width].astype(jnp.bfloat16)
        return jnp.dot(x, w, preferred_element_type=jnp.float32)

    q_ref[...] = proj(0, d_att) * q_scale
    k_ref[...] = proj(d_att, d_att)
    v_ref[...] = proj(2 * d_att, d_att)
    sg_ref[...] = _silu(proj(3 * d_att, d_att))
    c0 = 4 * d_att
    gb = proj(c0, d_conv)
    gc = proj(c0 + d_conv, d_conv)
    hh = proj(c0 + 2 * d_conv, d_conv)
    g_conv = proj(c0 + 3 * d_conv, d_conv)
    u = (gc * hh).reshape(nb, ts, d_conv)
    u_scr[:, SUBLANES - 2:SUBLANES, :] = st_ref[...]
    u_scr[:, SUBLANES:SUBLANES + ts, :] = u
    um2 = u_scr[:, SUBLANES - 2:SUBLANES - 2 + ts, :]
    um1 = u_scr[:, SUBLANES - 1:SUBLANES - 1 + ts, :]
    cw = cw_ref[...]
    y_conv = cw[0:1, :][None] * um2 + cw[1:2, :][None] * um1 + cw[2:3, :][None] * u
    z = gb * y_conv.reshape(nb * ts, d_conv) * _silu(g_conv)
    z_ref[...] = z.astype(jnp.bfloat16)
    nc_ref[...] = u_scr[:, ts + SUBLANES - 2:ts + SUBLANES, :]


def _sample_inproj(x2, w_in, conv_w, state, *, nb, ts, d_att, d_conv, q_scale):
    rows, d = x2.shape
    f32 = lambda c: jax.ShapeDtypeStruct((rows, c), jnp.float32)
    kern = functools.partial(_sample_inproj_kernel, nb=nb, ts=ts, d_att=d_att, d_conv=d_conv, q_scale=q_scale)
    vmem = d * w_in.shape[1] * 4 + 16 * rows * d * 4 + 8 * MIB
    return pl.pallas_call(
        kern,
        out_shape=[f32(d_att), f32(d_att), f32(d_att), f32(d_att),
                   jax.ShapeDtypeStruct((rows, d_conv), jnp.bfloat16),
                   jax.ShapeDtypeStruct((nb, 2, d_conv), jnp.float32)],
        scratch_shapes=[pltpu.VMEM((nb, ts + SUBLANES, d_conv), jnp.float32)],
        compiler_params=pltpu.CompilerParams(vmem_limit_bytes=int(vmem)),
        name="sample_inproj",
    )(x2, w_in, conv_w, state)


def kernel(x_prompt, x_sample, cache_k, cache_v, state_conv, w_in, conv_w, w_out, ln_g, ln_b):
    b, t, d = x_prompt.shape
    nb, ts, _ = x_sample.shape
    _, n_cache, n_heads, head_dim = cache_k.shape
    d_att = n_heads * head_dim
    d_conv = conv_w.shape[1]
    assert w_in.shape == (d, 4 * d_att + 4 * d_conv) and w_out.shape == (d_att + d_conv, d)
    assert d_att % LANES == 0 and LANES == 2 * head_dim and conv_w.shape[0] == 3
    max_window = max(w for w, _ in DILATED_PAIRS)
    assert n_cache == max_window and t <= max_window
    q_scale = head_dim ** -0.5 * math.log2(math.e)

    def to_time_minor(a):
        return jnp.transpose(a, (0, 2, 3, 1)).reshape(a.shape[0], d_att, a.shape[1])

    def from_time_minor(a):
        return jnp.transpose(a.reshape(a.shape[0], n_heads, head_dim, a.shape[2]), (0, 3, 1, 2))

    conv0 = jnp.zeros((b, 2, d_conv), jnp.float32)
    outs = _inproj(x_prompt, w_in, conv_w, conv0, tm=512, d_att=d_att, d_conv=d_conv, q_scale=q_scale)
    n_d = len(DILATED_PAIRS)
    qkv_st = outs[:n_d]
    kt_new, vt_new, sg_pm, z, new_conv_prompt = outs[n_d:]

    q_s, k_s, v_s, sg_s, z_s, new_conv_sample = _sample_inproj(
        x_sample.reshape(nb * ts, d), w_in, conv_w, state_conv, nb=nb, ts=ts, d_att=d_att, d_conv=d_conv,
        q_scale=q_scale)
    three = lambda a: a.reshape(nb, ts, d_att)

    att_pm, new_kt_s, new_vt_s, att_s = _attention(
        qkv_st, sg_pm, to_time_minor(cache_k), to_time_minor(cache_v),
        three(q_s), three(k_s), three(v_s), three(sg_s), n_heads=n_heads, head_dim=head_dim)

    y_prompt = _outproj(att_pm, z, x_prompt, w_out, ln_g, ln_b, tm=1024)
    y_sample = _outproj(att_s, z_s.reshape(1, nb * ts, d_conv), x_sample.reshape(1, nb * ts, d),
                        w_out, ln_g, ln_b, tm=nb * ts)

    return (y_prompt, y_sample.reshape(nb, ts, d),
            from_time_minor(kt_new), from_time_minor(vt_new), new_conv_prompt,
            from_time_minor(new_kt_s), from_time_minor(new_vt_s), new_conv_sample)
```

```python
import functools
import math

import numpy as np
import jax
import jax.numpy as jnp
from jax import lax
from jax.experimental import pallas as pl
from jax.experimental.pallas import tpu as pltpu

DILATED_PAIRS = ((128, 1), (512, 4), (2048, 16))
LN_EPS = 1e-5
DEPTH = 1
DEEPNORM_ALPHA = (2.0 * DEPTH) ** 0.25

LANES = 128
SUBLANES = 8
Q_BLOCK = 128
NEG_BIG = -1e30
ATTN_PARTS = 2
OUTPROJ_CHUNKS = 2
CAST_ROWS = 128
BLOCK_GROUP = 4
L_FLOOR = 2.0 ** -100

MIB = 1024 * 1024


def _dilations():
    dils = tuple(sorted(d for _, d in DILATED_PAIRS))
    assert dils[0] == 1 and all(b % a == 0 for a, b in zip(dils, dils[1:]))
    return dils


def _silu(x):
    return x * (1.0 / (1.0 + jnp.exp(-x)))


def _cast_rows(src_ref, dst_ref):
    n = src_ref.shape[0] // CAST_ROWS

    def body(i, carry):
        rows = pl.ds(pl.multiple_of(i * CAST_ROWS, CAST_ROWS), CAST_ROWS)
        dst_ref[rows, :] = src_ref[rows, :].astype(dst_ref.dtype)
        return carry

    lax.fori_loop(0, n, body, 0)


def _emit_streams(val, refs, which, dils, scr, tm):
    n_pairs = val.shape[1] // LANES
    for p in range(n_pairs):
        col = val[:, p * LANES:(p + 1) * LANES]
        refs[0][0, p, which, 0] = col.astype(jnp.bfloat16)
        if len(dils) > 1:
            scr[0][...] = col
        for lvl in range(1, len(dils)):
            d_prev = dils[lvl - 1]
            f = dils[lvl] // d_prev
            n_prev = tm // d_prev
            n = n_prev // f
            src, dst = scr[(lvl - 1) % 2], scr[lvl % 2]
            for s in range(d_prev):
                for r2 in range(f):
                    r = s + d_prev * r2
                    rows = src[pl.ds(s * n_prev + r2, n, stride=f), :]
                    refs[lvl][0, p, which, r] = rows.astype(jnp.bfloat16)
                    if lvl + 1 < len(dils):
                        dst[r * n:(r + 1) * n, :] = rows


def _inproj_kernel(*refs, tm, d_att, d_conv, n_pairs, q_scale, dils):
    x_ref, w_ref, cw_ref, hist_ref = refs[:4]
    n_d = len(dils)
    st_refs = refs[4:4 + n_d]
    kt_ref, vt_ref, sg_ref, z_ref, nc_ref, u_scr, s_a, s_b, w_bf = refs[4 + n_d:]

    @pl.when((pl.program_id(0) == 0) & (pl.program_id(1) == 0))
    def _():
        _cast_rows(w_ref, w_bf)

    @pl.when(pl.program_id(1) == 0)
    def _():
        u_scr[SUBLANES - 2:SUBLANES, :] = hist_ref[0]

    x = x_ref[0].astype(jnp.bfloat16)

    def proj(c0, width):
        return jnp.dot(x, w_bf[:, c0:c0 + width], preferred_element_type=jnp.float32)

    c0 = 4 * d_att
    gc = proj(c0 + d_conv, d_conv)
    hh = proj(c0 + 2 * d_conv, d_conv)
    u = gc * hh
    u_scr[SUBLANES:SUBLANES + tm, :] = u
    um2 = u_scr[SUBLANES - 2:SUBLANES - 2 + tm, :]
    um1 = u_scr[SUBLANES - 1:SUBLANES - 1 + tm, :]
    y_conv = cw_ref[0:1, :] * um2 + cw_ref[1:2, :] * um1 + cw_ref[2:3, :] * u
    tail = u[tm - 2:tm, :]
    u_scr[SUBLANES - 2:SUBLANES, :] = tail
    nc_ref[0] = tail
    gb = proj(c0, d_conv)
    g_conv = proj(c0 + 3 * d_conv, d_conv)
    z_ref[0] = (gb * y_conv * _silu(g_conv)).astype(jnp.bfloat16)

    sg = _silu(proj(3 * d_att, d_att)).astype(jnp.bfloat16)
    for p in range(n_pairs):
        sg_ref[0, p] = sg[:, p * LANES:(p + 1) * LANES]
    k = proj(d_att, d_att)
    kt_ref[0] = k.T
    _emit_streams(k, st_refs, 1, dils, (s_a, s_b), tm)
    v = proj(2 * d_att, d_att)
    vt_ref[0] = v.T
    _emit_streams(v, st_refs, 2, dils, (s_a, s_b), tm)
    q = proj(0, d_att) * q_scale
    _emit_streams(q, st_refs, 0, dils, (s_a, s_b), tm)


def _inproj(x, w_in, conv_w, hist, *, tm, d_att, d_conv, q_scale):
    b, t, d = x.shape
    n_pairs = d_att // LANES
    n_out = w_in.shape[1]
    dils = _dilations()
    assert tm % (dils[-1] * 2 * SUBLANES) == 0
    kern = functools.partial(_inproj_kernel, tm=tm, d_att=d_att, d_conv=d_conv,
                             n_pairs=n_pairs, q_scale=q_scale, dils=dils)
    st_shapes = [jax.ShapeDtypeStruct((b, n_pairs, 3, dd, t // dd, LANES), jnp.bfloat16) for dd in dils]
    st_specs = [pl.BlockSpec((1, n_pairs, 3, dd, tm // dd, LANES), lambda i, j: (i, 0, 0, 0, j, 0))
                for dd in dils]
    pm = jax.ShapeDtypeStruct((b, n_pairs, t, LANES), jnp.bfloat16)
    pm_spec = pl.BlockSpec((1, n_pairs, tm, LANES), lambda i, j: (i, 0, j, 0))
    row_spec = lambda c: pl.BlockSpec((1, tm, c), lambda i, j: (i, j, 0))
    col_spec = pl.BlockSpec((1, d_att, tm), lambda i, j: (i, 0, j))
    vmem = (d * n_out * (4 + 2) + 2 * tm * d * 4
            + 2 * tm * ((3 * len(dils) + 1) * d_att * 2 + 2 * d_att * 4 + d_conv * 2)
            + (tm + SUBLANES) * d_conv * 4 + 2 * tm * LANES * 4 + 12 * tm * d_att * 4)
    return pl.pallas_call(
        kern,
        grid=(b, t // tm),
        in_specs=[
            row_spec(d),
            pl.BlockSpec((d, n_out), lambda i, j: (0, 0), pipeline_mode=pl.Buffered(1)),
            pl.BlockSpec(conv_w.shape, lambda i, j: (0, 0)),
            pl.BlockSpec((1, 2, d_conv), lambda i, j: (i, 0, 0)),
        ],
        out_specs=st_specs + [col_spec, col_spec, pm_spec,
                              row_spec(d_conv), pl.BlockSpec((1, 2, d_conv), lambda i, j: (i, 0, 0))],
        out_shape=st_shapes + [
            jax.ShapeDtypeStruct((b, d_att, t), jnp.float32),
            jax.ShapeDtypeStruct((b, d_att, t), jnp.float32),
            pm,
            jax.ShapeDtypeStruct((b, t, d_conv), jnp.bfloat16),
            jax.ShapeDtypeStruct((b, 2, d_conv), jnp.float32)],
        scratch_shapes=[pltpu.VMEM((tm + SUBLANES, d_conv), jnp.float32),
                        pltpu.VMEM((tm, LANES), jnp.float32),
                        pltpu.VMEM((tm, LANES), jnp.float32),
                        pltpu.VMEM((d, n_out), jnp.bfloat16)],
        compiler_params=pltpu.CompilerParams(
            dimension_semantics=("arbitrary", "arbitrary"), vmem_limit_bytes=int(vmem)),
        name="prompt_inproj",
    )(x, w_in, conv_w, hist)


def _block_scores(q, kwin, lo):
    zero = jnp.zeros_like(q)
    q2 = jnp.concatenate([jnp.where(lo, q, zero), jnp.where(lo, zero, q)], axis=0)
    return lax.dot_general(q2, kwin, (((1,), (1,)), ((), ())), preferred_element_type=jnp.float32)


def _block_probs(s, bias, row_exact):
    ps, ms = [], []
    for h in range(2):
        sh = s[h * Q_BLOCK:(h + 1) * Q_BLOCK] + bias
        if row_exact:
            m = jnp.max(sh, axis=-1, keepdims=True)
        else:
            m = jnp.max(jnp.max(sh, axis=0, keepdims=True), axis=1, keepdims=True)
        ps.append(jnp.exp2(sh - m).astype(jnp.bfloat16))
        ms.append(jnp.broadcast_to(m, (Q_BLOCK, 1)))
    return ps, ms


def _block_values(ps, ms, vwin, lo):
    lo_v = lax.broadcasted_iota(jnp.int32, vwin.shape, 1) < LANES // 2
    one = jnp.ones_like(vwin)
    pv0 = jnp.dot(ps[0], jnp.where(lo_v, vwin, one), preferred_element_type=jnp.float32)
    pv1 = jnp.dot(ps[1], jnp.where(lo_v, one, vwin), preferred_element_type=jnp.float32)
    acc = jnp.where(lo, pv0, pv1)
    l = pltpu.roll(jnp.where(lo, pv1, pv0), LANES // 2, axis=1)
    return acc, jnp.where(lo, ms[0], ms[1]), l


def _branch_multiplicity(n_q, n_cache):
    i = np.arange(n_q)[:, None]
    pos = np.arange(n_cache + n_q)[None, :]
    dist = n_cache + i - pos
    mult = np.zeros(dist.shape, np.float32)
    for window, dil in DILATED_PAIRS:
        mult += (dist >= 0) & (dist <= window) & (dist % dil == 0)
    return mult


def _sample_unit(ck_ref, cv_ref, q_ref, kn_ref, vn_ref, sg_ref, mc_ref, mn_ref,
                 nk_ref, nv_ref, att_ref, *, n_heads, head_dim, ts, n_cache, n_pairs):
    d_att = n_heads * head_dim
    rows = n_heads * ts
    head_of_lane = lax.broadcasted_iota(jnp.int32, (rows, d_att), 1) // head_dim
    head_of_row = lax.broadcasted_iota(jnp.int32, (rows, d_att), 0) // ts
    own = head_of_lane == head_of_row

    q = q_ref[0]
    qx = jnp.where(own, jnp.concatenate([q] * n_heads, axis=0), 0.0).astype(jnp.bfloat16)
    kct = ck_ref[0]
    vct = cv_ref[0]
    pad = jnp.zeros((LANES - ts, d_att), jnp.float32)
    knt = jnp.concatenate([pad, kn_ref[0]], axis=0).T
    vnt = jnp.concatenate([pad, vn_ref[0]], axis=0).T
    nt = (((1,), (1,)), ((), ()))
    s_c = jnp.dot(qx, kct.astype(jnp.bfloat16), preferred_element_type=jnp.float32)
    s_n = jnp.dot(qx, knt.astype(jnp.bfloat16), preferred_element_type=jnp.float32)
    mc = mc_ref[...]
    mn = mn_ref[...]
    s_c = jnp.where(mc > 0, s_c, NEG_BIG)
    s_n = jnp.where(mn > 0, s_n, NEG_BIG)
    m = jnp.maximum(jnp.max(s_c, axis=-1, keepdims=True), jnp.max(s_n, axis=-1, keepdims=True))
    e_c = jnp.exp2(s_c - m) * mc
    e_n = jnp.exp2(s_n - m) * mn
    l = jnp.sum(e_c, axis=-1, keepdims=True) + jnp.sum(e_n, axis=-1, keepdims=True)
    ox = (lax.dot_general(e_c.astype(jnp.bfloat16), vct.astype(jnp.bfloat16), nt,
                          preferred_element_type=jnp.float32)
          + lax.dot_general(e_n.astype(jnp.bfloat16), vnt.astype(jnp.bfloat16), nt,
                            preferred_element_type=jnp.float32))
    ox = jnp.where(own, ox * (1.0 / l), 0.0)
    o = ox[0:ts]
    for h in range(1, n_heads):
        o = o + ox[h * ts:(h + 1) * ts]
    att = o * sg_ref[0]
    for p in range(n_pairs):
        att_ref[0, p] = att[:, p * LANES:(p + 1) * LANES]

    main = n_cache - LANES
    is_old = lax.broadcasted_iota(jnp.int32, (d_att, LANES), 1) < LANES - ts
    for src, new_t, dst in ((kct, knt, nk_ref), (vct, vnt, nv_ref)):
        rolled = pltpu.roll(src, n_cache - ts, axis=1)
        dst[0, :, 0:main] = rolled[:, 0:main]
        dst[0, :, main:n_cache] = jnp.where(is_old, rolled[:, main:n_cache], new_t)


def _split_streams(dils, seq, n_parts):
    items = sorted(((seq // d // Q_BLOCK, i, r) for i, d in enumerate(dils) for r in range(d) if i > 0),
                   reverse=True)
    parts, loads = [[] for _ in range(n_parts)], [0] * n_parts
    loads[-1] = seq // dils[0] // Q_BLOCK
    for cost, i, r in items:
        j = loads.index(min(loads))
        parts[j].append((i, r))
        loads[j] += cost
    parts[-1].append((0, 0))
    return parts


def _attention_kernel(*refs, seq, dils, head_dim, parts, sample_cfg):
    n_br = len(dils)
    qkv = refs[:n_br]
    sg_ref = refs[n_br]
    sample_in = refs[n_br + 1:n_br + 9]
    att_ref = refs[n_br + 9]
    sample_out = refs[n_br + 10:n_br + 13]
    scr = refs[n_br + 13:]
    a_scr, m_scr, l_scr = scr[0::3], scr[1::3], scr[2::3]
    part = pl.program_id(2)

    lane = lax.broadcasted_iota(jnp.int32, (Q_BLOCK, LANES), 1)
    lo = lane < head_dim
    row = lax.broadcasted_iota(jnp.int32, (Q_BLOCK, 2 * Q_BLOCK), 0)
    col = lax.broadcasted_iota(jnp.int32, (Q_BLOCK, 2 * Q_BLOCK), 1)
    band = jnp.where((col >= row) & (col <= row + Q_BLOCK), 0.0, NEG_BIG).astype(jnp.float32)
    row1 = lax.broadcasted_iota(jnp.int32, (Q_BLOCK, Q_BLOCK), 0)
    col1 = lax.broadcasted_iota(jnp.int32, (Q_BLOCK, Q_BLOCK), 1)
    causal = jnp.where(col1 <= row1, 0.0, NEG_BIG).astype(jnp.float32)

    def merge(row0):
        rows = slice(row0, row0 + Q_BLOCK)
        ms = [m_ref[rows, :] for m_ref in m_scr]
        top = functools.reduce(jnp.maximum, ms)
        ws = [jnp.exp2(m - top) for m in ms]
        num = functools.reduce(lambda a, b: a + b, [w * a_ref[rows, :] for w, a_ref in zip(ws, a_scr)])
        den = functools.reduce(lambda a, b: a + b, [w * l_ref[rows, :] for w, l_ref in zip(ws, l_scr)])
        att = num * (1.0 / den) * sg_ref[rows, :].astype(jnp.float32)
        att_ref[rows, :] = att.astype(att_ref.dtype)

    def run_blocks(items, row_exact):
        blocks = [(i, r, qb) for i, r in items for qb in range(seq // dils[i] // Q_BLOCK)]
        l_min = None
        for g in range(0, len(blocks), BLOCK_GROUP):
            group = blocks[g:g + BLOCK_GROUP]
            keys = [slice(max(qb - 1, 0) * Q_BLOCK, (qb + 1) * Q_BLOCK) for _, _, qb in group]
            scores = [_block_scores(qkv[i][0, r, qb * Q_BLOCK:(qb + 1) * Q_BLOCK, :], qkv[i][1, r, ks, :], lo)
                      for (i, r, qb), ks in zip(group, keys)]
            probs = [_block_probs(s, causal if qb == 0 else band, row_exact)
                     for s, (_, _, qb) in zip(scores, group)]
            for (i, r, qb), ks, (ps, ms) in zip(group, keys, probs):
                vals = _block_values(ps, ms, qkv[i][2, r, ks, :], lo)
                d, row0 = dils[i], qb * Q_BLOCK
                idx = pl.ds(row0, Q_BLOCK) if d == 1 else pl.ds(r + d * row0, Q_BLOCK, stride=d)
                for ref, val in zip((a_scr[i], m_scr[i], l_scr[i]), vals):
                    ref[idx, :] = val
                l_min = vals[2] if l_min is None else jnp.minimum(l_min, vals[2])
            for i, _, qb in group:
                if i == 0:
                    merge(qb * Q_BLOCK)
        return l_min

    for idx, items in enumerate(parts):
        @pl.when(part == idx)
        def _(items=items):
            _sample_unit(*sample_in, *sample_out, **sample_cfg)
            l_min = run_blocks(items, False)

            @pl.when(jnp.min(l_min) < L_FLOOR)
            def _():
                run_blocks(items, True)


def _attention(qkv_st, sg_pm, cache_kt, cache_vt, q_s, kn_s, vn_s, sg_s, *, n_heads, head_dim):
    b, n_pairs, t, _ = sg_pm.shape
    nb, d_att, n_cache = cache_kt.shape
    ts = q_s.shape[1]
    dils = _dilations()
    for w, d in DILATED_PAIRS:
        assert w == d * Q_BLOCK and t % (d * Q_BLOCK) == 0
    n_parts = ATTN_PARTS
    steps = b * n_pairs * n_parts
    assert steps % nb == 0 and n_heads % (steps // nb) == 0
    groups = steps // nb
    heads_u = n_heads // groups
    d_u = heads_u * head_dim
    assert d_u % LANES == 0
    pairs_u = d_u // LANES
    rows_u = heads_u * ts

    def unit(i, j, k):
        u = (i * n_pairs + j) * n_parts + k
        return u // groups, u % groups

    ins, specs = [], []
    for a, d in zip(qkv_st, dils):
        assert a.shape == (b, n_pairs, 3, d, t // d, LANES)
        ins.append(a)
        specs.append(pl.BlockSpec((None, None, 3, d, t // d, LANES), lambda i, j, k: (i, j, 0, 0, 0, 0)))
    pair_spec = pl.BlockSpec((None, None, t, LANES), lambda i, j, k: (i, j, 0, 0))
    cache_spec = pl.BlockSpec((1, d_u, n_cache), lambda i, j, k: (unit(i, j, k)[0], unit(i, j, k)[1], 0))
    row_spec = pl.BlockSpec((1, ts, d_u), lambda i, j, k: (unit(i, j, k)[0], 0, unit(i, j, k)[1]))
    mult = np.tile(_branch_multiplicity(ts, n_cache), (heads_u, 1))
    mult_c = jnp.asarray(mult[:, :n_cache])
    mult_n = jnp.asarray(np.pad(mult[:, n_cache:], ((0, 0), (LANES - ts, 0))))
    const = lambda shape: pl.BlockSpec(shape, lambda i, j, k: (0, 0))
    sample_cfg = dict(n_heads=heads_u, head_dim=head_dim, ts=ts, n_cache=n_cache, n_pairs=pairs_u)
    kern = functools.partial(_attention_kernel, seq=t, dils=dils, head_dim=head_dim,
                             parts=_split_streams(dils, t, n_parts), sample_cfg=sample_cfg)
    vmem = (2 * (3 * len(dils) + 2) * t * LANES * 2 + 3 * len(dils) * t * LANES * 4
            + 14 * d_u * n_cache * 4 + 8 * rows_u * n_cache * 4 + 12 * MIB)
    return pl.pallas_call(
        kern,
        grid=(b, n_pairs, n_parts),
        in_specs=specs + [pair_spec, cache_spec, cache_spec, row_spec, row_spec, row_spec, row_spec,
                          const((rows_u, n_cache)), const((rows_u, LANES))],
        out_specs=[pair_spec, cache_spec, cache_spec,
                   pl.BlockSpec((1, pairs_u, ts, LANES),
                                lambda i, j, k: (0, unit(i, j, k)[1], unit(i, j, k)[0], 0))],
        out_shape=[jax.ShapeDtypeStruct((b, n_pairs, t, LANES), jnp.bfloat16),
                   jax.ShapeDtypeStruct((nb, d_att, n_cache), jnp.float32),
                   jax.ShapeDtypeStruct((nb, d_att, n_cache), jnp.float32),
                   jax.ShapeDtypeStruct((1, d_att // LANES, nb * ts, LANES), jnp.float32)],
        scratch_shapes=[pltpu.VMEM((t, LANES), jnp.float32) for _ in range(3 * len(dils))],
        compiler_params=pltpu.CompilerParams(
            dimension_semantics=("parallel", "parallel", "arbitrary"), vmem_limit_bytes=int(vmem)),
        name="attention",
    )(*ins, sg_pm, cache_kt, cache_vt, q_s, kn_s, vn_s, sg_s, mult_c, mult_n)


def _outproj_kernel(att_ref, z_ref, x_ref, w_ref, g_ref, b_ref, y_ref, w_bf, *, n_pairs, n_chunks):
    @pl.when((pl.program_id(0) == 0) & (pl.program_id(1) == 0))
    def _():
        _cast_rows(w_ref, w_bf)

    tm = x_ref.shape[1]
    rows = tm // n_chunks
    for c in range(n_chunks):
        sl = slice(c * rows, (c + 1) * rows)
        parts = [att_ref[0, p, sl, :].astype(jnp.bfloat16) for p in range(n_pairs)]
        mix = jnp.concatenate(parts + [z_ref[0, sl, :].astype(jnp.bfloat16)], axis=-1)
        out = jnp.dot(mix, w_bf[...], preferred_element_type=jnp.float32)
        h = DEEPNORM_ALPHA * x_ref[0, sl, :] + out
        mu = jnp.mean(h, axis=-1, keepdims=True)
        d = h - mu
        var = jnp.mean(d * d, axis=-1, keepdims=True)
        y_ref[0, sl, :] = d * lax.rsqrt(var + LN_EPS) * g_ref[...] + b_ref[...]


def _outproj(att_pm, z, x, w_out, ln_g, ln_b, *, tm):
    b, t, d = x.shape
    n_pairs = att_pm.shape[1]
    d_conv = z.shape[-1]
    d_mix = w_out.shape[0]
    vmem = (d_mix * d * (4 + 2) + 4 * tm * d * 4 + 2 * tm * n_pairs * LANES * att_pm.dtype.itemsize
            + 2 * tm * d_conv * 2 + 6 * tm * d * 4)
    return pl.pallas_call(
        functools.partial(_outproj_kernel, n_pairs=n_pairs, n_chunks=OUTPROJ_CHUNKS),
        grid=(b, t // tm),
        in_specs=[
            pl.BlockSpec((1, n_pairs, tm, LANES), lambda i, j: (i, 0, j, 0)),
            pl.BlockSpec((1, tm, d_conv), lambda i, j: (i, j, 0)),
            pl.BlockSpec((1, tm, d), lambda i, j: (i, j, 0)),
            pl.BlockSpec((d_mix, d), lambda i, j: (0, 0), pipeline_mode=pl.Buffered(1)),
            pl.BlockSpec((1, d), lambda i, j: (0, 0)),
            pl.BlockSpec((1, d), lambda i, j: (0, 0)),
        ],
        out_specs=pl.BlockSpec((1, tm, d), lambda i, j: (i, j, 0)),
        out_shape=jax.ShapeDtypeStruct((b, t, d), jnp.float32),
        scratch_shapes=[pltpu.VMEM((d_mix, d), jnp.bfloat16)],
        compiler_params=pltpu.CompilerParams(
            dimension_semantics=("arbitrary", "arbitrary"), vmem_limit_bytes=int(vmem)),
        name="outproj_layernorm",
    )(att_pm, z, x, w_out, ln_g.reshape(1, d), ln_b.reshape(1, d))


def _sample_inproj_kernel(x_ref, w_ref, cw_ref, st_ref,
                          q_ref, k_ref, v_ref, sg_ref, z_ref, nc_ref, u_scr,
                          *, nb, ts, d_att, d_conv, q_scale):
    x = x_ref[...].astype(jnp.bfloat16)

    def proj(c0, width):
        w = w_ref[:, c0:c0 + width].astype(jnp.bfloat16)
        return jnp.dot(x, w, preferred_element_type=jnp.float32)

    q_ref[...] = proj(0, d_att) * q_scale
    k_ref[...] = proj(d_att, d_att)
    v_ref[...] = proj(2 * d_att, d_att)
    sg_ref[...] = _silu(proj(3 * d_att, d_att))
    c0 = 4 * d_att
    gb = proj(c0, d_conv)
    gc = proj(c0 + d_conv, d_conv)
    hh = proj(c0 + 2 * d_conv, d_conv)
    g_conv = proj(c0 + 3 * d_conv, d_conv)
    u = (gc * hh).reshape(nb, ts, d_conv)
    u_scr[:, SUBLANES - 2:SUBLANES, :] = st_ref[...]
    u_scr[:, SUBLANES:SUBLANES + ts, :] = u
    um2 = u_scr[:, SUBLANES - 2:SUBLANES - 2 + ts, :]
    um1 = u_scr[:, SUBLANES - 1:SUBLANES - 1 + ts, :]
    cw = cw_ref[...]
    y_conv = cw[0:1, :][None] * um2 + cw[1:2, :][None] * um1 + cw[2:3, :][None] * u
    z = gb * y_conv.reshape(nb * ts, d_conv) * _silu(g_conv)
    z_ref[...] = z.astype(jnp.bfloat16)
    nc_ref[...] = u_scr[:, ts + SUBLANES - 2:ts + SUBLANES, :]


def _sample_inproj(x2, w_in, conv_w, state, *, nb, ts, d_att, d_conv, q_scale):
    rows, d = x2.shape
    f32 = lambda c: jax.ShapeDtypeStruct((rows, c), jnp.float32)
    kern = functools.partial(_sample_inproj_kernel, nb=nb, ts=ts, d_att=d_att, d_conv=d_conv, q_scale=q_scale)
    vmem = d * w_in.shape[1] * 4 + 16 * rows * d * 4 + 8 * MIB
    return pl.pallas_call(
        kern,
        out_shape=[f32(d_att), f32(d_att), f32(d_att), f32(d_att),
                   jax.ShapeDtypeStruct((rows, d_conv), jnp.bfloat16),
                   jax.ShapeDtypeStruct((nb, 2, d_conv), jnp.float32)],
        scratch_shapes=[pltpu.VMEM((nb, ts + SUBLANES, d_conv), jnp.float32)],
        compiler_params=pltpu.CompilerParams(vmem_limit_bytes=int(vmem)),
        name="sample_inproj",
    )(x2, w_in, conv_w, state)


def kernel(x_prompt, x_sample, cache_k, cache_v, state_conv, w_in, conv_w, w_out, ln_g, ln_b):
    b, t, d = x_prompt.shape
    nb, ts, _ = x_sample.shape
    _, n_cache, n_heads, head_dim = cache_k.shape
    d_att = n_heads * head_dim
    d_conv = conv_w.shape[1]
    assert w_in.shape == (d, 4 * d_att + 4 * d_conv) and w_out.shape == (d_att + d_conv, d)
    assert d_att % LANES == 0 and LANES == 2 * head_dim and conv_w.shape[0] == 3
    max_window = max(w for w, _ in DILATED_PAIRS)
    assert n_cache == max_window and t <= max_window
    q_scale = head_dim ** -0.5 * math.log2(math.e)

    def to_time_minor(a):
        return jnp.transpose(a, (0, 2, 3, 1)).reshape(a.shape[0], d_att, a.shape[1])

    def from_time_minor(a):
        return jnp.transpose(a.reshape(a.shape[0], n_heads, head_dim, a.shape[2]), (0, 3, 1, 2))

    conv0 = jnp.zeros((b, 2, d_conv), jnp.float32)
    outs = _inproj(x_prompt, w_in, conv_w, conv0, tm=512, d_att=d_att, d_conv=d_conv, q_scale=q_scale)
    n_d = len(DILATED_PAIRS)
    qkv_st = outs[:n_d]
    kt_new, vt_new, sg_pm, z, new_conv_prompt = outs[n_d:]

    q_s, k_s, v_s, sg_s, z_s, new_conv_sample = _sample_inproj(
        x_sample.reshape(nb * ts, d), w_in, conv_w, state_conv, nb=nb, ts=ts, d_att=d_att, d_conv=d_conv,
        q_scale=q_scale)
    three = lambda a: a.reshape(nb, ts, d_att)

    att_pm, new_kt_s, new_vt_s, att_s = _attention(
        qkv_st, sg_pm, to_time_minor(cache_k), to_time_minor(cache_v),
        three(q_s), three(k_s), three(v_s), three(sg_s), n_heads=n_heads, head_dim=head_dim)

    y_prompt = _outproj(att_pm, z, x_prompt, w_out, ln_g, ln_b, tm=1024)
    y_sample = _outproj(att_s, z_s.reshape(1, nb * ts, d_conv), x_sample.reshape(1, nb * ts, d),
                        w_out, ln_g, ln_b, tm=nb * ts)

    return (y_prompt, y_sample.reshape(nb, ts, d),
            from_time_minor(kt_new), from_time_minor(vt_new), new_conv_prompt,
            from_time_minor(new_kt_s), from_time_minor(new_vt_s), new_conv_sample)
```

```python
import functools
import math

import numpy as np
import jax
import jax.numpy as jnp
from jax import lax
from jax.experimental import pallas as pl
from jax.experimental.pallas import tpu as pltpu

DILATED_PAIRS = ((128, 1), (512, 4), (2048, 16))
LN_EPS = 1e-5
DEPTH = 1
DEEPNORM_ALPHA = (2.0 * DEPTH) ** 0.25

LANES = 128
SUBLANES = 8
Q_BLOCK = 128
NEG_BIG = -1e30
ATTN_PARTS = 2
OUTPROJ_CHUNKS = 2
CAST_ROWS = 128
BLOCK_GROUP = 8
L_FLOOR = 2.0 ** -100

MIB = 1024 * 1024


def _dilations():
    dils = tuple(sorted(d for _, d in DILATED_PAIRS))
    assert dils[0] == 1 and all(b % a == 0 for a, b in zip(dils, dils[1:]))
    return dils


def _silu(x):
    return x * (1.0 / (1.0 + jnp.exp(-x)))


def _new_rows_t(rows):
    ts, d = rows.shape
    pad = jnp.zeros((LANES - ts, d), rows.dtype)
    return jnp.concatenate([pad, rows], axis=0).T


def _shift_cache(old, new_t, ts, dst_ref):
    d, n_cache = old.shape
    main = n_cache - LANES
    is_old = lax.broadcasted_iota(jnp.int32, (d, LANES), 1) < LANES - ts
    rolled = pltpu.roll(old, n_cache - ts, axis=1)
    dst_ref[0, :, 0:main] = rolled[:, 0:main]
    dst_ref[0, :, main:n_cache] = jnp.where(is_old, rolled[:, main:n_cache], new_t)


def _cast_rows(src_ref, dst_ref):
    n = src_ref.shape[0] // CAST_ROWS

    def body(i, carry):
        rows = pl.ds(pl.multiple_of(i * CAST_ROWS, CAST_ROWS), CAST_ROWS)
        dst_ref[rows, :] = src_ref[rows, :].astype(dst_ref.dtype)
        return carry

    lax.fori_loop(0, n, body, 0)


def _emit_streams(val, refs, which, dils, scr, tm):
    n_pairs = val.shape[1] // LANES
    for p in range(n_pairs):
        col = val[:, p * LANES:(p + 1) * LANES]
        refs[0][0, p, which, 0] = col.astype(jnp.bfloat16)
        if len(dils) > 1:
            scr[0][...] = col
        for lvl in range(1, len(dils)):
            d_prev = dils[lvl - 1]
            f = dils[lvl] // d_prev
            n_prev = tm // d_prev
            n = n_prev // f
            src, dst = scr[(lvl - 1) % 2], scr[lvl % 2]
            for s in range(d_prev):
                for r2 in range(f):
                    r = s + d_prev * r2
                    rows = src[pl.ds(s * n_prev + r2, n, stride=f), :]
                    refs[lvl][0, p, which, r] = rows.astype(jnp.bfloat16)
                    if lvl + 1 < len(dils):
                        dst[r * n:(r + 1) * n, :] = rows


def _inproj_kernel(*refs, tm, d_att, d_conv, n_pairs, q_scale, dils):
    x_ref, w_bf, cw_ref, hist_ref, cv_ref, vn_ref = refs[:6]
    n_d = len(dils)
    st_refs = refs[6:6 + n_d]
    kt_ref, vt_ref, sg_ref, z_ref, nc_ref, nv_ref, u_scr, s_a, s_b = refs[6 + n_d:]

    @pl.when(pl.program_id(1) == 0)
    def _():
        u_scr[SUBLANES - 2:SUBLANES, :] = hist_ref[0]

    _shift_cache(cv_ref[0], _new_rows_t(vn_ref[0]), vn_ref.shape[1], nv_ref)

    x = x_ref[0].astype(jnp.bfloat16)

    def proj(c0, width):
        return jnp.dot(x, w_bf[:, c0:c0 + width], preferred_element_type=jnp.float32)

    c0 = 4 * d_att
    gc = proj(c0 + d_conv, d_conv)
    hh = proj(c0 + 2 * d_conv, d_conv)
    u = gc * hh
    u_scr[SUBLANES:SUBLANES + tm, :] = u
    um2 = u_scr[SUBLANES - 2:SUBLANES - 2 + tm, :]
    um1 = u_scr[SUBLANES - 1:SUBLANES - 1 + tm, :]
    y_conv = cw_ref[0:1, :] * um2 + cw_ref[1:2, :] * um1 + cw_ref[2:3, :] * u
    tail = u[tm - 2:tm, :]
    u_scr[SUBLANES - 2:SUBLANES, :] = tail
    nc_ref[0] = tail
    gb = proj(c0, d_conv)
    g_conv = proj(c0 + 3 * d_conv, d_conv)
    z_ref[0] = (gb * y_conv * _silu(g_conv)).astype(jnp.bfloat16)

    sg = _silu(proj(3 * d_att, d_att)).astype(jnp.bfloat16)
    for p in range(n_pairs):
        sg_ref[0, p] = sg[:, p * LANES:(p + 1) * LANES]
    k = proj(d_att, d_att)
    kt_ref[0] = k.T
    _emit_streams(k, st_refs, 1, dils, (s_a, s_b), tm)
    v = proj(2 * d_att, d_att)
    vt_ref[0] = v.T
    _emit_streams(v, st_refs, 2, dils, (s_a, s_b), tm)
    q = proj(0, d_att) * q_scale
    _emit_streams(q, st_refs, 0, dils, (s_a, s_b), tm)


def _inproj(x, w_in_bf, conv_w, hist, cache_vt, vn_s, *, tm, d_att, d_conv, q_scale):
    b, t, d = x.shape
    n_pairs = d_att // LANES
    n_out = w_in_bf.shape[1]
    dils = _dilations()
    assert tm % (dils[-1] * 2 * SUBLANES) == 0
    kern = functools.partial(_inproj_kernel, tm=tm, d_att=d_att, d_conv=d_conv,
                             n_pairs=n_pairs, q_scale=q_scale, dils=dils)
    st_shapes = [jax.ShapeDtypeStruct((b, n_pairs, 3, dd, t // dd, LANES), jnp.bfloat16) for dd in dils]
    st_specs = [pl.BlockSpec((1, n_pairs, 3, dd, tm // dd, LANES), lambda i, j: (i, 0, 0, 0, j, 0))
                for dd in dils]
    pm = jax.ShapeDtypeStruct((b, n_pairs, t, LANES), jnp.bfloat16)
    pm_spec = pl.BlockSpec((1, n_pairs, tm, LANES), lambda i, j: (i, 0, j, 0))
    row_spec = lambda c: pl.BlockSpec((1, tm, c), lambda i, j: (i, j, 0))
    col_spec = pl.BlockSpec((1, d_att, tm), lambda i, j: (i, 0, j))
    nb, _, n_cache = cache_vt.shape
    ts = vn_s.shape[1]
    n_t = t // tm
    assert nb == b * n_t
    cache_spec = pl.BlockSpec((1, d_att, n_cache), lambda i, j: (i * n_t + j, 0, 0))
    vmem = (d * n_out * 2 + 2 * tm * d * 4
            + 2 * tm * ((3 * len(dils) + 1) * d_att * 2 + 2 * d_att * 4 + d_conv * 2)
            + (tm + SUBLANES) * d_conv * 4 + 2 * tm * LANES * 4 + 12 * tm * d_att * 4
            + 5 * d_att * n_cache * 4)
    return pl.pallas_call(
        kern,
        grid=(b, n_t),
        in_specs=[
            row_spec(d),
            pl.BlockSpec((d, n_out), lambda i, j: (0, 0), pipeline_mode=pl.Buffered(1)),
            pl.BlockSpec(conv_w.shape, lambda i, j: (0, 0)),
            pl.BlockSpec((1, 2, d_conv), lambda i, j: (i, 0, 0)),
            cache_spec,
            pl.BlockSpec((1, ts, d_att), lambda i, j: (i * n_t + j, 0, 0)),
        ],
        out_specs=st_specs + [col_spec, col_spec, pm_spec,
                              row_spec(d_conv), pl.BlockSpec((1, 2, d_conv), lambda i, j: (i, 0, 0)),
                              cache_spec],
        out_shape=st_shapes + [
            jax.ShapeDtypeStruct((b, d_att, t), jnp.float32),
            jax.ShapeDtypeStruct((b, d_att, t), jnp.float32),
            pm,
            jax.ShapeDtypeStruct((b, t, d_conv), jnp.bfloat16),
            jax.ShapeDtypeStruct((b, 2, d_conv), jnp.float32),
            jax.ShapeDtypeStruct((nb, d_att, n_cache), jnp.float32)],
        scratch_shapes=[pltpu.VMEM((tm + SUBLANES, d_conv), jnp.float32),
                        pltpu.VMEM((tm, LANES), jnp.float32),
                        pltpu.VMEM((tm, LANES), jnp.float32)],
        compiler_params=pltpu.CompilerParams(
            dimension_semantics=("parallel", "arbitrary"), vmem_limit_bytes=int(vmem)),
        name="prompt_inproj",
    )(x, w_in_bf, conv_w, hist, cache_vt, vn_s)


def _block_scores(q, kwin, lo):
    zero = jnp.zeros_like(q)
    q2 = jnp.concatenate([jnp.where(lo, q, zero), jnp.where(lo, zero, q)], axis=0)
    return lax.dot_general(q2, kwin, (((1,), (1,)), ((), ())), preferred_element_type=jnp.float32)


def _block_probs(s, bias, row_exact):
    ps, ms = [], []
    for h in range(2):
        sh = s[h * Q_BLOCK:(h + 1) * Q_BLOCK] + bias
        if row_exact:
            m = jnp.max(sh, axis=-1, keepdims=True)
        else:
            m = jnp.max(jnp.max(sh, axis=0, keepdims=True), axis=1, keepdims=True)
        ps.append(jnp.exp2(sh - m).astype(jnp.bfloat16))
        ms.append(jnp.broadcast_to(m, (Q_BLOCK, 1)))
    return ps, ms


def _block_values(ps, ms, vwin, lo):
    lo_v = lax.broadcasted_iota(jnp.int32, vwin.shape, 1) < LANES // 2
    one = jnp.ones_like(vwin)
    pv0 = jnp.dot(ps[0], jnp.where(lo_v, vwin, one), preferred_element_type=jnp.float32)
    pv1 = jnp.dot(ps[1], jnp.where(lo_v, one, vwin), preferred_element_type=jnp.float32)
    acc = jnp.where(lo, pv0, pv1)
    l = pltpu.roll(jnp.where(lo, pv1, pv0), LANES // 2, axis=1)
    return acc, jnp.where(lo, ms[0], ms[1]), l


def _branch_multiplicity(n_q, n_cache):
    i = np.arange(n_q)[:, None]
    pos = np.arange(n_cache + n_q)[None, :]
    dist = n_cache + i - pos
    mult = np.zeros(dist.shape, np.float32)
    for window, dil in DILATED_PAIRS:
        mult += (dist >= 0) & (dist <= window) & (dist % dil == 0)
    return mult


def _sample_unit(ck_ref, cv_ref, q_ref, kn_ref, vn_ref, sg_ref, mc_ref, mn_ref,
                 nk_ref, att_ref, *, n_heads, head_dim, ts, n_cache, n_pairs):
    d_att = n_heads * head_dim
    rows = n_heads * ts
    head_of_lane = lax.broadcasted_iota(jnp.int32, (rows, d_att), 1) // head_dim
    head_of_row = lax.broadcasted_iota(jnp.int32, (rows, d_att), 0) // ts
    own = head_of_lane == head_of_row

    q = q_ref[0]
    qx = jnp.where(own, jnp.concatenate([q] * n_heads, axis=0), 0.0).astype(jnp.bfloat16)
    kct = ck_ref[0]
    vct = cv_ref[0]
    knt = _new_rows_t(kn_ref[0])
    vnt = _new_rows_t(vn_ref[0])
    nt = (((1,), (1,)), ((), ()))
    s_c = jnp.dot(qx, kct.astype(jnp.bfloat16), preferred_element_type=jnp.float32)
    s_n = jnp.dot(qx, knt.astype(jnp.bfloat16), preferred_element_type=jnp.float32)
    mc = mc_ref[...]
    mn = mn_ref[...]
    s_c = jnp.where(mc > 0, s_c, NEG_BIG)
    s_n = jnp.where(mn > 0, s_n, NEG_BIG)
    m = jnp.maximum(jnp.max(s_c, axis=-1, keepdims=True), jnp.max(s_n, axis=-1, keepdims=True))
    e_c = jnp.exp2(s_c - m) * mc
    e_n = jnp.exp2(s_n - m) * mn
    l = jnp.sum(e_c, axis=-1, keepdims=True) + jnp.sum(e_n, axis=-1, keepdims=True)
    ox = (lax.dot_general(e_c.astype(jnp.bfloat16), vct.astype(jnp.bfloat16), nt,
                          preferred_element_type=jnp.float32)
          + lax.dot_general(e_n.astype(jnp.bfloat16), vnt.astype(jnp.bfloat16), nt,
                            preferred_element_type=jnp.float32))
    ox = jnp.where(own, ox * (1.0 / l), 0.0)
    o = ox[0:ts]
    for h in range(1, n_heads):
        o = o + ox[h * ts:(h + 1) * ts]
    att = o * sg_ref[0]
    for p in range(n_pairs):
        att_ref[0, p] = att[:, p * LANES:(p + 1) * LANES]

    _shift_cache(kct, knt, ts, nk_ref)


def _split_streams(dils, seq, n_parts):
    items = sorted(((seq // d // Q_BLOCK, i, r) for i, d in enumerate(dils) for r in range(d) if i > 0),
                   reverse=True)
    parts, loads = [[] for _ in range(n_parts)], [0] * n_parts
    loads[-1] = seq // dils[0] // Q_BLOCK
    for cost, i, r in items:
        j = loads.index(min(loads))
        parts[j].append((i, r))
        loads[j] += cost
    parts[-1].append((0, 0))
    return parts


def _attention_kernel(*refs, seq, dils, head_dim, parts, sample_cfg):
    n_br = len(dils)
    qkv = refs[:n_br]
    sg_ref = refs[n_br]
    sample_in = refs[n_br + 1:n_br + 9]
    att_ref = refs[n_br + 9]
    sample_out = refs[n_br + 10:n_br + 12]
    scr = refs[n_br + 12:]
    a_scr, m_scr, l_scr = scr[0::3], scr[1::3], scr[2::3]
    part = pl.program_id(2)

    lane = lax.broadcasted_iota(jnp.int32, (Q_BLOCK, LANES), 1)
    lo = lane < head_dim
    row = lax.broadcasted_iota(jnp.int32, (Q_BLOCK, 2 * Q_BLOCK), 0)
    col = lax.broadcasted_iota(jnp.int32, (Q_BLOCK, 2 * Q_BLOCK), 1)
    band = jnp.where((col >= row) & (col <= row + Q_BLOCK), 0.0, NEG_BIG).astype(jnp.float32)
    row1 = lax.broadcasted_iota(jnp.int32, (Q_BLOCK, Q_BLOCK), 0)
    col1 = lax.broadcasted_iota(jnp.int32, (Q_BLOCK, Q_BLOCK), 1)
    causal = jnp.where(col1 <= row1, 0.0, NEG_BIG).astype(jnp.float32)

    def merge(row0):
        rows = slice(row0, row0 + Q_BLOCK)
        ms = [m_ref[rows, :] for m_ref in m_scr]
        top = functools.reduce(jnp.maximum, ms)
        ws = [jnp.exp2(m - top) for m in ms]
        num = functools.reduce(lambda a, b: a + b, [w * a_ref[rows, :] for w, a_ref in zip(ws, a_scr)])
        den = functools.reduce(lambda a, b: a + b, [w * l_ref[rows, :] for w, l_ref in zip(ws, l_scr)])
        att = num * (1.0 / den) * sg_ref[rows, :].astype(jnp.float32)
        att_ref[rows, :] = att.astype(att_ref.dtype)

    def run_blocks(items, row_exact):
        blocks = [(i, r, qb) for i, r in items for qb in range(seq // dils[i] // Q_BLOCK)]
        l_min = None
        for g in range(0, len(blocks), BLOCK_GROUP):
            group = blocks[g:g + BLOCK_GROUP]
            keys = [slice(max(qb - 1, 0) * Q_BLOCK, (qb + 1) * Q_BLOCK) for _, _, qb in group]
            scores = [_block_scores(qkv[i][0, r, qb * Q_BLOCK:(qb + 1) * Q_BLOCK, :], qkv[i][1, r, ks, :], lo)
                      for (i, r, qb), ks in zip(group, keys)]
            probs = [_block_probs(s, causal if qb == 0 else band, row_exact)
                     for s, (_, _, qb) in zip(scores, group)]
            for (i, r, qb), ks, (ps, ms) in zip(group, keys, probs):
                vals = _block_values(ps, ms, qkv[i][2, r, ks, :], lo)
                d, row0 = dils[i], qb * Q_BLOCK
                idx = pl.ds(row0, Q_BLOCK) if d == 1 else pl.ds(r + d * row0, Q_BLOCK, stride=d)
                for ref, val in zip((a_scr[i], m_scr[i], l_scr[i]), vals):
                    ref[idx, :] = val
                l_min = vals[2] if l_min is None else jnp.minimum(l_min, vals[2])
            for i, _, qb in group:
                if i == 0:
                    merge(qb * Q_BLOCK)
        return l_min

    for idx, items in enumerate(parts):
        @pl.when(part == idx)
        def _(items=items):
            _sample_unit(*sample_in, *sample_out, **sample_cfg)
            l_min = run_blocks(items, False)

            @pl.when(jnp.min(l_min) < L_FLOOR)
            def _():
                run_blocks(items, True)


def _attention(qkv_st, sg_pm, cache_kt, cache_vt, q_s, kn_s, vn_s, sg_s, *, n_heads, head_dim):
    b, n_pairs, t, _ = sg_pm.shape
    nb, d_att, n_cache = cache_kt.shape
    ts = q_s.shape[1]
    dils = _dilations()
    for w, d in DILATED_PAIRS:
        assert w == d * Q_BLOCK and t % (d * Q_BLOCK) == 0
    n_parts = ATTN_PARTS
    steps = b * n_pairs * n_parts
    assert steps % nb == 0 and n_heads % (steps // nb) == 0
    groups = steps // nb
    heads_u = n_heads // groups
    d_u = heads_u * head_dim
    assert d_u % LANES == 0
    pairs_u = d_u // LANES
    rows_u = heads_u * ts

    def unit(i, j, k):
        u = (i * n_pairs + j) * n_parts + k
        return u // groups, u % groups

    ins, specs = [], []
    for a, d in zip(qkv_st, dils):
        assert a.shape == (b, n_pairs, 3, d, t // d, LANES)
        ins.append(a)
        specs.append(pl.BlockSpec((None, None, 3, d, t // d, LANES), lambda i, j, k: (i, j, 0, 0, 0, 0)))
    pair_spec = pl.BlockSpec((None, None, t, LANES), lambda i, j, k: (i, j, 0, 0))
    cache_spec = pl.BlockSpec((1, d_u, n_cache), lambda i, j, k: (unit(i, j, k)[0], unit(i, j, k)[1], 0))
    row_spec = pl.BlockSpec((1, ts, d_u), lambda i, j, k: (unit(i, j, k)[0], 0, unit(i, j, k)[1]))
    mult = np.tile(_branch_multiplicity(ts, n_cache), (heads_u, 1))
    mult_c = jnp.asarray(mult[:, :n_cache])
    mult_n = jnp.asarray(np.pad(mult[:, n_cache:], ((0, 0), (LANES - ts, 0))))
    const = lambda shape: pl.BlockSpec(shape, lambda i, j, k: (0, 0))
    sample_cfg = dict(n_heads=heads_u, head_dim=head_dim, ts=ts, n_cache=n_cache, n_pairs=pairs_u)
    kern = functools.partial(_attention_kernel, seq=t, dils=dils, head_dim=head_dim,
                             parts=_split_streams(dils, t, n_parts), sample_cfg=sample_cfg)
    vmem = (2 * (3 * len(dils) + 2) * t * LANES * 2 + 3 * len(dils) * t * LANES * 4
            + 14 * d_u * n_cache * 4 + 8 * rows_u * n_cache * 4 + 12 * MIB)
    return pl.pallas_call(
        kern,
        grid=(b, n_pairs, n_parts),
        in_specs=specs + [pair_spec, cache_spec, cache_spec, row_spec, row_spec, row_spec, row_spec,
                          const((rows_u, n_cache)), const((rows_u, LANES))],
        out_specs=[pair_spec, cache_spec,
                   pl.BlockSpec((1, pairs_u, ts, LANES),
                                lambda i, j, k: (0, unit(i, j, k)[1], unit(i, j, k)[0], 0))],
        out_shape=[jax.ShapeDtypeStruct((b, n_pairs, t, LANES), jnp.bfloat16),
                   jax.ShapeDtypeStruct((nb, d_att, n_cache), jnp.float32),
                   jax.ShapeDtypeStruct((1, d_att // LANES, nb * ts, LANES), jnp.float32)],
        scratch_shapes=[pltpu.VMEM((t, LANES), jnp.float32) for _ in range(3 * len(dils))],
        compiler_params=pltpu.CompilerParams(
            dimension_semantics=("parallel", "parallel", "arbitrary"), vmem_limit_bytes=int(vmem)),
        name="attention",
    )(*ins, sg_pm, cache_kt, cache_vt, q_s, kn_s, vn_s, sg_s, mult_c, mult_n)


def _outproj_kernel(att_ref, z_ref, x_ref, w_ref, g_ref, b_ref, y_ref, w_bf, *, n_pairs, n_chunks):
    @pl.when((pl.program_id(0) == 0) & (pl.program_id(1) == 0))
    def _():
        _cast_rows(w_ref, w_bf)

    tm = x_ref.shape[1]
    rows = tm // n_chunks
    for c in range(n_chunks):
        sl = slice(c * rows, (c + 1) * rows)
        parts = [att_ref[0, p, sl, :].astype(jnp.bfloat16) for p in range(n_pairs)]
        mix = jnp.concatenate(parts + [z_ref[0, sl, :].astype(jnp.bfloat16)], axis=-1)
        out = jnp.dot(mix, w_bf[...], preferred_element_type=jnp.float32)
        h = DEEPNORM_ALPHA * x_ref[0, sl, :] + out
        mu = jnp.mean(h, axis=-1, keepdims=True)
        d = h - mu
        var = jnp.mean(d * d, axis=-1, keepdims=True)
        y_ref[0, sl, :] = d * lax.rsqrt(var + LN_EPS) * g_ref[...] + b_ref[...]


def _outproj(att_pm, z, x, w_out, ln_g, ln_b, *, tm):
    b, t, d = x.shape
    n_pairs = att_pm.shape[1]
    d_conv = z.shape[-1]
    d_mix = w_out.shape[0]
    vmem = (d_mix * d * (4 + 2) + 4 * tm * d * 4 + 2 * tm * n_pairs * LANES * att_pm.dtype.itemsize
            + 2 * tm * d_conv * 2 + 6 * tm * d * 4)
    return pl.pallas_call(
        functools.partial(_outproj_kernel, n_pairs=n_pairs, n_chunks=OUTPROJ_CHUNKS),
        grid=(b, t // tm),
        in_specs=[
            pl.BlockSpec((1, n_pairs, tm, LANES), lambda i, j: (i, 0, j, 0)),
            pl.BlockSpec((1, tm, d_conv), lambda i, j: (i, j, 0)),
            pl.BlockSpec((1, tm, d), lambda i, j: (i, j, 0)),
            pl.BlockSpec((d_mix, d), lambda i, j: (0, 0), pipeline_mode=pl.Buffered(1)),
            pl.BlockSpec((1, d), lambda i, j: (0, 0)),
            pl.BlockSpec((1, d), lambda i, j: (0, 0)),
        ],
        out_specs=pl.BlockSpec((1, tm, d), lambda i, j: (i, j, 0)),
        out_shape=jax.ShapeDtypeStruct((b, t, d), jnp.float32),
        scratch_shapes=[pltpu.VMEM((d_mix, d), jnp.bfloat16)],
        compiler_params=pltpu.CompilerParams(
            dimension_semantics=("arbitrary", "arbitrary"), vmem_limit_bytes=int(vmem)),
        name="outproj_layernorm",
    )(att_pm, z, x, w_out, ln_g.reshape(1, d), ln_b.reshape(1, d))


def _sample_inproj_kernel(x_ref, w_ref, cw_ref, st_ref,
                          q_ref, k_ref, v_ref, sg_ref, z_ref, nc_ref, u_scr,
                          *, nb, ts, d_att, d_conv, q_scale):
    x = x_ref[...].astype(jnp.bfloat16)

    def proj(c0, width):
        return jnp.dot(x, w_ref[:, c0:c0 + width], preferred_element_type=jnp.float32)

    q_ref[...] = proj(0, d_att) * q_scale
    k_ref[...] = proj(d_att, d_att)
    v_ref[...] = proj(2 * d_att, d_att)
    sg_ref[...] = _silu(proj(3 * d_att, d_att))
    c0 = 4 * d_att
    gb = proj(c0, d_conv)
    gc = proj(c0 + d_conv, d_conv)
    hh = proj(c0 + 2 * d_conv, d_conv)
    g_conv = proj(c0 + 3 * d_conv, d_conv)
    u = (gc * hh).reshape(nb, ts, d_conv)
    u_scr[:, SUBLANES - 2:SUBLANES, :] = st_ref[...]
    u_scr[:, SUBLANES:SUBLANES + ts, :] = u
    um2 = u_scr[:, SUBLANES - 2:SUBLANES - 2 + ts, :]
    um1 = u_scr[:, SUBLANES - 1:SUBLANES - 1 + ts, :]
    cw = cw_ref[...]
    y_conv = cw[0:1, :][None] * um2 + cw[1:2, :][None] * um1 + cw[2:3, :][None] * u
    z = gb * y_conv.reshape(nb * ts, d_conv) * _silu(g_conv)
    z_ref[...] = z.astype(jnp.bfloat16)
    nc_ref[...] = u_scr[:, ts + SUBLANES - 2:ts + SUBLANES, :]


def _sample_inproj(x2, w_in_bf, conv_w, state, *, nb, ts, d_att, d_conv, q_scale):
    rows, d = x2.shape
    f32 = lambda c: jax.ShapeDtypeStruct((rows, c), jnp.float32)
    kern = functools.partial(_sample_inproj_kernel, nb=nb, ts=ts, d_att=d_att, d_conv=d_conv, q_scale=q_scale)
    vmem = d * w_in_bf.shape[1] * 2 + 16 * rows * d * 4 + 8 * MIB
    return pl.pallas_call(
        kern,
        out_shape=[f32(d_att), f32(d_att), f32(d_att), f32(d_att),
                   jax.ShapeDtypeStruct((rows, d_conv), jnp.bfloat16),
                   jax.ShapeDtypeStruct((nb, 2, d_conv), jnp.float32)],
        scratch_shapes=[pltpu.VMEM((nb, ts + SUBLANES, d_conv), jnp.float32)],
        compiler_params=pltpu.CompilerParams(vmem_limit_bytes=int(vmem)),
        name="sample_inproj",
    )(x2, w_in_bf, conv_w, state)


def kernel(x_prompt, x_sample, cache_k, cache_v, state_conv, w_in, conv_w, w_out, ln_g, ln_b):
    b, t, d = x_prompt.shape
    nb, ts, _ = x_sample.shape
    _, n_cache, n_heads, head_dim = cache_k.shape
    d_att = n_heads * head_dim
    d_conv = conv_w.shape[1]
    assert w_in.shape == (d, 4 * d_att + 4 * d_conv) and w_out.shape == (d_att + d_conv, d)
    assert d_att % LANES == 0 and LANES == 2 * head_dim and conv_w.shape[0] == 3
    max_window = max(w for w, _ in DILATED_PAIRS)
    assert n_cache == max_window and t <= max_window
    q_scale = head_dim ** -0.5 * math.log2(math.e)

    def to_time_minor(a):
        return jnp.transpose(a, (0, 2, 3, 1)).reshape(a.shape[0], d_att, a.shape[1])

    def from_time_minor(a):
        return jnp.transpose(a.reshape(a.shape[0], n_heads, head_dim, a.shape[2]), (0, 3, 1, 2))

    w_in_bf = w_in.astype(jnp.bfloat16)
    q_s, k_s, v_s, sg_s, z_s, new_conv_sample = _sample_inproj(
        x_sample.reshape(nb * ts, d), w_in_bf, conv_w, state_conv, nb=nb, ts=ts, d_att=d_att, d_conv=d_conv,
        q_scale=q_scale)
    three = lambda a: a.reshape(nb, ts, d_att)
    cache_kt, cache_vt = to_time_minor(cache_k), to_time_minor(cache_v)

    conv0 = jnp.zeros((b, 2, d_conv), jnp.float32)
    outs = _inproj(x_prompt, w_in_bf, conv_w, conv0, cache_vt, three(v_s),
                   tm=512, d_att=d_att, d_conv=d_conv, q_scale=q_scale)
    n_d = len(DILATED_PAIRS)
    qkv_st = outs[:n_d]
    kt_new, vt_new, sg_pm, z, new_conv_prompt, new_vt_s = outs[n_d:]

    att_pm, new_kt_s, att_s = _attention(
        qkv_st, sg_pm, cache_kt, cache_vt,
        three(q_s), three(k_s), three(v_s), three(sg_s), n_heads=n_heads, head_dim=head_dim)

    y_prompt = _outproj(att_pm, z, x_prompt, w_out, ln_g, ln_b, tm=1024)
    y_sample = _outproj(att_s, z_s.reshape(1, nb * ts, d_conv), x_sample.reshape(1, nb * ts, d),
                        w_out, ln_g, ln_b, tm=nb * ts)

    return (y_prompt, y_sample.reshape(nb, ts, d),
            from_time_minor(kt_new), from_time_minor(vt_new), new_conv_prompt,
            from_time_minor(new_kt_s), from_time_minor(new_vt_s), new_conv_sample)
```

```python
import functools
import math

import numpy as np
import jax
import jax.numpy as jnp
from jax import lax
from jax.experimental import pallas as pl
from jax.experimental.pallas import tpu as pltpu

DILATED_PAIRS = ((128, 1), (512, 4), (2048, 16))
LN_EPS = 1e-5
DEPTH = 1
DEEPNORM_ALPHA = (2.0 * DEPTH) ** 0.25

LANES = 128
SUBLANES = 8
Q_BLOCK = 128
NEG_BIG = -1e30
ATTN_PARTS = 2
CAST_ROWS = 128
BLOCK_GROUP = 8
L_FLOOR = 2.0 ** -100

MIB = 1024 * 1024


def _dilations():
    dils = tuple(sorted(d for _, d in DILATED_PAIRS))
    assert dils[0] == 1 and all(b % a == 0 for a, b in zip(dils, dils[1:]))
    return dils


def _silu(x):
    return x * (1.0 / (1.0 + jnp.exp(-x)))


def _new_rows_t(rows):
    ts, d = rows.shape
    pad = jnp.zeros((LANES - ts, d), rows.dtype)
    return jnp.concatenate([pad, rows], axis=0).T


def _shift_cache(old, new_t, ts, dst_ref):
    d, n_cache = old.shape
    main = n_cache - LANES
    is_old = lax.broadcasted_iota(jnp.int32, (d, LANES), 1) < LANES - ts
    rolled = pltpu.roll(old, n_cache - ts, axis=1)
    dst_ref[0, :, 0:main] = rolled[:, 0:main]
    dst_ref[0, :, main:n_cache] = jnp.where(is_old, rolled[:, main:n_cache], new_t)


def _cast_rows(src_ref, dst_ref):
    n = src_ref.shape[0] // CAST_ROWS

    def body(i, carry):
        rows = pl.ds(pl.multiple_of(i * CAST_ROWS, CAST_ROWS), CAST_ROWS)
        dst_ref[rows, :] = src_ref[rows, :].astype(dst_ref.dtype)
        return carry

    lax.fori_loop(0, n, body, 0)


def _emit_streams(val, refs, which, dils, scr, tm):
    n_pairs = val.shape[1] // LANES
    for p in range(n_pairs):
        col = val[:, p * LANES:(p + 1) * LANES]
        refs[0][0, p, which, 0] = col.astype(jnp.bfloat16)
        if len(dils) > 1:
            scr[0][...] = col
        for lvl in range(1, len(dils)):
            d_prev = dils[lvl - 1]
            f = dils[lvl] // d_prev
            n_prev = tm // d_prev
            n = n_prev // f
            src, dst = scr[(lvl - 1) % 2], scr[lvl % 2]
            for s in range(d_prev):
                for r2 in range(f):
                    r = s + d_prev * r2
                    rows = src[pl.ds(s * n_prev + r2, n, stride=f), :]
                    refs[lvl][0, p, which, r] = rows.astype(jnp.bfloat16)
                    if lvl + 1 < len(dils):
                        dst[r * n:(r + 1) * n, :] = rows


def _inproj_kernel(*refs, tm, d_att, d_conv, n_pairs, q_scale, dils):
    x_ref, w_bf, cw_ref, hist_ref, cv_ref, vn_ref = refs[:6]
    n_d = len(dils)
    st_refs = refs[6:6 + n_d]
    kt_ref, vt_ref, sg_ref, z_ref, nc_ref, nv_ref, u_scr, s_a, s_b = refs[6 + n_d:]

    @pl.when(pl.program_id(1) == 0)
    def _():
        u_scr[SUBLANES - 2:SUBLANES, :] = hist_ref[0]

    _shift_cache(cv_ref[0], _new_rows_t(vn_ref[0]), vn_ref.shape[1], nv_ref)

    x = x_ref[0].astype(jnp.bfloat16)

    def proj(c0, width):
        return jnp.dot(x, w_bf[:, c0:c0 + width], preferred_element_type=jnp.float32)

    c0 = 4 * d_att
    gc = proj(c0 + d_conv, d_conv)
    hh = proj(c0 + 2 * d_conv, d_conv)
    u = gc * hh
    u_scr[SUBLANES:SUBLANES + tm, :] = u
    um2 = u_scr[SUBLANES - 2:SUBLANES - 2 + tm, :]
    um1 = u_scr[SUBLANES - 1:SUBLANES - 1 + tm, :]
    y_conv = cw_ref[0:1, :] * um2 + cw_ref[1:2, :] * um1 + cw_ref[2:3, :] * u
    tail = u[tm - 2:tm, :]
    u_scr[SUBLANES - 2:SUBLANES, :] = tail
    nc_ref[0] = tail
    gb = proj(c0, d_conv)
    g_conv = proj(c0 + 3 * d_conv, d_conv)
    z_ref[0] = (gb * y_conv * _silu(g_conv)).astype(jnp.bfloat16)

    sg = _silu(proj(3 * d_att, d_att)).astype(jnp.bfloat16)
    for p in range(n_pairs):
        sg_ref[0, p] = sg[:, p * LANES:(p + 1) * LANES]
    k = proj(d_att, d_att)
    kt_ref[0] = k.T
    _emit_streams(k, st_refs, 1, dils, (s_a, s_b), tm)
    v = proj(2 * d_att, d_att)
    vt_ref[0] = v.T
    _emit_streams(v, st_refs, 2, dils, (s_a, s_b), tm)
    q = proj(0, d_att) * q_scale
    _emit_streams(q, st_refs, 0, dils, (s_a, s_b), tm)


def _inproj(x, w_in_bf, conv_w, hist, cache_vt, vn_s, *, tm, d_att, d_conv, q_scale):
    b, t, d = x.shape
    n_pairs = d_att // LANES
    n_out = w_in_bf.shape[1]
    dils = _dilations()
    assert tm % (dils[-1] * 2 * SUBLANES) == 0
    kern = functools.partial(_inproj_kernel, tm=tm, d_att=d_att, d_conv=d_conv,
                             n_pairs=n_pairs, q_scale=q_scale, dils=dils)
    st_shapes = [jax.ShapeDtypeStruct((b, n_pairs, 3, dd, t // dd, LANES), jnp.bfloat16) for dd in dils]
    st_specs = [pl.BlockSpec((1, n_pairs, 3, dd, tm // dd, LANES), lambda i, j: (i, 0, 0, 0, j, 0))
                for dd in dils]
    pm = jax.ShapeDtypeStruct((b, n_pairs, t, LANES), jnp.bfloat16)
    pm_spec = pl.BlockSpec((1, n_pairs, tm, LANES), lambda i, j: (i, 0, j, 0))
    row_spec = lambda c: pl.BlockSpec((1, tm, c), lambda i, j: (i, j, 0))
    col_spec = pl.BlockSpec((1, d_att, tm), lambda i, j: (i, 0, j))
    nb, _, n_cache = cache_vt.shape
    ts = vn_s.shape[1]
    n_t = t // tm
    assert nb == b * n_t
    cache_spec = pl.BlockSpec((1, d_att, n_cache), lambda i, j: (i * n_t + j, 0, 0))
    vmem = (d * n_out * 2 + 2 * tm * d * 4
            + 2 * tm * ((3 * len(dils) + 1) * d_att * 2 + 2 * d_att * 4 + d_conv * 2)
            + (tm + SUBLANES) * d_conv * 4 + 2 * tm * LANES * 4 + 12 * tm * d_att * 4
            + 5 * d_att * n_cache * 4)
    return pl.pallas_call(
        kern,
        grid=(b, n_t),
        in_specs=[
            row_spec(d),
            pl.BlockSpec((d, n_out), lambda i, j: (0, 0), pipeline_mode=pl.Buffered(1)),
            pl.BlockSpec(conv_w.shape, lambda i, j: (0, 0)),
            pl.BlockSpec((1, 2, d_conv), lambda i, j: (i, 0, 0)),
            cache_spec,
            pl.BlockSpec((1, ts, d_att), lambda i, j: (i * n_t + j, 0, 0)),
        ],
        out_specs=st_specs + [col_spec, col_spec, pm_spec,
                              row_spec(d_conv), pl.BlockSpec((1, 2, d_conv), lambda i, j: (i, 0, 0)),
                              cache_spec],
        out_shape=st_shapes + [
            jax.ShapeDtypeStruct((b, d_att, t), jnp.float32),
            jax.ShapeDtypeStruct((b, d_att, t), jnp.float32),
            pm,
            jax.ShapeDtypeStruct((b, t, d_conv), jnp.bfloat16),
            jax.ShapeDtypeStruct((b, 2, d_conv), jnp.float32),
            jax.ShapeDtypeStruct((nb, d_att, n_cache), jnp.float32)],
        scratch_shapes=[pltpu.VMEM((tm + SUBLANES, d_conv), jnp.float32),
                        pltpu.VMEM((tm, LANES), jnp.float32),
                        pltpu.VMEM((tm, LANES), jnp.float32)],
        compiler_params=pltpu.CompilerParams(
            dimension_semantics=("parallel", "arbitrary"), vmem_limit_bytes=int(vmem)),
        name="prompt_inproj",
    )(x, w_in_bf, conv_w, hist, cache_vt, vn_s)


def _block_scores(q, kwin, lo):
    zero = jnp.zeros_like(q)
    q2 = jnp.concatenate([jnp.where(lo, q, zero), jnp.where(lo, zero, q)], axis=0)
    return lax.dot_general(q2, kwin, (((1,), (1,)), ((), ())), preferred_element_type=jnp.float32)


def _block_probs(s, bias, row_exact):
    ps, ms = [], []
    for h in range(2):
        sh = s[h * Q_BLOCK:(h + 1) * Q_BLOCK] + bias
        if row_exact:
            m = jnp.max(sh, axis=-1, keepdims=True)
        else:
            m = jnp.max(jnp.max(sh, axis=0, keepdims=True), axis=1, keepdims=True)
        ps.append(jnp.exp2(sh - m).astype(jnp.bfloat16))
        ms.append(jnp.broadcast_to(m, (Q_BLOCK, 1)))
    return ps, ms


def _block_values(ps, ms, vwin, lo):
    lo_v = lax.broadcasted_iota(jnp.int32, vwin.shape, 1) < LANES // 2
    one = jnp.ones_like(vwin)
    pv0 = jnp.dot(ps[0], jnp.where(lo_v, vwin, one), preferred_element_type=jnp.float32)
    pv1 = jnp.dot(ps[1], jnp.where(lo_v, one, vwin), preferred_element_type=jnp.float32)
    acc = jnp.where(lo, pv0, pv1)
    l = pltpu.roll(jnp.where(lo, pv1, pv0), LANES // 2, axis=1)
    return acc, jnp.where(lo, ms[0], ms[1]), l


def _branch_multiplicity(n_q, n_cache):
    i = np.arange(n_q)[:, None]
    pos = np.arange(n_cache + n_q)[None, :]
    dist = n_cache + i - pos
    mult = np.zeros(dist.shape, np.float32)
    for window, dil in DILATED_PAIRS:
        mult += (dist >= 0) & (dist <= window) & (dist % dil == 0)
    return mult


def _sample_unit(ck_ref, cv_ref, q_ref, kn_ref, vn_ref, sg_ref, mc_ref, mn_ref,
                 nk_ref, att_ref, *, n_heads, head_dim, ts, n_cache, n_pairs):
    d_att = n_heads * head_dim
    rows = n_heads * ts
    head_of_lane = lax.broadcasted_iota(jnp.int32, (rows, d_att), 1) // head_dim
    head_of_row = lax.broadcasted_iota(jnp.int32, (rows, d_att), 0) // ts
    own = head_of_lane == head_of_row

    q = q_ref[0]
    qx = jnp.where(own, jnp.concatenate([q] * n_heads, axis=0), 0.0).astype(jnp.bfloat16)
    kct = ck_ref[0]
    vct = cv_ref[0]
    knt = _new_rows_t(kn_ref[0])
    vnt = _new_rows_t(vn_ref[0])
    nt = (((1,), (1,)), ((), ()))
    s_c = jnp.dot(qx, kct.astype(jnp.bfloat16), preferred_element_type=jnp.float32)
    s_n = jnp.dot(qx, knt.astype(jnp.bfloat16), preferred_element_type=jnp.float32)
    mc = mc_ref[...]
    mn = mn_ref[...]
    s_c = jnp.where(mc > 0, s_c, NEG_BIG)
    s_n = jnp.where(mn > 0, s_n, NEG_BIG)
    m = jnp.maximum(jnp.max(s_c, axis=-1, keepdims=True), jnp.max(s_n, axis=-1, keepdims=True))
    e_c = jnp.exp2(s_c - m) * mc
    e_n = jnp.exp2(s_n - m) * mn
    l = jnp.sum(e_c, axis=-1, keepdims=True) + jnp.sum(e_n, axis=-1, keepdims=True)
    ox = (lax.dot_general(e_c.astype(jnp.bfloat16), vct.astype(jnp.bfloat16), nt,
                          preferred_element_type=jnp.float32)
          + lax.dot_general(e_n.astype(jnp.bfloat16), vnt.astype(jnp.bfloat16), nt,
                            preferred_element_type=jnp.float32))
    ox = jnp.where(own, ox * (1.0 / l), 0.0)
    o = ox[0:ts]
    for h in range(1, n_heads):
        o = o + ox[h * ts:(h + 1) * ts]
    att = o * sg_ref[0]
    for p in range(n_pairs):
        att_ref[0, p] = att[:, p * LANES:(p + 1) * LANES]

    _shift_cache(kct, knt, ts, nk_ref)


def _split_streams(dils, seq, n_parts):
    items = sorted(((seq // d // Q_BLOCK, i, r) for i, d in enumerate(dils) for r in range(d) if i > 0),
                   reverse=True)
    parts, loads = [[] for _ in range(n_parts)], [0] * n_parts
    loads[-1] = seq // dils[0] // Q_BLOCK
    for cost, i, r in items:
        j = loads.index(min(loads))
        parts[j].append((i, r))
        loads[j] += cost
    parts[-1].append((0, 0))
    return parts


def _attention_kernel(*refs, seq, dils, head_dim, parts, sample_cfg):
    n_br = len(dils)
    qkv = refs[:n_br]
    sg_ref = refs[n_br]
    sample_in = refs[n_br + 1:n_br + 9]
    att_ref = refs[n_br + 9]
    sample_out = refs[n_br + 10:n_br + 12]
    scr = refs[n_br + 12:]
    a_scr, m_scr, l_scr = scr[0::3], scr[1::3], scr[2::3]
    part = pl.program_id(2)

    lane = lax.broadcasted_iota(jnp.int32, (Q_BLOCK, LANES), 1)
    lo = lane < head_dim
    row = lax.broadcasted_iota(jnp.int32, (Q_BLOCK, 2 * Q_BLOCK), 0)
    col = lax.broadcasted_iota(jnp.int32, (Q_BLOCK, 2 * Q_BLOCK), 1)
    band = jnp.where((col >= row) & (col <= row + Q_BLOCK), 0.0, NEG_BIG).astype(jnp.float32)
    row1 = lax.broadcasted_iota(jnp.int32, (Q_BLOCK, Q_BLOCK), 0)
    col1 = lax.broadcasted_iota(jnp.int32, (Q_BLOCK, Q_BLOCK), 1)
    causal = jnp.where(col1 <= row1, 0.0, NEG_BIG).astype(jnp.float32)

    def merge(row0):
        rows = slice(row0, row0 + Q_BLOCK)
        ms = [m_ref[rows, :] for m_ref in m_scr]
        top = functools.reduce(jnp.maximum, ms)
        ws = [jnp.exp2(m - top) for m in ms]
        num = functools.reduce(lambda a, b: a + b, [w * a_ref[rows, :] for w, a_ref in zip(ws, a_scr)])
        den = functools.reduce(lambda a, b: a + b, [w * l_ref[rows, :] for w, l_ref in zip(ws, l_scr)])
        att = num * (1.0 / den) * sg_ref[rows, :].astype(jnp.float32)
        att_ref[rows, :] = att.astype(att_ref.dtype)

    def run_blocks(items, row_exact):
        blocks = [(i, r, qb) for i, r in items for qb in range(seq // dils[i] // Q_BLOCK)]
        l_min = None
        for g in range(0, len(blocks), BLOCK_GROUP):
            group = blocks[g:g + BLOCK_GROUP]
            keys = [slice(max(qb - 1, 0) * Q_BLOCK, (qb + 1) * Q_BLOCK) for _, _, qb in group]
            scores = [_block_scores(qkv[i][0, r, qb * Q_BLOCK:(qb + 1) * Q_BLOCK, :], qkv[i][1, r, ks, :], lo)
                      for (i, r, qb), ks in zip(group, keys)]
            probs = [_block_probs(s, causal if qb == 0 else band, row_exact)
                     for s, (_, _, qb) in zip(scores, group)]
            for (i, r, qb), ks, (ps, ms) in zip(group, keys, probs):
                vals = _block_values(ps, ms, qkv[i][2, r, ks, :], lo)
                d, row0 = dils[i], qb * Q_BLOCK
                idx = pl.ds(row0, Q_BLOCK) if d == 1 else pl.ds(r + d * row0, Q_BLOCK, stride=d)
                for ref, val in zip((a_scr[i], m_scr[i], l_scr[i]), vals):
                    ref[idx, :] = val
                l_min = vals[2] if l_min is None else jnp.minimum(l_min, vals[2])
            for i, _, qb in group:
                if i == 0:
                    merge(qb * Q_BLOCK)
        return l_min

    for idx, items in enumerate(parts):
        @pl.when(part == idx)
        def _(items=items):
            _sample_unit(*sample_in, *sample_out, **sample_cfg)
            l_min = run_blocks(items, False)

            @pl.when(jnp.min(l_min) < L_FLOOR)
            def _():
                run_blocks(items, True)


def _attention(qkv_st, sg_pm, cache_kt, cache_vt, q_s, kn_s, vn_s, sg_s, *, n_heads, head_dim):
    b, n_pairs, t, _ = sg_pm.shape
    nb, d_att, n_cache = cache_kt.shape
    ts = q_s.shape[1]
    dils = _dilations()
    for w, d in DILATED_PAIRS:
        assert w == d * Q_BLOCK and t % (d * Q_BLOCK) == 0
    n_parts = ATTN_PARTS
    steps = b * n_pairs * n_parts
    assert steps % nb == 0 and n_heads % (steps // nb) == 0
    groups = steps // nb
    heads_u = n_heads // groups
    d_u = heads_u * head_dim
    assert d_u % LANES == 0
    pairs_u = d_u // LANES
    rows_u = heads_u * ts

    def unit(i, j, k):
        u = (i * n_pairs + j) * n_parts + k
        return u // groups, u % groups

    ins, specs = [], []
    for a, d in zip(qkv_st, dils):
        assert a.shape == (b, n_pairs, 3, d, t // d, LANES)
        ins.append(a)
        specs.append(pl.BlockSpec((None, None, 3, d, t // d, LANES), lambda i, j, k: (i, j, 0, 0, 0, 0)))
    pair_spec = pl.BlockSpec((None, None, t, LANES), lambda i, j, k: (i, j, 0, 0))
    cache_spec = pl.BlockSpec((1, d_u, n_cache), lambda i, j, k: (unit(i, j, k)[0], unit(i, j, k)[1], 0))
    row_spec = pl.BlockSpec((1, ts, d_u), lambda i, j, k: (unit(i, j, k)[0], 0, unit(i, j, k)[1]))
    mult = np.tile(_branch_multiplicity(ts, n_cache), (heads_u, 1))
    mult_c = jnp.asarray(mult[:, :n_cache])
    mult_n = jnp.asarray(np.pad(mult[:, n_cache:], ((0, 0), (LANES - ts, 0))))
    const = lambda shape: pl.BlockSpec(shape, lambda i, j, k: (0, 0))
    sample_cfg = dict(n_heads=heads_u, head_dim=head_dim, ts=ts, n_cache=n_cache, n_pairs=pairs_u)
    kern = functools.partial(_attention_kernel, seq=t, dils=dils, head_dim=head_dim,
                             parts=_split_streams(dils, t, n_parts), sample_cfg=sample_cfg)
    vmem = (2 * (3 * len(dils) + 2) * t * LANES * 2 + 3 * len(dils) * t * LANES * 4
            + 14 * d_u * n_cache * 4 + 8 * rows_u * n_cache * 4 + 12 * MIB)
    return pl.pallas_call(
        kern,
        grid=(b, n_pairs, n_parts),
        in_specs=specs + [pair_spec, cache_spec, cache_spec, row_spec, row_spec, row_spec, row_spec,
                          const((rows_u, n_cache)), const((rows_u, LANES))],
        out_specs=[pair_spec, cache_spec,
                   pl.BlockSpec((1, pairs_u, ts, LANES),
                                lambda i, j, k: (0, unit(i, j, k)[1], unit(i, j, k)[0], 0))],
        out_shape=[jax.ShapeDtypeStruct((b, n_pairs, t, LANES), jnp.bfloat16),
                   jax.ShapeDtypeStruct((nb, d_att, n_cache), jnp.float32),
                   jax.ShapeDtypeStruct((1, d_att // LANES, nb * ts, LANES), jnp.float32)],
        scratch_shapes=[pltpu.VMEM((t, LANES), jnp.float32) for _ in range(3 * len(dils))],
        compiler_params=pltpu.CompilerParams(
            dimension_semantics=("parallel", "parallel", "arbitrary"), vmem_limit_bytes=int(vmem)),
        name="attention",
    )(*ins, sg_pm, cache_kt, cache_vt, q_s, kn_s, vn_s, sg_s, mult_c, mult_n)


def _outproj_kernel(att_ref, z_ref, x_ref, w_ref, g_ref, b_ref, y_ref, w_bf, h_scr, *, n_pairs, n_tiles):
    j = pl.program_id(1)

    @pl.when((pl.program_id(0) == 0) & (j == 0))
    def _():
        _cast_rows(w_ref, w_bf)

    def project(slot):
        parts = [att_ref[0, p].astype(jnp.bfloat16) for p in range(n_pairs)]
        mix = jnp.concatenate(parts + [z_ref[0].astype(jnp.bfloat16)], axis=-1)
        out = jnp.dot(mix, w_bf[...], preferred_element_type=jnp.float32)
        h_scr[slot] = DEEPNORM_ALPHA * x_ref[0] + out

    def normalise(slot):
        h = h_scr[slot]
        mu = jnp.mean(h, axis=-1, keepdims=True)
        d = h - mu
        var = jnp.mean(d * d, axis=-1, keepdims=True)
        y_ref[0] = d * lax.rsqrt(var + LN_EPS) * g_ref[...] + b_ref[...]

    @pl.when(j == 0)
    def _():
        project(0)

    for parity in range(2):
        @pl.when((j > 0) & (j < n_tiles) & (j % 2 == parity))
        def _(parity=parity):
            project(parity)
            normalise(1 - parity)

    @pl.when(j == n_tiles)
    def _():
        normalise((n_tiles - 1) % 2)


def _outproj(att_pm, z, x, w_out, ln_g, ln_b, *, tm):
    b, t, d = x.shape
    n_pairs = att_pm.shape[1]
    d_conv = z.shape[-1]
    d_mix = w_out.shape[0]
    n_tiles = t // tm
    vmem = (d_mix * d * (4 + 2) + 4 * tm * d * 4 + 2 * tm * n_pairs * LANES * att_pm.dtype.itemsize
            + 2 * tm * d_conv * 2 + 2 * tm * d * 4 + 6 * tm * d * 4)
    last = n_tiles - 1
    return pl.pallas_call(
        functools.partial(_outproj_kernel, n_pairs=n_pairs, n_tiles=n_tiles),
        grid=(b, n_tiles + 1),
        in_specs=[
            pl.BlockSpec((1, n_pairs, tm, LANES), lambda i, j: (i, 0, jnp.minimum(j, last), 0)),
            pl.BlockSpec((1, tm, d_conv), lambda i, j: (i, jnp.minimum(j, last), 0)),
            pl.BlockSpec((1, tm, d), lambda i, j: (i, jnp.minimum(j, last), 0)),
            pl.BlockSpec((d_mix, d), lambda i, j: (0, 0), pipeline_mode=pl.Buffered(1)),
            pl.BlockSpec((1, d), lambda i, j: (0, 0)),
            pl.BlockSpec((1, d), lambda i, j: (0, 0)),
        ],
        out_specs=pl.BlockSpec((1, tm, d), lambda i, j: (i, jnp.maximum(j - 1, 0), 0)),
        out_shape=jax.ShapeDtypeStruct((b, t, d), jnp.float32),
        scratch_shapes=[pltpu.VMEM((d_mix, d), jnp.bfloat16),
                        pltpu.VMEM((2, tm, d), jnp.float32)],
        compiler_params=pltpu.CompilerParams(
            dimension_semantics=("arbitrary", "arbitrary"), vmem_limit_bytes=int(vmem)),
        name="outproj_layernorm",
    )(att_pm, z, x, w_out, ln_g.reshape(1, d), ln_b.reshape(1, d))


def _sample_inproj_kernel(x_ref, w_ref, cw_ref, st_ref,
                          q_ref, k_ref, v_ref, sg_ref, z_ref, nc_ref, u_scr,
                          *, nb, ts, d_att, d_conv, q_scale):
    x = x_ref[...].astype(jnp.bfloat16)

    def proj(c0, width):
        return jnp.dot(x, w_ref[:, c0:c0 + width], preferred_element_type=jnp.float32)

    q_ref[...] = proj(0, d_att) * q_scale
    k_ref[...] = proj(d_att, d_att)
    v_ref[...] = proj(2 * d_att, d_att)
    sg_ref[...] = _silu(proj(3 * d_att, d_att))
    c0 = 4 * d_att
    gb = proj(c0, d_conv)
    gc = proj(c0 + d_conv, d_conv)
    hh = proj(c0 + 2 * d_conv, d_conv)
    g_conv = proj(c0 + 3 * d_conv, d_conv)
    u = (gc * hh).reshape(nb, ts, d_conv)
    u_scr[:, SUBLANES - 2:SUBLANES, :] = st_ref[...]
    u_scr[:, SUBLANES:SUBLANES + ts, :] = u
    um2 = u_scr[:, SUBLANES - 2:SUBLANES - 2 + ts, :]
    um1 = u_scr[:, SUBLANES - 1:SUBLANES - 1 + ts, :]
    cw = cw_ref[...]
    y_conv = cw[0:1, :][None] * um2 + cw[1:2, :][None] * um1 + cw[2:3, :][None] * u
    z = gb * y_conv.reshape(nb * ts, d_conv) * _silu(g_conv)
    z_ref[...] = z.astype(jnp.bfloat16)
    nc_ref[...] = u_scr[:, ts + SUBLANES - 2:ts + SUBLANES, :]


def _sample_inproj(x2, w_in_bf, conv_w, state, *, nb, ts, d_att, d_conv, q_scale):
    rows, d = x2.shape
    f32 = lambda c: jax.ShapeDtypeStruct((rows, c), jnp.float32)
    kern = functools.partial(_sample_inproj_kernel, nb=nb, ts=ts, d_att=d_att, d_conv=d_conv, q_scale=q_scale)
    vmem = d * w_in_bf.shape[1] * 2 + 16 * rows * d * 4 + 8 * MIB
    return pl.pallas_call(
        kern,
        out_shape=[f32(d_att), f32(d_att), f32(d_att), f32(d_att),
                   jax.ShapeDtypeStruct((rows, d_conv), jnp.bfloat16),
                   jax.ShapeDtypeStruct((nb, 2, d_conv), jnp.float32)],
        scratch_shapes=[pltpu.VMEM((nb, ts + SUBLANES, d_conv), jnp.float32)],
        compiler_params=pltpu.CompilerParams(vmem_limit_bytes=int(vmem)),
        name="sample_inproj",
    )(x2, w_in_bf, conv_w, state)


def kernel(x_prompt, x_sample, cache_k, cache_v, state_conv, w_in, conv_w, w_out, ln_g, ln_b):
    b, t, d = x_prompt.shape
    nb, ts, _ = x_sample.shape
    _, n_cache, n_heads, head_dim = cache_k.shape
    d_att = n_heads * head_dim
    d_conv = conv_w.shape[1]
    assert w_in.shape == (d, 4 * d_att + 4 * d_conv) and w_out.shape == (d_att + d_conv, d)
    assert d_att % LANES == 0 and LANES == 2 * head_dim and conv_w.shape[0] == 3
    max_window = max(w for w, _ in DILATED_PAIRS)
    assert n_cache == max_window and t <= max_window
    q_scale = head_dim ** -0.5 * math.log2(math.e)

    def to_time_minor(a):
        return jnp.transpose(a, (0, 2, 3, 1)).reshape(a.shape[0], d_att, a.shape[1])

    def from_time_minor(a):
        return jnp.transpose(a.reshape(a.shape[0], n_heads, head_dim, a.shape[2]), (0, 3, 1, 2))

    w_in_bf = w_in.astype(jnp.bfloat16)
    q_s, k_s, v_s, sg_s, z_s, new_conv_sample = _sample_inproj(
        x_sample.reshape(nb * ts, d), w_in_bf, conv_w, state_conv, nb=nb, ts=ts, d_att=d_att, d_conv=d_conv,
        q_scale=q_scale)
    three = lambda a: a.reshape(nb, ts, d_att)
    cache_kt, cache_vt = to_time_minor(cache_k), to_time_minor(cache_v)

    conv0 = jnp.zeros((b, 2, d_conv), jnp.float32)
    outs = _inproj(x_prompt, w_in_bf, conv_w, conv0, cache_vt, three(v_s),
                   tm=512, d_att=d_att, d_conv=d_conv, q_scale=q_scale)
    n_d = len(DILATED_PAIRS)
    qkv_st = outs[:n_d]
    kt_new, vt_new, sg_pm, z, new_conv_prompt, new_vt_s = outs[n_d:]

    att_pm, new_kt_s, att_s = _attention(
        qkv_st, sg_pm, cache_kt, cache_vt,
        three(q_s), three(k_s), three(v_s), three(sg_s), n_heads=n_heads, head_dim=head_dim)

    y_prompt = _outproj(att_pm, z, x_prompt, w_out, ln_g, ln_b, tm=512)
    y_sample = _outproj(att_s, z_s.reshape(1, nb * ts, d_conv), x_sample.reshape(1, nb * ts, d),
                        w_out, ln_g, ln_b, tm=nb * ts)

    return (y_prompt, y_sample.reshape(nb, ts, d),
            from_time_minor(kt_new), from_time_minor(vt_new), new_conv_prompt,
            from_time_minor(new_kt_s), from_time_minor(new_vt_s), new_conv_sample)
```

```python
import functools
import math

import numpy as np
import jax
import jax.numpy as jnp
from jax import lax
from jax.experimental import pallas as pl
from jax.experimental.pallas import tpu as pltpu

DILATED_PAIRS = ((128, 1), (512, 4), (2048, 16))
LN_EPS = 1e-5
DEPTH = 1
DEEPNORM_ALPHA = (2.0 * DEPTH) ** 0.25

LANES = 128
SUBLANES = 8
Q_BLOCK = 128
NEG_BIG = -1e30
ATTN_PARTS = 1
OUTPROJ_CHUNKS = 2
BLOCK_GROUP = 8
L_FLOOR = 2.0 ** -100

MIB = 1024 * 1024


def _dilations():
    dils = tuple(sorted(d for _, d in DILATED_PAIRS))
    assert dils[0] == 1 and all(b % a == 0 for a, b in zip(dils, dils[1:]))
    return dils


def _silu(x):
    return x * (1.0 / (1.0 + jnp.exp(-x)))


def _new_rows_t(rows):
    ts, d = rows.shape
    pad = jnp.zeros((LANES - ts, d), rows.dtype)
    return jnp.concatenate([pad, rows], axis=0).T


def _shift_cache(old, new_t, ts, dst_ref):
    d, n_cache = old.shape
    main = n_cache - LANES
    is_old = lax.broadcasted_iota(jnp.int32, (d, LANES), 1) < LANES - ts
    rolled = pltpu.roll(old, n_cache - ts, axis=1)
    dst_ref[0, :, 0:main] = rolled[:, 0:main]
    dst_ref[0, :, main:n_cache] = jnp.where(is_old, rolled[:, main:n_cache], new_t)


def _emit_streams(val, refs, which, dils, scr, tm):
    n_pairs = val.shape[1] // LANES
    for p in range(n_pairs):
        col = val[:, p * LANES:(p + 1) * LANES]
        refs[0][0, p, which, 0] = col.astype(jnp.bfloat16)
        if len(dils) > 1:
            scr[0][...] = col
        for lvl in range(1, len(dils)):
            d_prev = dils[lvl - 1]
            f = dils[lvl] // d_prev
            n_prev = tm // d_prev
            n = n_prev // f
            src, dst = scr[(lvl - 1) % 2], scr[lvl % 2]
            for s in range(d_prev):
                for r2 in range(f):
                    r = s + d_prev * r2
                    rows = src[pl.ds(s * n_prev + r2, n, stride=f), :]
                    refs[lvl][0, p, which, r] = rows.astype(jnp.bfloat16)
                    if lvl + 1 < len(dils):
                        dst[r * n:(r + 1) * n, :] = rows


def _inproj_kernel(*refs, tm, d_att, d_conv, n_pairs, q_scale, dils):
    x_ref, w_bf, cw_ref, hist_ref, cv_ref, vn_ref = refs[:6]
    n_d = len(dils)
    st_refs = refs[6:6 + n_d]
    kt_ref, vt_ref, sg_ref, z_ref, nc_ref, nv_ref, u_scr, s_a, s_b = refs[6 + n_d:]

    @pl.when(pl.program_id(1) == 0)
    def _():
        u_scr[SUBLANES - 2:SUBLANES, :] = hist_ref[0]

    _shift_cache(cv_ref[0], _new_rows_t(vn_ref[0]), vn_ref.shape[1], nv_ref)

    x = x_ref[0].astype(jnp.bfloat16)

    def proj(c0, width):
        return jnp.dot(x, w_bf[:, c0:c0 + width], preferred_element_type=jnp.float32)

    c0 = 4 * d_att
    gc = proj(c0 + d_conv, d_conv)
    hh = proj(c0 + 2 * d_conv, d_conv)
    u = gc * hh
    u_scr[SUBLANES:SUBLANES + tm, :] = u
    um2 = u_scr[SUBLANES - 2:SUBLANES - 2 + tm, :]
    um1 = u_scr[SUBLANES - 1:SUBLANES - 1 + tm, :]
    y_conv = cw_ref[0:1, :] * um2 + cw_ref[1:2, :] * um1 + cw_ref[2:3, :] * u
    tail = u[tm - 2:tm, :]
    u_scr[SUBLANES - 2:SUBLANES, :] = tail
    nc_ref[0] = tail
    gb = proj(c0, d_conv)
    g_conv = proj(c0 + 3 * d_conv, d_conv)
    z_ref[0] = (gb * y_conv * _silu(g_conv)).astype(jnp.bfloat16)

    sg = _silu(proj(3 * d_att, d_att)).astype(jnp.bfloat16)
    for p in range(n_pairs):
        sg_ref[0, p] = sg[:, p * LANES:(p + 1) * LANES]
    k = proj(d_att, d_att)
    kt_ref[0] = k.T
    _emit_streams(k, st_refs, 1, dils, (s_a, s_b), tm)
    v = proj(2 * d_att, d_att)
    vt_ref[0] = v.T
    _emit_streams(v, st_refs, 2, dils, (s_a, s_b), tm)
    q = proj(0, d_att) * q_scale
    _emit_streams(q, st_refs, 0, dils, (s_a, s_b), tm)


def _inproj(x, w_in_bf, conv_w, hist, cache_vt, vn_s, *, tm, d_att, d_conv, q_scale):
    b, t, d = x.shape
    n_pairs = d_att // LANES
    n_out = w_in_bf.shape[1]
    dils = _dilations()
    assert tm % (dils[-1] * 2 * SUBLANES) == 0
    kern = functools.partial(_inproj_kernel, tm=tm, d_att=d_att, d_conv=d_conv,
                             n_pairs=n_pairs, q_scale=q_scale, dils=dils)
    st_shapes = [jax.ShapeDtypeStruct((b, n_pairs, 3, dd, t // dd, LANES), jnp.bfloat16) for dd in dils]
    st_specs = [pl.BlockSpec((1, n_pairs, 3, dd, tm // dd, LANES), lambda i, j: (i, 0, 0, 0, j, 0))
                for dd in dils]
    pm = jax.ShapeDtypeStruct((b, n_pairs, t, LANES), jnp.bfloat16)
    pm_spec = pl.BlockSpec((1, n_pairs, tm, LANES), lambda i, j: (i, 0, j, 0))
    row_spec = lambda c: pl.BlockSpec((1, tm, c), lambda i, j: (i, j, 0))
    col_spec = pl.BlockSpec((1, d_att, tm), lambda i, j: (i, 0, j))
    nb, _, n_cache = cache_vt.shape
    ts = vn_s.shape[1]
    n_t = t // tm
    assert nb == b * n_t
    cache_spec = pl.BlockSpec((1, d_att, n_cache), lambda i, j: (i * n_t + j, 0, 0))
    vmem = (d * n_out * 2 + 2 * tm * d * 4
            + 2 * tm * ((3 * len(dils) + 1) * d_att * 2 + 2 * d_att * 4 + d_conv * 2)
            + (tm + SUBLANES) * d_conv * 4 + 2 * tm * LANES * 4 + 12 * tm * d_att * 4
            + 5 * d_att * n_cache * 4)
    return pl.pallas_call(
        kern,
        grid=(b, n_t),
        in_specs=[
            row_spec(d),
            pl.BlockSpec((d, n_out), lambda i, j: (0, 0), pipeline_mode=pl.Buffered(1)),
            pl.BlockSpec(conv_w.shape, lambda i, j: (0, 0)),
            pl.BlockSpec((1, 2, d_conv), lambda i, j: (i, 0, 0)),
            cache_spec,
            pl.BlockSpec((1, ts, d_att), lambda i, j: (i * n_t + j, 0, 0)),
        ],
        out_specs=st_specs + [col_spec, col_spec, pm_spec,
                              row_spec(d_conv), pl.BlockSpec((1, 2, d_conv), lambda i, j: (i, 0, 0)),
                              cache_spec],
        out_shape=st_shapes + [
            jax.ShapeDtypeStruct((b, d_att, t), jnp.float32),
            jax.ShapeDtypeStruct((b, d_att, t), jnp.float32),
            pm,
            jax.ShapeDtypeStruct((b, t, d_conv), jnp.bfloat16),
            jax.ShapeDtypeStruct((b, 2, d_conv), jnp.float32),
            jax.ShapeDtypeStruct((nb, d_att, n_cache), jnp.float32)],
        scratch_shapes=[pltpu.VMEM((tm + SUBLANES, d_conv), jnp.float32),
                        pltpu.VMEM((tm, LANES), jnp.float32),
                        pltpu.VMEM((tm, LANES), jnp.float32)],
        compiler_params=pltpu.CompilerParams(
            dimension_semantics=("parallel", "arbitrary"), vmem_limit_bytes=int(vmem)),
        name="prompt_inproj",
    )(x, w_in_bf, conv_w, hist, cache_vt, vn_s)


def _block_scores(q, kwin, lo):
    zero = jnp.zeros_like(q)
    q2 = jnp.concatenate([jnp.where(lo, q, zero), jnp.where(lo, zero, q)], axis=0)
    return lax.dot_general(q2, kwin, (((1,), (1,)), ((), ())), preferred_element_type=jnp.float32)


def _block_probs(s, bias, row_exact):
    ps, ms = [], []
    for h in range(2):
        sh = s[h * Q_BLOCK:(h + 1) * Q_BLOCK] + bias
        if row_exact:
            m = jnp.max(sh, axis=-1, keepdims=True)
        else:
            m = jnp.max(jnp.max(sh, axis=0, keepdims=True), axis=1, keepdims=True)
        ps.append(jnp.exp2(sh - m).astype(jnp.bfloat16))
        ms.append(jnp.broadcast_to(m, (Q_BLOCK, 1)))
    return ps, ms


def _block_values(ps, ms, vwin, lo):
    lo_v = lax.broadcasted_iota(jnp.int32, vwin.shape, 1) < LANES // 2
    one = jnp.ones_like(vwin)
    pv0 = jnp.dot(ps[0], jnp.where(lo_v, vwin, one), preferred_element_type=jnp.float32)
    pv1 = jnp.dot(ps[1], jnp.where(lo_v, one, vwin), preferred_element_type=jnp.float32)
    acc = jnp.where(lo, pv0, pv1)
    l = pltpu.roll(jnp.where(lo, pv1, pv0), LANES // 2, axis=1)
    return acc, jnp.where(lo, ms[0], ms[1]), l


def _branch_multiplicity(n_q, n_cache):
    i = np.arange(n_q)[:, None]
    pos = np.arange(n_cache + n_q)[None, :]
    dist = n_cache + i - pos
    mult = np.zeros(dist.shape, np.float32)
    for window, dil in DILATED_PAIRS:
        mult += (dist >= 0) & (dist <= window) & (dist % dil == 0)
    return mult


def _sample_unit(ck_ref, cv_ref, q_ref, kn_ref, vn_ref, sg_ref, mc_ref, mn_ref,
                 nk_ref, att_ref, *, n_heads, head_dim, ts, n_cache, n_pairs):
    d_att = n_heads * head_dim
    rows = n_heads * ts
    head_of_lane = lax.broadcasted_iota(jnp.int32, (rows, d_att), 1) // head_dim
    head_of_row = lax.broadcasted_iota(jnp.int32, (rows, d_att), 0) // ts
    own = head_of_lane == head_of_row

    q = q_ref[0]
    qx = jnp.where(own, jnp.concatenate([q] * n_heads, axis=0), 0.0).astype(jnp.bfloat16)
    kct = ck_ref[0]
    vct = cv_ref[0]
    knt = _new_rows_t(kn_ref[0])
    vnt = _new_rows_t(vn_ref[0])
    nt = (((1,), (1,)), ((), ()))
    s_c = jnp.dot(qx, kct.astype(jnp.bfloat16), preferred_element_type=jnp.float32)
    s_n = jnp.dot(qx, knt.astype(jnp.bfloat16), preferred_element_type=jnp.float32)
    mc = mc_ref[...]
    mn = mn_ref[...]
    s_c = jnp.where(mc > 0, s_c, NEG_BIG)
    s_n = jnp.where(mn > 0, s_n, NEG_BIG)
    m = jnp.maximum(jnp.max(s_c, axis=-1, keepdims=True), jnp.max(s_n, axis=-1, keepdims=True))
    e_c = jnp.exp2(s_c - m) * mc
    e_n = jnp.exp2(s_n - m) * mn
    l = jnp.sum(e_c, axis=-1, keepdims=True) + jnp.sum(e_n, axis=-1, keepdims=True)
    ox = (lax.dot_general(e_c.astype(jnp.bfloat16), vct.astype(jnp.bfloat16), nt,
                          preferred_element_type=jnp.float32)
          + lax.dot_general(e_n.astype(jnp.bfloat16), vnt.astype(jnp.bfloat16), nt,
                            preferred_element_type=jnp.float32))
    ox = jnp.where(own, ox * (1.0 / l), 0.0)
    o = ox[0:ts]
    for h in range(1, n_heads):
        o = o + ox[h * ts:(h + 1) * ts]
    att = o * sg_ref[0]
    for p in range(n_pairs):
        att_ref[0, p] = att[:, p * LANES:(p + 1) * LANES]

    _shift_cache(kct, knt, ts, nk_ref)


def _split_streams(dils, seq, n_parts):
    items = sorted(((seq // d // Q_BLOCK, i, r) for i, d in enumerate(dils) for r in range(d) if i > 0),
                   reverse=True)
    parts, loads = [[] for _ in range(n_parts)], [0] * n_parts
    loads[-1] = seq // dils[0] // Q_BLOCK
    for cost, i, r in items:
        j = loads.index(min(loads))
        parts[j].append((i, r))
        loads[j] += cost
    parts[-1].append((0, 0))
    return parts


def _attention_kernel(*refs, seq, dils, head_dim, parts, sample_cfg):
    n_br = len(dils)
    qkv = refs[:n_br]
    sg_ref = refs[n_br]
    sample_in = refs[n_br + 1:n_br + 9]
    att_ref = refs[n_br + 9]
    sample_out = refs[n_br + 10:n_br + 12]
    scr = refs[n_br + 12:]
    a_scr, m_scr, l_scr = scr[0::3], scr[1::3], scr[2::3]
    part = pl.program_id(2)

    lane = lax.broadcasted_iota(jnp.int32, (Q_BLOCK, LANES), 1)
    lo = lane < head_dim
    row = lax.broadcasted_iota(jnp.int32, (Q_BLOCK, 2 * Q_BLOCK), 0)
    col = lax.broadcasted_iota(jnp.int32, (Q_BLOCK, 2 * Q_BLOCK), 1)
    band = jnp.where((col >= row) & (col <= row + Q_BLOCK), 0.0, NEG_BIG).astype(jnp.float32)
    row1 = lax.broadcasted_iota(jnp.int32, (Q_BLOCK, Q_BLOCK), 0)
    col1 = lax.broadcasted_iota(jnp.int32, (Q_BLOCK, Q_BLOCK), 1)
    causal = jnp.where(col1 <= row1, 0.0, NEG_BIG).astype(jnp.float32)

    def merge(row0):
        rows = slice(row0, row0 + Q_BLOCK)
        ms = [m_ref[rows, :] for m_ref in m_scr]
        top = functools.reduce(jnp.maximum, ms)
        ws = [jnp.exp2(m - top) for m in ms]
        num = functools.reduce(lambda a, b: a + b, [w * a_ref[rows, :] for w, a_ref in zip(ws, a_scr)])
        den = functools.reduce(lambda a, b: a + b, [w * l_ref[rows, :] for w, l_ref in zip(ws, l_scr)])
        att = num * (1.0 / den) * sg_ref[rows, :].astype(jnp.float32)
        att_ref[rows, :] = att.astype(att_ref.dtype)

    def run_blocks(items, row_exact):
        blocks = [(i, r, qb) for i, r in items for qb in range(seq // dils[i] // Q_BLOCK)]
        l_min = None
        for g in range(0, len(blocks), BLOCK_GROUP):
            group = blocks[g:g + BLOCK_GROUP]
            keys = [slice(max(qb - 1, 0) * Q_BLOCK, (qb + 1) * Q_BLOCK) for _, _, qb in group]
            scores = [_block_scores(qkv[i][0, r, qb * Q_BLOCK:(qb + 1) * Q_BLOCK, :], qkv[i][1, r, ks, :], lo)
                      for (i, r, qb), ks in zip(group, keys)]
            probs = [_block_probs(s, causal if qb == 0 else band, row_exact)
                     for s, (_, _, qb) in zip(scores, group)]
            for (i, r, qb), ks, (ps, ms) in zip(group, keys, probs):
                vals = _block_values(ps, ms, qkv[i][2, r, ks, :], lo)
                d, row0 = dils[i], qb * Q_BLOCK
                idx = pl.ds(row0, Q_BLOCK) if d == 1 else pl.ds(r + d * row0, Q_BLOCK, stride=d)
                for ref, val in zip((a_scr[i], m_scr[i], l_scr[i]), vals):
                    ref[idx, :] = val
                l_min = vals[2] if l_min is None else jnp.minimum(l_min, vals[2])
            for i, _, qb in group:
                if i == 0:
                    merge(qb * Q_BLOCK)
        return l_min

    for idx, items in enumerate(parts):
        @pl.when(part == idx)
        def _(items=items):
            _sample_unit(*sample_in, *sample_out, **sample_cfg)
            l_min = run_blocks(items, False)

            @pl.when(jnp.min(l_min) < L_FLOOR)
            def _():
                run_blocks(items, True)


def _attention(qkv_st, sg_pm, cache_kt, cache_vt, q_s, kn_s, vn_s, sg_s, *, n_heads, head_dim):
    b, n_pairs, t, _ = sg_pm.shape
    nb, d_att, n_cache = cache_kt.shape
    ts = q_s.shape[1]
    dils = _dilations()
    for w, d in DILATED_PAIRS:
        assert w == d * Q_BLOCK and t % (d * Q_BLOCK) == 0
    n_parts = ATTN_PARTS
    steps = b * n_pairs * n_parts
    assert steps % nb == 0 and n_heads % (steps // nb) == 0
    groups = steps // nb
    heads_u = n_heads // groups
    d_u = heads_u * head_dim
    assert d_u % LANES == 0
    pairs_u = d_u // LANES
    rows_u = heads_u * ts

    def unit(i, j, k):
        u = (i * n_pairs + j) * n_parts + k
        return u // groups, u % groups

    ins, specs = [], []
    for a, d in zip(qkv_st, dils):
        assert a.shape == (b, n_pairs, 3, d, t // d, LANES)
        ins.append(a)
        specs.append(pl.BlockSpec((None, None, 3, d, t // d, LANES), lambda i, j, k: (i, j, 0, 0, 0, 0)))
    pair_spec = pl.BlockSpec((None, None, t, LANES), lambda i, j, k: (i, j, 0, 0))
    cache_spec = pl.BlockSpec((1, d_u, n_cache), lambda i, j, k: (unit(i, j, k)[0], unit(i, j, k)[1], 0))
    row_spec = pl.BlockSpec((1, ts, d_u), lambda i, j, k: (unit(i, j, k)[0], 0, unit(i, j, k)[1]))
    mult = np.tile(_branch_multiplicity(ts, n_cache), (heads_u, 1))
    mult_c = jnp.asarray(mult[:, :n_cache])
    mult_n = jnp.asarray(np.pad(mult[:, n_cache:], ((0, 0), (LANES - ts, 0))))
    const = lambda shape: pl.BlockSpec(shape, lambda i, j, k: (0, 0))
    sample_cfg = dict(n_heads=heads_u, head_dim=head_dim, ts=ts, n_cache=n_cache, n_pairs=pairs_u)
    kern = functools.partial(_attention_kernel, seq=t, dils=dils, head_dim=head_dim,
                             parts=_split_streams(dils, t, n_parts), sample_cfg=sample_cfg)
    vmem = (2 * (3 * len(dils) + 2) * t * LANES * 2 + 3 * len(dils) * t * LANES * 4
            + 14 * d_u * n_cache * 4 + 8 * rows_u * n_cache * 4 + 12 * MIB)
    return pl.pallas_call(
        kern,
        grid=(b, n_pairs, n_parts),
        in_specs=specs + [pair_spec, cache_spec, cache_spec, row_spec, row_spec, row_spec, row_spec,
                          const((rows_u, n_cache)), const((rows_u, LANES))],
        out_specs=[pair_spec, cache_spec,
                   pl.BlockSpec((1, pairs_u, ts, LANES),
                                lambda i, j, k: (0, unit(i, j, k)[1], unit(i, j, k)[0], 0))],
        out_shape=[jax.ShapeDtypeStruct((b, n_pairs, t, LANES), jnp.bfloat16),
                   jax.ShapeDtypeStruct((nb, d_att, n_cache), jnp.float32),
                   jax.ShapeDtypeStruct((1, d_att // LANES, nb * ts, LANES), jnp.float32)],
        scratch_shapes=[pltpu.VMEM((t, LANES), jnp.float32) for _ in range(3 * len(dils))],
        compiler_params=pltpu.CompilerParams(
            dimension_semantics=("parallel", "parallel", "arbitrary"), vmem_limit_bytes=int(vmem)),
        name="attention",
    )(*ins, sg_pm, cache_kt, cache_vt, q_s, kn_s, vn_s, sg_s, mult_c, mult_n)


def _outproj_kernel(att_ref, z_ref, x_ref, w_bf, g_ref, b_ref, y_ref, *, n_pairs, n_chunks):
    tm = x_ref.shape[1]
    rows = tm // n_chunks
    for c in range(n_chunks):
        sl = slice(c * rows, (c + 1) * rows)
        parts = [att_ref[0, p, sl, :].astype(jnp.bfloat16) for p in range(n_pairs)]
        mix = jnp.concatenate(parts + [z_ref[0, sl, :].astype(jnp.bfloat16)], axis=-1)
        out = jnp.dot(mix, w_bf[...], preferred_element_type=jnp.float32)
        h = DEEPNORM_ALPHA * x_ref[0, sl, :] + out
        mu = jnp.mean(h, axis=-1, keepdims=True)
        d = h - mu
        var = jnp.mean(d * d, axis=-1, keepdims=True)
        y_ref[0, sl, :] = d * lax.rsqrt(var + LN_EPS) * g_ref[...] + b_ref[...]


def _outproj(att_pm, z, x, w_out_bf, ln_g, ln_b, *, tm):
    b, t, d = x.shape
    n_pairs = att_pm.shape[1]
    d_conv = z.shape[-1]
    d_mix = w_out_bf.shape[0]
    vmem = (d_mix * d * 2 + 4 * tm * d * 4 + 2 * tm * n_pairs * LANES * att_pm.dtype.itemsize
            + 2 * tm * d_conv * 2 + 6 * tm * d * 4)
    return pl.pallas_call(
        functools.partial(_outproj_kernel, n_pairs=n_pairs, n_chunks=OUTPROJ_CHUNKS),
        grid=(b, t // tm),
        in_specs=[
            pl.BlockSpec((1, n_pairs, tm, LANES), lambda i, j: (i, 0, j, 0)),
            pl.BlockSpec((1, tm, d_conv), lambda i, j: (i, j, 0)),
            pl.BlockSpec((1, tm, d), lambda i, j: (i, j, 0)),
            pl.BlockSpec((d_mix, d), lambda i, j: (0, 0), pipeline_mode=pl.Buffered(1)),
            pl.BlockSpec((1, d), lambda i, j: (0, 0)),
            pl.BlockSpec((1, d), lambda i, j: (0, 0)),
        ],
        out_specs=pl.BlockSpec((1, tm, d), lambda i, j: (i, j, 0)),
        out_shape=jax.ShapeDtypeStruct((b, t, d), jnp.float32),
        compiler_params=pltpu.CompilerParams(
            dimension_semantics=("parallel", "parallel"), vmem_limit_bytes=int(vmem)),
        name="outproj_layernorm",
    )(att_pm, z, x, w_out_bf, ln_g.reshape(1, d), ln_b.reshape(1, d))


def _sample_inproj_kernel(x_ref, w_ref, cw_ref, st_ref,
                          q_ref, k_ref, v_ref, sg_ref, z_ref, nc_ref, u_scr,
                          *, nb, ts, d_att, d_conv, q_scale):
    x = x_ref[...].astype(jnp.bfloat16)

    def proj(c0, width):
        return jnp.dot(x, w_ref[:, c0:c0 + width], preferred_element_type=jnp.float32)

    q_ref[...] = proj(0, d_att) * q_scale
    k_ref[...] = proj(d_att, d_att)
    v_ref[...] = proj(2 * d_att, d_att)
    sg_ref[...] = _silu(proj(3 * d_att, d_att))
    c0 = 4 * d_att
    gb = proj(c0, d_conv)
    gc = proj(c0 + d_conv, d_conv)
    hh = proj(c0 + 2 * d_conv, d_conv)
    g_conv = proj(c0 + 3 * d_conv, d_conv)
    u = (gc * hh).reshape(nb, ts, d_conv)
    u_scr[:, SUBLANES - 2:SUBLANES, :] = st_ref[...]
    u_scr[:, SUBLANES:SUBLANES + ts, :] = u
    um2 = u_scr[:, SUBLANES - 2:SUBLANES - 2 + ts, :]
    um1 = u_scr[:, SUBLANES - 1:SUBLANES - 1 + ts, :]
    cw = cw_ref[...]
    y_conv = cw[0:1, :][None] * um2 + cw[1:2, :][None] * um1 + cw[2:3, :][None] * u
    z = gb * y_conv.reshape(nb * ts, d_conv) * _silu(g_conv)
    z_ref[...] = z.astype(jnp.bfloat16)
    nc_ref[...] = u_scr[:, ts + SUBLANES - 2:ts + SUBLANES, :]


def _sample_inproj(x2, w_in_bf, conv_w, state, *, nb, ts, d_att, d_conv, q_scale):
    rows, d = x2.shape
    f32 = lambda c: jax.ShapeDtypeStruct((rows, c), jnp.float32)
    kern = functools.partial(_sample_inproj_kernel, nb=nb, ts=ts, d_att=d_att, d_conv=d_conv, q_scale=q_scale)
    vmem = d * w_in_bf.shape[1] * 2 + 16 * rows * d * 4 + 8 * MIB
    return pl.pallas_call(
        kern,
        out_shape=[f32(d_att), f32(d_att), f32(d_att), f32(d_att),
                   jax.ShapeDtypeStruct((rows, d_conv), jnp.bfloat16),
                   jax.ShapeDtypeStruct((nb, 2, d_conv), jnp.float32)],
        scratch_shapes=[pltpu.VMEM((nb, ts + SUBLANES, d_conv), jnp.float32)],
        compiler_params=pltpu.CompilerParams(vmem_limit_bytes=int(vmem)),
        name="sample_inproj",
    )(x2, w_in_bf, conv_w, state)


def kernel(x_prompt, x_sample, cache_k, cache_v, state_conv, w_in, conv_w, w_out, ln_g, ln_b):
    b, t, d = x_prompt.shape
    nb, ts, _ = x_sample.shape
    _, n_cache, n_heads, head_dim = cache_k.shape
    d_att = n_heads * head_dim
    d_conv = conv_w.shape[1]
    assert w_in.shape == (d, 4 * d_att + 4 * d_conv) and w_out.shape == (d_att + d_conv, d)
    assert d_att % LANES == 0 and LANES == 2 * head_dim and conv_w.shape[0] == 3
    max_window = max(w for w, _ in DILATED_PAIRS)
    assert n_cache == max_window and t <= max_window
    q_scale = head_dim ** -0.5 * math.log2(math.e)

    def to_time_minor(a):
        return jnp.transpose(a, (0, 2, 3, 1)).reshape(a.shape[0], d_att, a.shape[1])

    def from_time_minor(a):
        return jnp.transpose(a.reshape(a.shape[0], n_heads, head_dim, a.shape[2]), (0, 3, 1, 2))

    w_in_bf = w_in.astype(jnp.bfloat16)
    w_out_bf = w_out.astype(jnp.bfloat16)
    q_s, k_s, v_s, sg_s, z_s, new_conv_sample = _sample_inproj(
        x_sample.reshape(nb * ts, d), w_in_bf, conv_w, state_conv, nb=nb, ts=ts, d_att=d_att, d_conv=d_conv,
        q_scale=q_scale)
    three = lambda a: a.reshape(nb, ts, d_att)
    cache_kt, cache_vt = to_time_minor(cache_k), to_time_minor(cache_v)

    conv0 = jnp.zeros((b, 2, d_conv), jnp.float32)
    outs = _inproj(x_prompt, w_in_bf, conv_w, conv0, cache_vt, three(v_s),
                   tm=512, d_att=d_att, d_conv=d_conv, q_scale=q_scale)
    n_d = len(DILATED_PAIRS)
    qkv_st = outs[:n_d]
    kt_new, vt_new, sg_pm, z, new_conv_prompt, new_vt_s = outs[n_d:]

    att_pm, new_kt_s, att_s = _attention(
        qkv_st, sg_pm, cache_kt, cache_vt,
        three(q_s), three(k_s), three(v_s), three(sg_s), n_heads=n_heads, head_dim=head_dim)

    y_prompt = _outproj(att_pm, z, x_prompt, w_out_bf, ln_g, ln_b, tm=2048)
    y_sample = _outproj(att_s, z_s.reshape(1, nb * ts, d_conv), x_sample.reshape(1, nb * ts, d),
                        w_out_bf, ln_g, ln_b, tm=nb * ts)

    return (y_prompt, y_sample.reshape(nb, ts, d),
            from_time_minor(kt_new), from_time_minor(vt_new), new_conv_prompt,
            from_time_minor(new_kt_s), from_time_minor(new_vt_s), new_conv_sample)
```

```python
import functools
import math

import numpy as np
import jax
import jax.numpy as jnp
from jax import lax
from jax.experimental import pallas as pl
from jax.experimental.pallas import tpu as pltpu

DILATED_PAIRS = ((128, 1), (512, 4), (2048, 16))
LN_EPS = 1e-5
DEPTH = 1
DEEPNORM_ALPHA = (2.0 * DEPTH) ** 0.25

LANES = 128
SUBLANES = 8
Q_BLOCK = 128
NEG_BIG = -1e30
ATTN_PARTS = 1
OUTPROJ_CHUNKS = 2
CAST_ROWS = 128
BLOCK_GROUP = 8
L_FLOOR = 2.0 ** -100

MIB = 1024 * 1024


def _dilations():
    dils = tuple(sorted(d for _, d in DILATED_PAIRS))
    assert dils[0] == 1 and all(b % a == 0 for a, b in zip(dils, dils[1:]))
    return dils


def _silu(x):
    return x * (1.0 / (1.0 + jnp.exp(-x)))


def _new_rows_t(rows):
    ts, d = rows.shape
    pad = jnp.zeros((LANES - ts, d), rows.dtype)
    return jnp.concatenate([pad, rows], axis=0).T


def _shift_cache(old, new_t, ts, dst_ref):
    d, n_cache = old.shape
    main = n_cache - LANES
    is_old = lax.broadcasted_iota(jnp.int32, (d, LANES), 1) < LANES - ts
    rolled = pltpu.roll(old, n_cache - ts, axis=1)
    dst_ref[0, :, 0:main] = rolled[:, 0:main]
    dst_ref[0, :, main:n_cache] = jnp.where(is_old, rolled[:, main:n_cache], new_t)


def _cast_rows(src_ref, dst_ref):
    n = src_ref.shape[0] // CAST_ROWS

    def body(i, carry):
        rows = pl.ds(pl.multiple_of(i * CAST_ROWS, CAST_ROWS), CAST_ROWS)
        dst_ref[rows, :] = src_ref[rows, :].astype(dst_ref.dtype)
        return carry

    lax.fori_loop(0, n, body, 0)


def _emit_streams(val, refs, which, dils, scr, tm):
    n_pairs = val.shape[1] // LANES
    for p in range(n_pairs):
        col = val[:, p * LANES:(p + 1) * LANES]
        refs[0][0, p, which, 0] = col.astype(jnp.bfloat16)
        if len(dils) > 1:
            scr[0][...] = col
        for lvl in range(1, len(dils)):
            d_prev = dils[lvl - 1]
            f = dils[lvl] // d_prev
            n_prev = tm // d_prev
            n = n_prev // f
            src, dst = scr[(lvl - 1) % 2], scr[lvl % 2]
            for s in range(d_prev):
                for r2 in range(f):
                    r = s + d_prev * r2
                    rows = src[pl.ds(s * n_prev + r2, n, stride=f), :]
                    refs[lvl][0, p, which, r] = rows.astype(jnp.bfloat16)
                    if lvl + 1 < len(dils):
                        dst[r * n:(r + 1) * n, :] = rows


def _inproj_kernel(*refs, tm, d_att, d_conv, n_pairs, q_scale, dils):
    x_ref, w_bf, cw_ref, hist_ref, cv_ref, vn_ref = refs[:6]
    n_d = len(dils)
    st_refs = refs[6:6 + n_d]
    kt_ref, vt_ref, sg_ref, z_ref, nc_ref, nv_ref, u_scr, s_a, s_b = refs[6 + n_d:]

    @pl.when(pl.program_id(1) == 0)
    def _():
        u_scr[SUBLANES - 2:SUBLANES, :] = hist_ref[0]

    _shift_cache(cv_ref[0], _new_rows_t(vn_ref[0]), vn_ref.shape[1], nv_ref)

    x = x_ref[0].astype(jnp.bfloat16)

    def proj(c0, width):
        return jnp.dot(x, w_bf[:, c0:c0 + width], preferred_element_type=jnp.float32)

    c0 = 4 * d_att
    gc = proj(c0 + d_conv, d_conv)
    hh = proj(c0 + 2 * d_conv, d_conv)
    u = gc * hh
    u_scr[SUBLANES:SUBLANES + tm, :] = u
    um2 = u_scr[SUBLANES - 2:SUBLANES - 2 + tm, :]
    um1 = u_scr[SUBLANES - 1:SUBLANES - 1 + tm, :]
    y_conv = cw_ref[0:1, :] * um2 + cw_ref[1:2, :] * um1 + cw_ref[2:3, :] * u
    tail = u[tm - 2:tm, :]
    u_scr[SUBLANES - 2:SUBLANES, :] = tail
    nc_ref[0] = tail
    gb = proj(c0, d_conv)
    g_conv = proj(c0 + 3 * d_conv, d_conv)
    z_ref[0] = (gb * y_conv * _silu(g_conv)).astype(jnp.bfloat16)

    sg = _silu(proj(3 * d_att, d_att)).astype(jnp.bfloat16)
    for p in range(n_pairs):
        sg_ref[0, p] = sg[:, p * LANES:(p + 1) * LANES]
    k = proj(d_att, d_att)
    kt_ref[0] = k.T
    _emit_streams(k, st_refs, 1, dils, (s_a, s_b), tm)
    v = proj(2 * d_att, d_att)
    vt_ref[0] = v.T
    _emit_streams(v, st_refs, 2, dils, (s_a, s_b), tm)
    q = proj(0, d_att) * q_scale
    _emit_streams(q, st_refs, 0, dils, (s_a, s_b), tm)


def _inproj(x, w_in_bf, conv_w, hist, cache_vt, vn_s, *, tm, d_att, d_conv, q_scale):
    b, t, d = x.shape
    n_pairs = d_att // LANES
    n_out = w_in_bf.shape[1]
    dils = _dilations()
    assert tm % (dils[-1] * 2 * SUBLANES) == 0
    kern = functools.partial(_inproj_kernel, tm=tm, d_att=d_att, d_conv=d_conv,
                             n_pairs=n_pairs, q_scale=q_scale, dils=dils)
    st_shapes = [jax.ShapeDtypeStruct((b, n_pairs, 3, dd, t // dd, LANES), jnp.bfloat16) for dd in dils]
    st_specs = [pl.BlockSpec((1, n_pairs, 3, dd, tm // dd, LANES), lambda i, j: (i, 0, 0, 0, j, 0))
                for dd in dils]
    pm = jax.ShapeDtypeStruct((b, n_pairs, t, LANES), jnp.bfloat16)
    pm_spec = pl.BlockSpec((1, n_pairs, tm, LANES), lambda i, j: (i, 0, j, 0))
    row_spec = lambda c: pl.BlockSpec((1, tm, c), lambda i, j: (i, j, 0))
    col_spec = pl.BlockSpec((1, d_att, tm), lambda i, j: (i, 0, j))
    nb, _, n_cache = cache_vt.shape
    ts = vn_s.shape[1]
    n_t = t // tm
    assert nb == b * n_t
    cache_spec = pl.BlockSpec((1, d_att, n_cache), lambda i, j: (i * n_t + j, 0, 0))
    vmem = (d * n_out * 2 + 2 * tm * d * 4
            + 2 * tm * ((3 * len(dils) + 1) * d_att * 2 + 2 * d_att * 4 + d_conv * 2)
            + (tm + SUBLANES) * d_conv * 4 + 2 * tm * LANES * 4 + 12 * tm * d_att * 4
            + 5 * d_att * n_cache * 4)
    return pl.pallas_call(
        kern,
        grid=(b, n_t),
        in_specs=[
            row_spec(d),
            pl.BlockSpec((d, n_out), lambda i, j: (0, 0), pipeline_mode=pl.Buffered(1)),
            pl.BlockSpec(conv_w.shape, lambda i, j: (0, 0)),
            pl.BlockSpec((1, 2, d_conv), lambda i, j: (i, 0, 0)),
            cache_spec,
            pl.BlockSpec((1, ts, d_att), lambda i, j: (i * n_t + j, 0, 0)),
        ],
        out_specs=st_specs + [col_spec, col_spec, pm_spec,
                              row_spec(d_conv), pl.BlockSpec((1, 2, d_conv), lambda i, j: (i, 0, 0)),
                              cache_spec],
        out_shape=st_shapes + [
            jax.ShapeDtypeStruct((b, d_att, t), jnp.float32),
            jax.ShapeDtypeStruct((b, d_att, t), jnp.float32),
            pm,
            jax.ShapeDtypeStruct((b, t, d_conv), jnp.bfloat16),
            jax.ShapeDtypeStruct((b, 2, d_conv), jnp.float32),
            jax.ShapeDtypeStruct((nb, d_att, n_cache), jnp.float32)],
        scratch_shapes=[pltpu.VMEM((tm + SUBLANES, d_conv), jnp.float32),
                        pltpu.VMEM((tm, LANES), jnp.float32),
                        pltpu.VMEM((tm, LANES), jnp.float32)],
        compiler_params=pltpu.CompilerParams(
            dimension_semantics=("parallel", "arbitrary"), vmem_limit_bytes=int(vmem)),
        name="prompt_inproj",
    )(x, w_in_bf, conv_w, hist, cache_vt, vn_s)


def _block_scores(q, kwin, lo):
    zero = jnp.zeros_like(q)
    q2 = jnp.concatenate([jnp.where(lo, q, zero), jnp.where(lo, zero, q)], axis=0)
    return lax.dot_general(q2, kwin, (((1,), (1,)), ((), ())), preferred_element_type=jnp.float32)


def _block_probs(s, bias, row_exact):
    ps, ms = [], []
    for h in range(2):
        sh = s[h * Q_BLOCK:(h + 1) * Q_BLOCK] + bias
        if row_exact:
            m = jnp.max(sh, axis=-1, keepdims=True)
        else:
            m = jnp.max(jnp.max(sh, axis=0, keepdims=True), axis=1, keepdims=True)
        ps.append(jnp.exp2(sh - m).astype(jnp.bfloat16))
        ms.append(jnp.broadcast_to(m, (Q_BLOCK, 1)))
    return ps, ms


def _block_values(ps, ms, vwin, lo):
    lo_v = lax.broadcasted_iota(jnp.int32, vwin.shape, 1) < LANES // 2
    one = jnp.ones_like(vwin)
    pv0 = jnp.dot(ps[0], jnp.where(lo_v, vwin, one), preferred_element_type=jnp.float32)
    pv1 = jnp.dot(ps[1], jnp.where(lo_v, one, vwin), preferred_element_type=jnp.float32)
    acc = jnp.where(lo, pv0, pv1)
    l = pltpu.roll(jnp.where(lo, pv1, pv0), LANES // 2, axis=1)
    return acc, jnp.where(lo, ms[0], ms[1]), l


def _branch_multiplicity(n_q, n_cache):
    i = np.arange(n_q)[:, None]
    pos = np.arange(n_cache + n_q)[None, :]
    dist = n_cache + i - pos
    mult = np.zeros(dist.shape, np.float32)
    for window, dil in DILATED_PAIRS:
        mult += (dist >= 0) & (dist <= window) & (dist % dil == 0)
    return mult


def _sample_unit(ck_ref, cv_ref, q_ref, kn_ref, vn_ref, sg_ref, mc_ref, mn_ref,
                 nk_ref, att_ref, *, n_heads, head_dim, ts, n_cache, n_pairs):
    d_att = n_heads * head_dim
    rows = n_heads * ts
    head_of_lane = lax.broadcasted_iota(jnp.int32, (rows, d_att), 1) // head_dim
    head_of_row = lax.broadcasted_iota(jnp.int32, (rows, d_att), 0) // ts
    own = head_of_lane == head_of_row

    q = q_ref[0]
    qx = jnp.where(own, jnp.concatenate([q] * n_heads, axis=0), 0.0).astype(jnp.bfloat16)
    kct = ck_ref[0]
    vct = cv_ref[0]
    knt = _new_rows_t(kn_ref[0])
    vnt = _new_rows_t(vn_ref[0])
    nt = (((1,), (1,)), ((), ()))
    s_c = jnp.dot(qx, kct.astype(jnp.bfloat16), preferred_element_type=jnp.float32)
    s_n = jnp.dot(qx, knt.astype(jnp.bfloat16), preferred_element_type=jnp.float32)
    mc = mc_ref[...]
    mn = mn_ref[...]
    s_c = jnp.where(mc > 0, s_c, NEG_BIG)
    s_n = jnp.where(mn > 0, s_n, NEG_BIG)
    m = jnp.maximum(jnp.max(s_c, axis=-1, keepdims=True), jnp.max(s_n, axis=-1, keepdims=True))
    e_c = jnp.exp2(s_c - m) * mc
    e_n = jnp.exp2(s_n - m) * mn
    l = jnp.sum(e_c, axis=-1, keepdims=True) + jnp.sum(e_n, axis=-1, keepdims=True)
    ox = (lax.dot_general(e_c.astype(jnp.bfloat16), vct.astype(jnp.bfloat16), nt,
                          preferred_element_type=jnp.float32)
          + lax.dot_general(e_n.astype(jnp.bfloat16), vnt.astype(jnp.bfloat16), nt,
                            preferred_element_type=jnp.float32))
    ox = jnp.where(own, ox * (1.0 / l), 0.0)
    o = ox[0:ts]
    for h in range(1, n_heads):
        o = o + ox[h * ts:(h + 1) * ts]
    att = o * sg_ref[0]
    for p in range(n_pairs):
        att_ref[0, p] = att[:, p * LANES:(p + 1) * LANES]

    _shift_cache(kct, knt, ts, nk_ref)


def _split_streams(dils, seq, n_parts):
    items = sorted(((seq // d // Q_BLOCK, i, r) for i, d in enumerate(dils) for r in range(d) if i > 0),
                   reverse=True)
    parts, loads = [[] for _ in range(n_parts)], [0] * n_parts
    loads[-1] = seq // dils[0] // Q_BLOCK
    for cost, i, r in items:
        j = loads.index(min(loads))
        parts[j].append((i, r))
        loads[j] += cost
    parts[-1].append((0, 0))
    return parts


def _attention_kernel(*refs, seq, dils, head_dim, parts, sample_cfg):
    n_br = len(dils)
    qkv = refs[:n_br]
    sg_ref = refs[n_br]
    sample_in = refs[n_br + 1:n_br + 9]
    att_ref = refs[n_br + 9]
    sample_out = refs[n_br + 10:n_br + 12]
    scr = refs[n_br + 12:]
    a_scr, m_scr, l_scr = scr[0::3], scr[1::3], scr[2::3]
    part = pl.program_id(2)

    lane = lax.broadcasted_iota(jnp.int32, (Q_BLOCK, LANES), 1)
    lo = lane < head_dim
    row = lax.broadcasted_iota(jnp.int32, (Q_BLOCK, 2 * Q_BLOCK), 0)
    col = lax.broadcasted_iota(jnp.int32, (Q_BLOCK, 2 * Q_BLOCK), 1)
    band = jnp.where((col >= row) & (col <= row + Q_BLOCK), 0.0, NEG_BIG).astype(jnp.float32)
    row1 = lax.broadcasted_iota(jnp.int32, (Q_BLOCK, Q_BLOCK), 0)
    col1 = lax.broadcasted_iota(jnp.int32, (Q_BLOCK, Q_BLOCK), 1)
    causal = jnp.where(col1 <= row1, 0.0, NEG_BIG).astype(jnp.float32)

    def merge(row0):
        rows = slice(row0, row0 + Q_BLOCK)
        ms = [m_ref[rows, :] for m_ref in m_scr]
        top = functools.reduce(jnp.maximum, ms)
        ws = [jnp.exp2(m - top) for m in ms]
        num = functools.reduce(lambda a, b: a + b, [w * a_ref[rows, :] for w, a_ref in zip(ws, a_scr)])
        den = functools.reduce(lambda a, b: a + b, [w * l_ref[rows, :] for w, l_ref in zip(ws, l_scr)])
        att = num * (1.0 / den) * sg_ref[rows, :].astype(jnp.float32)
        att_ref[rows, :] = att.astype(att_ref.dtype)

    def run_blocks(items, row_exact):
        blocks = [(i, r, qb) for i, r in items for qb in range(seq // dils[i] // Q_BLOCK)]
        l_min = None
        for g in range(0, len(blocks), BLOCK_GROUP):
            group = blocks[g:g + BLOCK_GROUP]
            keys = [slice(max(qb - 1, 0) * Q_BLOCK, (qb + 1) * Q_BLOCK) for _, _, qb in group]
            scores = [_block_scores(qkv[i][0, r, qb * Q_BLOCK:(qb + 1) * Q_BLOCK, :], qkv[i][1, r, ks, :], lo)
                      for (i, r, qb), ks in zip(group, keys)]
            probs = [_block_probs(s, causal if qb == 0 else band, row_exact)
                     for s, (_, _, qb) in zip(scores, group)]
            for (i, r, qb), ks, (ps, ms) in zip(group, keys, probs):
                vals = _block_values(ps, ms, qkv[i][2, r, ks, :], lo)
                d, row0 = dils[i], qb * Q_BLOCK
                idx = pl.ds(row0, Q_BLOCK) if d == 1 else pl.ds(r + d * row0, Q_BLOCK, stride=d)
                for ref, val in zip((a_scr[i], m_scr[i], l_scr[i]), vals):
                    ref[idx, :] = val
                l_min = vals[2] if l_min is None else jnp.minimum(l_min, vals[2])
            for i, _, qb in group:
                if i == 0:
                    merge(qb * Q_BLOCK)
        return l_min

    for idx, items in enumerate(parts):
        @pl.when(part == idx)
        def _(items=items):
            _sample_unit(*sample_in, *sample_out, **sample_cfg)
            l_min = run_blocks(items, False)

            @pl.when(jnp.min(l_min) < L_FLOOR)
            def _():
                run_blocks(items, True)


def _attention(qkv_st, sg_pm, cache_kt, cache_vt, q_s, kn_s, vn_s, sg_s, *, n_heads, head_dim):
    b, n_pairs, t, _ = sg_pm.shape
    nb, d_att, n_cache = cache_kt.shape
    ts = q_s.shape[1]
    dils = _dilations()
    for w, d in DILATED_PAIRS:
        assert w == d * Q_BLOCK and t % (d * Q_BLOCK) == 0
    n_parts = ATTN_PARTS
    steps = b * n_pairs * n_parts
    assert steps % nb == 0 and n_heads % (steps // nb) == 0
    groups = steps // nb
    heads_u = n_heads // groups
    d_u = heads_u * head_dim
    assert d_u % LANES == 0
    pairs_u = d_u // LANES
    rows_u = heads_u * ts

    def unit(i, j, k):
        u = (i * n_pairs + j) * n_parts + k
        return u // groups, u % groups

    ins, specs = [], []
    for a, d in zip(qkv_st, dils):
        assert a.shape == (b, n_pairs, 3, d, t // d, LANES)
        ins.append(a)
        specs.append(pl.BlockSpec((None, None, 3, d, t // d, LANES), lambda i, j, k: (i, j, 0, 0, 0, 0)))
    pair_spec = pl.BlockSpec((None, None, t, LANES), lambda i, j, k: (i, j, 0, 0))
    cache_spec = pl.BlockSpec((1, d_u, n_cache), lambda i, j, k: (unit(i, j, k)[0], unit(i, j, k)[1], 0))
    row_spec = pl.BlockSpec((1, ts, d_u), lambda i, j, k: (unit(i, j, k)[0], 0, unit(i, j, k)[1]))
    mult = np.tile(_branch_multiplicity(ts, n_cache), (heads_u, 1))
    mult_c = jnp.asarray(mult[:, :n_cache])
    mult_n = jnp.asarray(np.pad(mult[:, n_cache:], ((0, 0), (LANES - ts, 0))))
    const = lambda shape: pl.BlockSpec(shape, lambda i, j, k: (0, 0))
    sample_cfg = dict(n_heads=heads_u, head_dim=head_dim, ts=ts, n_cache=n_cache, n_pairs=pairs_u)
    kern = functools.partial(_attention_kernel, seq=t, dils=dils, head_dim=head_dim,
                             parts=_split_streams(dils, t, n_parts), sample_cfg=sample_cfg)
    vmem = (2 * (3 * len(dils) + 2) * t * LANES * 2 + 3 * len(dils) * t * LANES * 4
            + 14 * d_u * n_cache * 4 + 8 * rows_u * n_cache * 4 + 12 * MIB)
    return pl.pallas_call(
        kern,
        grid=(b, n_pairs, n_parts),
        in_specs=specs + [pair_spec, cache_spec, cache_spec, row_spec, row_spec, row_spec, row_spec,
                          const((rows_u, n_cache)), const((rows_u, LANES))],
        out_specs=[pair_spec, cache_spec,
                   pl.BlockSpec((1, pairs_u, ts, LANES),
                                lambda i, j, k: (0, unit(i, j, k)[1], unit(i, j, k)[0], 0))],
        out_shape=[jax.ShapeDtypeStruct((b, n_pairs, t, LANES), jnp.bfloat16),
                   jax.ShapeDtypeStruct((nb, d_att, n_cache), jnp.float32),
                   jax.ShapeDtypeStruct((1, d_att // LANES, nb * ts, LANES), jnp.float32)],
        scratch_shapes=[pltpu.VMEM((t, LANES), jnp.float32) for _ in range(3 * len(dils))],
        compiler_params=pltpu.CompilerParams(
            dimension_semantics=("parallel", "parallel", "arbitrary"), vmem_limit_bytes=int(vmem)),
        name="attention",
    )(*ins, sg_pm, cache_kt, cache_vt, q_s, kn_s, vn_s, sg_s, mult_c, mult_n)


def _outproj_kernel(att_ref, z_ref, x_ref, w_ref, g_ref, b_ref, y_ref, w_bf, *, n_pairs, n_chunks):
    @pl.when((pl.program_id(0) == 0) & (pl.program_id(1) == 0))
    def _():
        _cast_rows(w_ref, w_bf)

    tm = x_ref.shape[1]
    rows = tm // n_chunks
    for c in range(n_chunks):
        sl = slice(c * rows, (c + 1) * rows)
        parts = [att_ref[0, p, sl, :].astype(jnp.bfloat16) for p in range(n_pairs)]
        mix = jnp.concatenate(parts + [z_ref[0, sl, :].astype(jnp.bfloat16)], axis=-1)
        out = jnp.dot(mix, w_bf[...], preferred_element_type=jnp.float32)
        h = DEEPNORM_ALPHA * x_ref[0, sl, :] + out
        mu = jnp.mean(h, axis=-1, keepdims=True)
        d = h - mu
        var = jnp.mean(d * d, axis=-1, keepdims=True)
        y_ref[0, sl, :] = d * lax.rsqrt(var + LN_EPS) * g_ref[...] + b_ref[...]


def _outproj(att_pm, z, x, w_out, ln_g, ln_b, *, tm):
    b, t, d = x.shape
    n_pairs = att_pm.shape[1]
    d_conv = z.shape[-1]
    d_mix = w_out.shape[0]
    vmem = (d_mix * d * (4 + 2) + 4 * tm * d * 4 + 2 * tm * n_pairs * LANES * att_pm.dtype.itemsize
            + 2 * tm * d_conv * 2 + 6 * tm * d * 4)
    return pl.pallas_call(
        functools.partial(_outproj_kernel, n_pairs=n_pairs, n_chunks=OUTPROJ_CHUNKS),
        grid=(b, t // tm),
        in_specs=[
            pl.BlockSpec((1, n_pairs, tm, LANES), lambda i, j: (i, 0, j, 0)),
            pl.BlockSpec((1, tm, d_conv), lambda i, j: (i, j, 0)),
            pl.BlockSpec((1, tm, d), lambda i, j: (i, j, 0)),
            pl.BlockSpec((d_mix, d), lambda i, j: (0, 0), pipeline_mode=pl.Buffered(1)),
            pl.BlockSpec((1, d), lambda i, j: (0, 0)),
            pl.BlockSpec((1, d), lambda i, j: (0, 0)),
        ],
        out_specs=pl.BlockSpec((1, tm, d), lambda i, j: (i, j, 0)),
        out_shape=jax.ShapeDtypeStruct((b, t, d), jnp.float32),
        scratch_shapes=[pltpu.VMEM((d_mix, d), jnp.bfloat16)],
        compiler_params=pltpu.CompilerParams(
            dimension_semantics=("arbitrary", "arbitrary"), vmem_limit_bytes=int(vmem)),
        name="outproj_layernorm",
    )(att_pm, z, x, w_out, ln_g.reshape(1, d), ln_b.reshape(1, d))


def _sample_inproj_kernel(x_ref, w_ref, cw_ref, st_ref,
                          q_ref, k_ref, v_ref, sg_ref, z_ref, nc_ref, u_scr,
                          *, nb, ts, d_att, d_conv, q_scale):
    x = x_ref[...].astype(jnp.bfloat16)

    def proj(c0, width):
        return jnp.dot(x, w_ref[:, c0:c0 + width], preferred_element_type=jnp.float32)

    q_ref[...] = proj(0, d_att) * q_scale
    k_ref[...] = proj(d_att, d_att)
    v_ref[...] = proj(2 * d_att, d_att)
    sg_ref[...] = _silu(proj(3 * d_att, d_att))
    c0 = 4 * d_att
    gb = proj(c0, d_conv)
    gc = proj(c0 + d_conv, d_conv)
    hh = proj(c0 + 2 * d_conv, d_conv)
    g_conv = proj(c0 + 3 * d_conv, d_conv)
    u = (gc * hh).reshape(nb, ts, d_conv)
    u_scr[:, SUBLANES - 2:SUBLANES, :] = st_ref[...]
    u_scr[:, SUBLANES:SUBLANES + ts, :] = u
    um2 = u_scr[:, SUBLANES - 2:SUBLANES - 2 + ts, :]
    um1 = u_scr[:, SUBLANES - 1:SUBLANES - 1 + ts, :]
    cw = cw_ref[...]
    y_conv = cw[0:1, :][None] * um2 + cw[1:2, :][None] * um1 + cw[2:3, :][None] * u
    z = gb * y_conv.reshape(nb * ts, d_conv) * _silu(g_conv)
    z_ref[...] = z.astype(jnp.bfloat16)
    nc_ref[...] = u_scr[:, ts + SUBLANES - 2:ts + SUBLANES, :]


def _sample_inproj(x2, w_in_bf, conv_w, state, *, nb, ts, d_att, d_conv, q_scale):
    rows, d = x2.shape
    f32 = lambda c: jax.ShapeDtypeStruct((rows, c), jnp.float32)
    kern = functools.partial(_sample_inproj_kernel, nb=nb, ts=ts, d_att=d_att, d_conv=d_conv, q_scale=q_scale)
    vmem = d * w_in_bf.shape[1] * 2 + 16 * rows * d * 4 + 8 * MIB
    return pl.pallas_call(
        kern,
        out_shape=[f32(d_att), f32(d_att), f32(d_att), f32(d_att),
                   jax.ShapeDtypeStruct((rows, d_conv), jnp.bfloat16),
                   jax.ShapeDtypeStruct((nb, 2, d_conv), jnp.float32)],
        scratch_shapes=[pltpu.VMEM((nb, ts + SUBLANES, d_conv), jnp.float32)],
        compiler_params=pltpu.CompilerParams(vmem_limit_bytes=int(vmem)),
        name="sample_inproj",
    )(x2, w_in_bf, conv_w, state)


def kernel(x_prompt, x_sample, cache_k, cache_v, state_conv, w_in, conv_w, w_out, ln_g, ln_b):
    b, t, d = x_prompt.shape
    nb, ts, _ = x_sample.shape
    _, n_cache, n_heads, head_dim = cache_k.shape
    d_att = n_heads * head_dim
    d_conv = conv_w.shape[1]
    assert w_in.shape == (d, 4 * d_att + 4 * d_conv) and w_out.shape == (d_att + d_conv, d)
    assert d_att % LANES == 0 and LANES == 2 * head_dim and conv_w.shape[0] == 3
    max_window = max(w for w, _ in DILATED_PAIRS)
    assert n_cache == max_window and t <= max_window
    q_scale = head_dim ** -0.5 * math.log2(math.e)

    def to_time_minor(a):
        return jnp.transpose(a, (0, 2, 3, 1)).reshape(a.shape[0], d_att, a.shape[1])

    def from_time_minor(a):
        return jnp.transpose(a.reshape(a.shape[0], n_heads, head_dim, a.shape[2]), (0, 3, 1, 2))

    w_in_bf = w_in.astype(jnp.bfloat16)
    q_s, k_s, v_s, sg_s, z_s, new_conv_sample = _sample_inproj(
        x_sample.reshape(nb * ts, d), w_in_bf, conv_w, state_conv, nb=nb, ts=ts, d_att=d_att, d_conv=d_conv,
        q_scale=q_scale)
    three = lambda a: a.reshape(nb, ts, d_att)
    cache_kt, cache_vt = to_time_minor(cache_k), to_time_minor(cache_v)

    conv0 = jnp.zeros((b, 2, d_conv), jnp.float32)
    outs = _inproj(x_prompt, w_in_bf, conv_w, conv0, cache_vt, three(v_s),
                   tm=512, d_att=d_att, d_conv=d_conv, q_scale=q_scale)
    n_d = len(DILATED_PAIRS)
    qkv_st = outs[:n_d]
    kt_new, vt_new, sg_pm, z, new_conv_prompt, new_vt_s = outs[n_d:]

    att_pm, new_kt_s, att_s = _attention(
        qkv_st, sg_pm, cache_kt, cache_vt,
        three(q_s), three(k_s), three(v_s), three(sg_s), n_heads=n_heads, head_dim=head_dim)

    y_prompt = _outproj(att_pm, z, x_prompt, w_out, ln_g, ln_b, tm=2048)
    y_sample = _outproj(att_s, z_s.reshape(1, nb * ts, d_conv), x_sample.reshape(1, nb * ts, d),
                        w_out, ln_g, ln_b, tm=nb * ts)

    return (y_prompt, y_sample.reshape(nb, ts, d),
            from_time_minor(kt_new), from_time_minor(vt_new), new_conv_prompt,
            from_time_minor(new_kt_s), from_time_minor(new_vt_s), new_conv_sample)
```

```python
import functools
import math

import numpy as np
import jax
import jax.numpy as jnp
from jax import lax
from jax.experimental import pallas as pl
from jax.experimental.pallas import tpu as pltpu

DILATED_PAIRS = ((128, 1), (512, 4), (2048, 16))
LN_EPS = 1e-5
DEPTH = 1
DEEPNORM_ALPHA = (2.0 * DEPTH) ** 0.25

LANES = 128
SUBLANES = 8
Q_BLOCK = 128
NEG_BIG = -1e30
ATTN_PARTS = 1
OUTPROJ_CHUNKS = 8
CAST_ROWS = 128
BLOCK_GROUP = 8
L_FLOOR = 2.0 ** -100

MIB = 1024 * 1024


def _dilations():
    dils = tuple(sorted(d for _, d in DILATED_PAIRS))
    assert dils[0] == 1 and all(b % a == 0 for a, b in zip(dils, dils[1:]))
    return dils


def _silu(x):
    return x * (1.0 / (1.0 + jnp.exp(-x)))


def _new_rows_t(rows):
    ts, d = rows.shape
    pad = jnp.zeros((LANES - ts, d), rows.dtype)
    return jnp.concatenate([pad, rows], axis=0).T


def _shift_cache(old, new_t, ts, dst_ref):
    d, n_cache = old.shape
    main = n_cache - LANES
    is_old = lax.broadcasted_iota(jnp.int32, (d, LANES), 1) < LANES - ts
    rolled = pltpu.roll(old, n_cache - ts, axis=1)
    dst_ref[0, :, 0:main] = rolled[:, 0:main]
    dst_ref[0, :, main:n_cache] = jnp.where(is_old, rolled[:, main:n_cache], new_t)


def _cast_rows(src_ref, dst_ref):
    n = src_ref.shape[0] // CAST_ROWS

    def body(i, carry):
        rows = pl.ds(pl.multiple_of(i * CAST_ROWS, CAST_ROWS), CAST_ROWS)
        dst_ref[rows, :] = src_ref[rows, :].astype(dst_ref.dtype)
        return carry

    lax.fori_loop(0, n, body, 0)


def _emit_streams(val, refs, which, dils, scr, tm):
    n_pairs = val.shape[1] // LANES
    for p in range(n_pairs):
        col = val[:, p * LANES:(p + 1) * LANES]
        refs[0][0, p, which, 0] = col.astype(jnp.bfloat16)
        if len(dils) > 1:
            scr[0][...] = col
        for lvl in range(1, len(dils)):
            d_prev = dils[lvl - 1]
            f = dils[lvl] // d_prev
            n_prev = tm // d_prev
            n = n_prev // f
            src, dst = scr[(lvl - 1) % 2], scr[lvl % 2]
            for s in range(d_prev):
                for r2 in range(f):
                    r = s + d_prev * r2
                    rows = src[pl.ds(s * n_prev + r2, n, stride=f), :]
                    refs[lvl][0, p, which, r] = rows.astype(jnp.bfloat16)
                    if lvl + 1 < len(dils):
                        dst[r * n:(r + 1) * n, :] = rows


def _inproj_kernel(*refs, tm, d_att, d_conv, n_pairs, q_scale, dils):
    x_ref, w_bf, cw_ref, hist_ref, cv_ref, vn_ref = refs[:6]
    n_d = len(dils)
    st_refs = refs[6:6 + n_d]
    kt_ref, vt_ref, sg_ref, z_ref, nc_ref, nv_ref, u_scr, s_a, s_b = refs[6 + n_d:]

    @pl.when(pl.program_id(1) == 0)
    def _():
        u_scr[SUBLANES - 2:SUBLANES, :] = hist_ref[0]

    _shift_cache(cv_ref[0], _new_rows_t(vn_ref[0]), vn_ref.shape[1], nv_ref)

    x = x_ref[0].astype(jnp.bfloat16)

    def proj(c0, width):
        return jnp.dot(x, w_bf[:, c0:c0 + width], preferred_element_type=jnp.float32)

    c0 = 4 * d_att
    gc = proj(c0 + d_conv, d_conv)
    hh = proj(c0 + 2 * d_conv, d_conv)
    u = gc * hh
    u_scr[SUBLANES:SUBLANES + tm, :] = u
    um2 = u_scr[SUBLANES - 2:SUBLANES - 2 + tm, :]
    um1 = u_scr[SUBLANES - 1:SUBLANES - 1 + tm, :]
    y_conv = cw_ref[0:1, :] * um2 + cw_ref[1:2, :] * um1 + cw_ref[2:3, :] * u
    tail = u[tm - 2:tm, :]
    u_scr[SUBLANES - 2:SUBLANES, :] = tail
    nc_ref[0] = tail
    gb = proj(c0, d_conv)
    g_conv = proj(c0 + 3 * d_conv, d_conv)
    z_ref[0] = (gb * y_conv * _silu(g_conv)).astype(jnp.bfloat16)

    sg = _silu(proj(3 * d_att, d_att)).astype(jnp.bfloat16)
    for p in range(n_pairs):
        sg_ref[0, p] = sg[:, p * LANES:(p + 1) * LANES]
    k = proj(d_att, d_att)
    kt_ref[0] = k.T
    _emit_streams(k, st_refs, 1, dils, (s_a, s_b), tm)
    v = proj(2 * d_att, d_att)
    vt_ref[0] = v.T
    _emit_streams(v, st_refs, 2, dils, (s_a, s_b), tm)
    q = proj(0, d_att) * q_scale
    _emit_streams(q, st_refs, 0, dils, (s_a, s_b), tm)


def _inproj(x, w_in_bf, conv_w, hist, cache_vt, vn_s, *, tm, d_att, d_conv, q_scale):
    b, t, d = x.shape
    n_pairs = d_att // LANES
    n_out = w_in_bf.shape[1]
    dils = _dilations()
    assert tm % (dils[-1] * 2 * SUBLANES) == 0
    kern = functools.partial(_inproj_kernel, tm=tm, d_att=d_att, d_conv=d_conv,
                             n_pairs=n_pairs, q_scale=q_scale, dils=dils)
    st_shapes = [jax.ShapeDtypeStruct((b, n_pairs, 3, dd, t // dd, LANES), jnp.bfloat16) for dd in dils]
    st_specs = [pl.BlockSpec((1, n_pairs, 3, dd, tm // dd, LANES), lambda i, j: (i, 0, 0, 0, j, 0))
                for dd in dils]
    pm = jax.ShapeDtypeStruct((b, n_pairs, t, LANES), jnp.bfloat16)
    pm_spec = pl.BlockSpec((1, n_pairs, tm, LANES), lambda i, j: (i, 0, j, 0))
    row_spec = lambda c: pl.BlockSpec((1, tm, c), lambda i, j: (i, j, 0))
    col_spec = pl.BlockSpec((1, d_att, tm), lambda i, j: (i, 0, j))
    nb, _, n_cache = cache_vt.shape
    ts = vn_s.shape[1]
    n_t = t // tm
    assert nb == b * n_t
    cache_spec = pl.BlockSpec((1, d_att, n_cache), lambda i, j: (i * n_t + j, 0, 0))
    vmem = (d * n_out * 2 + 2 * tm * d * 4
            + 2 * tm * ((3 * len(dils) + 1) * d_att * 2 + 2 * d_att * 4 + d_conv * 2)
            + (tm + SUBLANES) * d_conv * 4 + 2 * tm * LANES * 4 + 12 * tm * d_att * 4
            + 5 * d_att * n_cache * 4)
    return pl.pallas_call(
        kern,
        grid=(b, n_t),
        in_specs=[
            row_spec(d),
            pl.BlockSpec((d, n_out), lambda i, j: (0, 0), pipeline_mode=pl.Buffered(1)),
            pl.BlockSpec(conv_w.shape, lambda i, j: (0, 0)),
            pl.BlockSpec((1, 2, d_conv), lambda i, j: (i, 0, 0)),
            cache_spec,
            pl.BlockSpec((1, ts, d_att), lambda i, j: (i * n_t + j, 0, 0)),
        ],
        out_specs=st_specs + [col_spec, col_spec, pm_spec,
                              row_spec(d_conv), pl.BlockSpec((1, 2, d_conv), lambda i, j: (i, 0, 0)),
                              cache_spec],
        out_shape=st_shapes + [
            jax.ShapeDtypeStruct((b, d_att, t), jnp.float32),
            jax.ShapeDtypeStruct((b, d_att, t), jnp.float32),
            pm,
            jax.ShapeDtypeStruct((b, t, d_conv), jnp.bfloat16),
            jax.ShapeDtypeStruct((b, 2, d_conv), jnp.float32),
            jax.ShapeDtypeStruct((nb, d_att, n_cache), jnp.float32)],
        scratch_shapes=[pltpu.VMEM((tm + SUBLANES, d_conv), jnp.float32),
                        pltpu.VMEM((tm, LANES), jnp.float32),
                        pltpu.VMEM((tm, LANES), jnp.float32)],
        compiler_params=pltpu.CompilerParams(
            dimension_semantics=("parallel", "arbitrary"), vmem_limit_bytes=int(vmem)),
        name="prompt_inproj",
    )(x, w_in_bf, conv_w, hist, cache_vt, vn_s)


def _block_scores(q, kwin, lo):
    zero = jnp.zeros_like(q)
    q2 = jnp.concatenate([jnp.where(lo, q, zero), jnp.where(lo, zero, q)], axis=0)
    return lax.dot_general(q2, kwin, (((1,), (1,)), ((), ())), preferred_element_type=jnp.float32)


def _block_probs(s, bias, row_exact):
    ps, ms = [], []
    for h in range(2):
        sh = s[h * Q_BLOCK:(h + 1) * Q_BLOCK] + bias
        if row_exact:
            m = jnp.max(sh, axis=-1, keepdims=True)
        else:
            m = jnp.max(jnp.max(sh, axis=0, keepdims=True), axis=1, keepdims=True)
        ps.append(jnp.exp2(sh - m).astype(jnp.bfloat16))
        ms.append(jnp.broadcast_to(m, (Q_BLOCK, 1)))
    return ps, ms


def _block_values(ps, ms, vwin, lo):
    lo_v = lax.broadcasted_iota(jnp.int32, vwin.shape, 1) < LANES // 2
    one = jnp.ones_like(vwin)
    pv0 = jnp.dot(ps[0], jnp.where(lo_v, vwin, one), preferred_element_type=jnp.float32)
    pv1 = jnp.dot(ps[1], jnp.where(lo_v, one, vwin), preferred_element_type=jnp.float32)
    acc = jnp.where(lo, pv0, pv1)
    l = pltpu.roll(jnp.where(lo, pv1, pv0), LANES // 2, axis=1)
    return acc, jnp.where(lo, ms[0], ms[1]), l


def _branch_multiplicity(n_q, n_cache):
    i = np.arange(n_q)[:, None]
    pos = np.arange(n_cache + n_q)[None, :]
    dist = n_cache + i - pos
    mult = np.zeros(dist.shape, np.float32)
    for window, dil in DILATED_PAIRS:
        mult += (dist >= 0) & (dist <= window) & (dist % dil == 0)
    return mult


def _sample_unit(ck_ref, cv_ref, q_ref, kn_ref, vn_ref, sg_ref, mc_ref, mn_ref,
                 nk_ref, att_ref, *, n_heads, head_dim, ts, n_cache, n_pairs):
    d_att = n_heads * head_dim
    rows = n_heads * ts
    head_of_lane = lax.broadcasted_iota(jnp.int32, (rows, d_att), 1) // head_dim
    head_of_row = lax.broadcasted_iota(jnp.int32, (rows, d_att), 0) // ts
    own = head_of_lane == head_of_row

    q = q_ref[0]
    qx = jnp.where(own, jnp.concatenate([q] * n_heads, axis=0), 0.0).astype(jnp.bfloat16)
    kct = ck_ref[0]
    vct = cv_ref[0]
    knt = _new_rows_t(kn_ref[0])
    vnt = _new_rows_t(vn_ref[0])
    nt = (((1,), (1,)), ((), ()))
    s_c = jnp.dot(qx, kct.astype(jnp.bfloat16), preferred_element_type=jnp.float32)
    s_n = jnp.dot(qx, knt.astype(jnp.bfloat16), preferred_element_type=jnp.float32)
    mc = mc_ref[...]
    mn = mn_ref[...]
    s_c = jnp.where(mc > 0, s_c, NEG_BIG)
    s_n = jnp.where(mn > 0, s_n, NEG_BIG)
    m = jnp.maximum(jnp.max(s_c, axis=-1, keepdims=True), jnp.max(s_n, axis=-1, keepdims=True))
    e_c = jnp.exp2(s_c - m) * mc
    e_n = jnp.exp2(s_n - m) * mn
    l = jnp.sum(e_c, axis=-1, keepdims=True) + jnp.sum(e_n, axis=-1, keepdims=True)
    ox = (lax.dot_general(e_c.astype(jnp.bfloat16), vct.astype(jnp.bfloat16), nt,
                          preferred_element_type=jnp.float32)
          + lax.dot_general(e_n.astype(jnp.bfloat16), vnt.astype(jnp.bfloat16), nt,
                            preferred_element_type=jnp.float32))
    ox = jnp.where(own, ox * (1.0 / l), 0.0)
    o = ox[0:ts]
    for h in range(1, n_heads):
        o = o + ox[h * ts:(h + 1) * ts]
    att = o * sg_ref[0]
    for p in range(n_pairs):
        att_ref[0, p] = att[:, p * LANES:(p + 1) * LANES]

    _shift_cache(kct, knt, ts, nk_ref)


def _split_streams(dils, seq, n_parts):
    items = sorted(((seq // d // Q_BLOCK, i, r) for i, d in enumerate(dils) for r in range(d) if i > 0),
                   reverse=True)
    parts, loads = [[] for _ in range(n_parts)], [0] * n_parts
    loads[-1] = seq // dils[0] // Q_BLOCK
    for cost, i, r in items:
        j = loads.index(min(loads))
        parts[j].append((i, r))
        loads[j] += cost
    parts[-1].append((0, 0))
    return parts


def _attention_kernel(*refs, seq, dils, head_dim, parts, sample_cfg):
    n_br = len(dils)
    qkv = refs[:n_br]
    sg_ref = refs[n_br]
    sample_in = refs[n_br + 1:n_br + 9]
    att_ref = refs[n_br + 9]
    sample_out = refs[n_br + 10:n_br + 12]
    scr = refs[n_br + 12:]
    a_scr, m_scr, l_scr = scr[0::3], scr[1::3], scr[2::3]
    part = pl.program_id(2)

    lane = lax.broadcasted_iota(jnp.int32, (Q_BLOCK, LANES), 1)
    lo = lane < head_dim
    row = lax.broadcasted_iota(jnp.int32, (Q_BLOCK, 2 * Q_BLOCK), 0)
    col = lax.broadcasted_iota(jnp.int32, (Q_BLOCK, 2 * Q_BLOCK), 1)
    band = jnp.where((col >= row) & (col <= row + Q_BLOCK), 0.0, NEG_BIG).astype(jnp.float32)
    row1 = lax.broadcasted_iota(jnp.int32, (Q_BLOCK, Q_BLOCK), 0)
    col1 = lax.broadcasted_iota(jnp.int32, (Q_BLOCK, Q_BLOCK), 1)
    causal = jnp.where(col1 <= row1, 0.0, NEG_BIG).astype(jnp.float32)

    def merge(row0):
        rows = slice(row0, row0 + Q_BLOCK)
        ms = [m_ref[rows, :] for m_ref in m_scr]
        top = functools.reduce(jnp.maximum, ms)
        ws = [jnp.exp2(m - top) for m in ms]
        num = functools.reduce(lambda a, b: a + b, [w * a_ref[rows, :] for w, a_ref in zip(ws, a_scr)])
        den = functools.reduce(lambda a, b: a + b, [w * l_ref[rows, :] for w, l_ref in zip(ws, l_scr)])
        att = num * (1.0 / den) * sg_ref[rows, :].astype(jnp.float32)
        att_ref[rows, :] = att.astype(att_ref.dtype)

    def run_blocks(items, row_exact):
        blocks = [(i, r, qb) for i, r in items for qb in range(seq // dils[i] // Q_BLOCK)]
        l_min = None
        for g in range(0, len(blocks), BLOCK_GROUP):
            group = blocks[g:g + BLOCK_GROUP]
            keys = [slice(max(qb - 1, 0) * Q_BLOCK, (qb + 1) * Q_BLOCK) for _, _, qb in group]
            scores = [_block_scores(qkv[i][0, r, qb * Q_BLOCK:(qb + 1) * Q_BLOCK, :], qkv[i][1, r, ks, :], lo)
                      for (i, r, qb), ks in zip(group, keys)]
            probs = [_block_probs(s, causal if qb == 0 else band, row_exact)
                     for s, (_, _, qb) in zip(scores, group)]
            for (i, r, qb), ks, (ps, ms) in zip(group, keys, probs):
                vals = _block_values(ps, ms, qkv[i][2, r, ks, :], lo)
                d, row0 = dils[i], qb * Q_BLOCK
                idx = pl.ds(row0, Q_BLOCK) if d == 1 else pl.ds(r + d * row0, Q_BLOCK, stride=d)
                for ref, val in zip((a_scr[i], m_scr[i], l_scr[i]), vals):
                    ref[idx, :] = val
                l_min = vals[2] if l_min is None else jnp.minimum(l_min, vals[2])
            for i, _, qb in group:
                if i == 0:
                    merge(qb * Q_BLOCK)
        return l_min

    for idx, items in enumerate(parts):
        @pl.when(part == idx)
        def _(items=items):
            _sample_unit(*sample_in, *sample_out, **sample_cfg)
            l_min = run_blocks(items, False)

            @pl.when(jnp.min(l_min) < L_FLOOR)
            def _():
                run_blocks(items, True)


def _attention(qkv_st, sg_pm, cache_kt, cache_vt, q_s, kn_s, vn_s, sg_s, *, n_heads, head_dim):
    b, n_pairs, t, _ = sg_pm.shape
    nb, d_att, n_cache = cache_kt.shape
    ts = q_s.shape[1]
    dils = _dilations()
    for w, d in DILATED_PAIRS:
        assert w == d * Q_BLOCK and t % (d * Q_BLOCK) == 0
    n_parts = ATTN_PARTS
    steps = b * n_pairs * n_parts
    assert steps % nb == 0 and n_heads % (steps // nb) == 0
    groups = steps // nb
    heads_u = n_heads // groups
    d_u = heads_u * head_dim
    assert d_u % LANES == 0
    pairs_u = d_u // LANES
    rows_u = heads_u * ts

    def unit(i, j, k):
        u = (i * n_pairs + j) * n_parts + k
        return u // groups, u % groups

    ins, specs = [], []
    for a, d in zip(qkv_st, dils):
        assert a.shape == (b, n_pairs, 3, d, t // d, LANES)
        ins.append(a)
        specs.append(pl.BlockSpec((None, None, 3, d, t // d, LANES), lambda i, j, k: (i, j, 0, 0, 0, 0)))
    pair_spec = pl.BlockSpec((None, None, t, LANES), lambda i, j, k: (i, j, 0, 0))
    cache_spec = pl.BlockSpec((1, d_u, n_cache), lambda i, j, k: (unit(i, j, k)[0], unit(i, j, k)[1], 0))
    row_spec = pl.BlockSpec((1, ts, d_u), lambda i, j, k: (unit(i, j, k)[0], 0, unit(i, j, k)[1]))
    mult = np.tile(_branch_multiplicity(ts, n_cache), (heads_u, 1))
    mult_c = jnp.asarray(mult[:, :n_cache])
    mult_n = jnp.asarray(np.pad(mult[:, n_cache:], ((0, 0), (LANES - ts, 0))))
    const = lambda shape: pl.BlockSpec(shape, lambda i, j, k: (0, 0))
    sample_cfg = dict(n_heads=heads_u, head_dim=head_dim, ts=ts, n_cache=n_cache, n_pairs=pairs_u)
    kern = functools.partial(_attention_kernel, seq=t, dils=dils, head_dim=head_dim,
                             parts=_split_streams(dils, t, n_parts), sample_cfg=sample_cfg)
    vmem = (2 * (3 * len(dils) + 2) * t * LANES * 2 + 3 * len(dils) * t * LANES * 4
            + 14 * d_u * n_cache * 4 + 8 * rows_u * n_cache * 4 + 12 * MIB)
    return pl.pallas_call(
        kern,
        grid=(b, n_pairs, n_parts),
        in_specs=specs + [pair_spec, cache_spec, cache_spec, row_spec, row_spec, row_spec, row_spec,
                          const((rows_u, n_cache)), const((rows_u, LANES))],
        out_specs=[pair_spec, cache_spec,
                   pl.BlockSpec((1, pairs_u, ts, LANES),
                                lambda i, j, k: (0, unit(i, j, k)[1], unit(i, j, k)[0], 0))],
        out_shape=[jax.ShapeDtypeStruct((b, n_pairs, t, LANES), jnp.bfloat16),
                   jax.ShapeDtypeStruct((nb, d_att, n_cache), jnp.float32),
                   jax.ShapeDtypeStruct((1, d_att // LANES, nb * ts, LANES), jnp.float32)],
        scratch_shapes=[pltpu.VMEM((t, LANES), jnp.float32) for _ in range(3 * len(dils))],
        compiler_params=pltpu.CompilerParams(
            dimension_semantics=("parallel", "parallel", "arbitrary"), vmem_limit_bytes=int(vmem)),
        name="attention",
    )(*ins, sg_pm, cache_kt, cache_vt, q_s, kn_s, vn_s, sg_s, mult_c, mult_n)


def _outproj_kernel(att_ref, z_ref, x_ref, w_ref, g_ref, b_ref, y_ref, w_bf, *, n_pairs, n_chunks):
    @pl.when((pl.program_id(0) == 0) & (pl.program_id(1) == 0))
    def _():
        _cast_rows(w_ref, w_bf)

    tm = x_ref.shape[1]
    rows = tm // n_chunks
    for c in range(n_chunks):
        sl = slice(c * rows, (c + 1) * rows)
        parts = [att_ref[0, p, sl, :].astype(jnp.bfloat16) for p in range(n_pairs)]
        mix = jnp.concatenate(parts + [z_ref[0, sl, :].astype(jnp.bfloat16)], axis=-1)
        out = jnp.dot(mix, w_bf[...], preferred_element_type=jnp.float32)
        h = DEEPNORM_ALPHA * x_ref[0, sl, :] + out
        mu = jnp.mean(h, axis=-1, keepdims=True)
        d = h - mu
        var = jnp.mean(d * d, axis=-1, keepdims=True)
        y_ref[0, sl, :] = d * lax.rsqrt(var + LN_EPS) * g_ref[...] + b_ref[...]


def _outproj(att_pm, z, x, w_out, ln_g, ln_b, *, tm):
    b, t, d = x.shape
    n_pairs = att_pm.shape[1]
    d_conv = z.shape[-1]
    d_mix = w_out.shape[0]
    vmem = (d_mix * d * (4 + 2) + 4 * tm * d * 4 + 2 * tm * n_pairs * LANES * att_pm.dtype.itemsize
            + 2 * tm * d_conv * 2 + 6 * tm * d * 4)
    return pl.pallas_call(
        functools.partial(_outproj_kernel, n_pairs=n_pairs, n_chunks=OUTPROJ_CHUNKS),
        grid=(b, t // tm),
        in_specs=[
            pl.BlockSpec((1, n_pairs, tm, LANES), lambda i, j: (i, 0, j, 0)),
            pl.BlockSpec((1, tm, d_conv), lambda i, j: (i, j, 0)),
            pl.BlockSpec((1, tm, d), lambda i, j: (i, j, 0)),
            pl.BlockSpec((d_mix, d), lambda i, j: (0, 0), pipeline_mode=pl.Buffered(1)),
            pl.BlockSpec((1, d), lambda i, j: (0, 0)),
            pl.BlockSpec((1, d), lambda i, j: (0, 0)),
        ],
        out_specs=pl.BlockSpec((1, tm, d), lambda i, j: (i, j, 0)),
        out_shape=jax.ShapeDtypeStruct((b, t, d), jnp.float32),
        scratch_shapes=[pltpu.VMEM((d_mix, d), jnp.bfloat16)],
        compiler_params=pltpu.CompilerParams(
            dimension_semantics=("arbitrary", "arbitrary"), vmem_limit_bytes=int(vmem)),
        name="outproj_layernorm",
    )(att_pm, z, x, w_out, ln_g.reshape(1, d), ln_b.reshape(1, d))


def _sample_inproj_kernel(x_ref, w_ref, cw_ref, st_ref,
                          q_ref, k_ref, v_ref, sg_ref, z_ref, nc_ref, u_scr,
                          *, nb, ts, d_att, d_conv, q_scale):
    x = x_ref[...].astype(jnp.bfloat16)

    def proj(c0, width):
        return jnp.dot(x, w_ref[:, c0:c0 + width], preferred_element_type=jnp.float32)

    q_ref[...] = proj(0, d_att) * q_scale
    k_ref[...] = proj(d_att, d_att)
    v_ref[...] = proj(2 * d_att, d_att)
    sg_ref[...] = _silu(proj(3 * d_att, d_att))
    c0 = 4 * d_att
    gb = proj(c0, d_conv)
    gc = proj(c0 + d_conv, d_conv)
    hh = proj(c0 + 2 * d_conv, d_conv)
    g_conv = proj(c0 + 3 * d_conv, d_conv)
    u = (gc * hh).reshape(nb, ts, d_conv)
    u_scr[:, SUBLANES - 2:SUBLANES, :] = st_ref[...]
    u_scr[:, SUBLANES:SUBLANES + ts, :] = u
    um2 = u_scr[:, SUBLANES - 2:SUBLANES - 2 + ts, :]
    um1 = u_scr[:, SUBLANES - 1:SUBLANES - 1 + ts, :]
    cw = cw_ref[...]
    y_conv = cw[0:1, :][None] * um2 + cw[1:2, :][None] * um1 + cw[2:3, :][None] * u
    z = gb * y_conv.reshape(nb * ts, d_conv) * _silu(g_conv)
    z_ref[...] = z.astype(jnp.bfloat16)
    nc_ref[...] = u_scr[:, ts + SUBLANES - 2:ts + SUBLANES, :]


def _sample_inproj(x2, w_in_bf, conv_w, state, *, nb, ts, d_att, d_conv, q_scale):
    rows, d = x2.shape
    f32 = lambda c: jax.ShapeDtypeStruct((rows, c), jnp.float32)
    kern = functools.partial(_sample_inproj_kernel, nb=nb, ts=ts, d_att=d_att, d_conv=d_conv, q_scale=q_scale)
    vmem = d * w_in_bf.shape[1] * 2 + 16 * rows * d * 4 + 8 * MIB
    return pl.pallas_call(
        kern,
        out_shape=[f32(d_att), f32(d_att), f32(d_att), f32(d_att),
                   jax.ShapeDtypeStruct((rows, d_conv), jnp.bfloat16),
                   jax.ShapeDtypeStruct((nb, 2, d_conv), jnp.float32)],
        scratch_shapes=[pltpu.VMEM((nb, ts + SUBLANES, d_conv), jnp.float32)],
        compiler_params=pltpu.CompilerParams(vmem_limit_bytes=int(vmem)),
        name="sample_inproj",
    )(x2, w_in_bf, conv_w, state)


def kernel(x_prompt, x_sample, cache_k, cache_v, state_conv, w_in, conv_w, w_out, ln_g, ln_b):
    b, t, d = x_prompt.shape
    nb, ts, _ = x_sample.shape
    _, n_cache, n_heads, head_dim = cache_k.shape
    d_att = n_heads * head_dim
    d_conv = conv_w.shape[1]
    assert w_in.shape == (d, 4 * d_att + 4 * d_conv) and w_out.shape == (d_att + d_conv, d)
    assert d_att % LANES == 0 and LANES == 2 * head_dim and conv_w.shape[0] == 3
    max_window = max(w for w, _ in DILATED_PAIRS)
    assert n_cache == max_window and t <= max_window
    q_scale = head_dim ** -0.5 * math.log2(math.e)

    def to_time_minor(a):
        return jnp.transpose(a, (0, 2, 3, 1)).reshape(a.shape[0], d_att, a.shape[1])

    def from_time_minor(a):
        return jnp.transpose(a.reshape(a.shape[0], n_heads, head_dim, a.shape[2]), (0, 3, 1, 2))

    w_in_bf = w_in.astype(jnp.bfloat16)
    q_s, k_s, v_s, sg_s, z_s, new_conv_sample = _sample_inproj(
        x_sample.reshape(nb * ts, d), w_in_bf, conv_w, state_conv, nb=nb, ts=ts, d_att=d_att, d_conv=d_conv,
        q_scale=q_scale)
    three = lambda a: a.reshape(nb, ts, d_att)
    cache_kt, cache_vt = to_time_minor(cache_k), to_time_minor(cache_v)

    conv0 = jnp.zeros((b, 2, d_conv), jnp.float32)
    outs = _inproj(x_prompt, w_in_bf, conv_w, conv0, cache_vt, three(v_s),
                   tm=512, d_att=d_att, d_conv=d_conv, q_scale=q_scale)
    n_d = len(DILATED_PAIRS)
    qkv_st = outs[:n_d]
    kt_new, vt_new, sg_pm, z, new_conv_prompt, new_vt_s = outs[n_d:]

    att_pm, new_kt_s, att_s = _attention(
        qkv_st, sg_pm, cache_kt, cache_vt,
        three(q_s), three(k_s), three(v_s), three(sg_s), n_heads=n_heads, head_dim=head_dim)

    y_prompt = _outproj(att_pm, z, x_prompt, w_out, ln_g, ln_b, tm=2048)
    y_sample = _outproj(att_s, z_s.reshape(1, nb * ts, d_conv), x_sample.reshape(1, nb * ts, d),
                        w_out, ln_g, ln_b, tm=nb * ts)

    return (y_prompt, y_sample.reshape(nb, ts, d),
            from_time_minor(kt_new), from_time_minor(vt_new), new_conv_prompt,
            from_time_minor(new_kt_s), from_time_minor(new_vt_s), new_conv_sample)
```

```python
import functools
import math

import numpy as np
import jax
import jax.numpy as jnp
from jax import lax
from jax.experimental import pallas as pl
from jax.experimental.pallas import tpu as pltpu

DILATED_PAIRS = ((128, 1), (512, 4), (2048, 16))
LN_EPS = 1e-5
DEPTH = 1
DEEPNORM_ALPHA = (2.0 * DEPTH) ** 0.25

LANES = 128
SUBLANES = 8
Q_BLOCK = 128
NEG_BIG = -1e30
ATTN_PARTS = 1
OUTPROJ_CHUNK_ROWS = 256
CAST_ROWS = 128
BLOCK_GROUP = 8
L_FLOOR = 2.0 ** -100

MIB = 1024 * 1024


def _dilations():
    dils = tuple(sorted(d for _, d in DILATED_PAIRS))
    assert dils[0] == 1 and all(b % a == 0 for a, b in zip(dils, dils[1:]))
    return dils


def _silu(x):
    return x * (1.0 / (1.0 + jnp.exp(-x)))


def _new_rows_t(rows):
    ts, d = rows.shape
    pad = jnp.zeros((LANES - ts, d), rows.dtype)
    return jnp.concatenate([pad, rows], axis=0).T


def _shift_cache(old, new_t, ts, dst_ref):
    d, n_cache = old.shape
    main = n_cache - LANES
    is_old = lax.broadcasted_iota(jnp.int32, (d, LANES), 1) < LANES - ts
    rolled = pltpu.roll(old, n_cache - ts, axis=1)
    dst_ref[0, :, 0:main] = rolled[:, 0:main]
    dst_ref[0, :, main:n_cache] = jnp.where(is_old, rolled[:, main:n_cache], new_t)


def _cast_rows(src_ref, dst_ref):
    n = src_ref.shape[0] // CAST_ROWS

    def body(i, carry):
        rows = pl.ds(pl.multiple_of(i * CAST_ROWS, CAST_ROWS), CAST_ROWS)
        dst_ref[rows, :] = src_ref[rows, :].astype(dst_ref.dtype)
        return carry

    lax.fori_loop(0, n, body, 0)


def _emit_streams(val, refs, which, dils, scr, tm):
    n_pairs = val.shape[1] // LANES
    for p in range(n_pairs):
        col = val[:, p * LANES:(p + 1) * LANES]
        refs[0][0, p, which, 0] = col.astype(jnp.bfloat16)
        if len(dils) > 1:
            scr[0][...] = col
        for lvl in range(1, len(dils)):
            d_prev = dils[lvl - 1]
            f = dils[lvl] // d_prev
            n_prev = tm // d_prev
            n = n_prev // f
            src, dst = scr[(lvl - 1) % 2], scr[lvl % 2]
            for s in range(d_prev):
                for r2 in range(f):
                    r = s + d_prev * r2
                    rows = src[pl.ds(s * n_prev + r2, n, stride=f), :]
                    refs[lvl][0, p, which, r] = rows.astype(jnp.bfloat16)
                    if lvl + 1 < len(dils):
                        dst[r * n:(r + 1) * n, :] = rows


def _inproj_kernel(*refs, tm, d_att, d_conv, n_pairs, q_scale, dils):
    x_ref, w_bf, cw_ref, hist_ref, cv_ref, vn_ref = refs[:6]
    n_d = len(dils)
    st_refs = refs[6:6 + n_d]
    kt_ref, vt_ref, sg_ref, z_ref, nc_ref, nv_ref, u_scr, s_a, s_b = refs[6 + n_d:]

    @pl.when(pl.program_id(1) == 0)
    def _():
        u_scr[SUBLANES - 2:SUBLANES, :] = hist_ref[0]

    _shift_cache(cv_ref[0], _new_rows_t(vn_ref[0]), vn_ref.shape[1], nv_ref)

    x = x_ref[0].astype(jnp.bfloat16)

    def proj(c0, width):
        return jnp.dot(x, w_bf[:, c0:c0 + width], preferred_element_type=jnp.float32)

    c0 = 4 * d_att
    gc = proj(c0 + d_conv, d_conv)
    hh = proj(c0 + 2 * d_conv, d_conv)
    u = gc * hh
    u_scr[SUBLANES:SUBLANES + tm, :] = u
    um2 = u_scr[SUBLANES - 2:SUBLANES - 2 + tm, :]
    um1 = u_scr[SUBLANES - 1:SUBLANES - 1 + tm, :]
    y_conv = cw_ref[0:1, :] * um2 + cw_ref[1:2, :] * um1 + cw_ref[2:3, :] * u
    tail = u[tm - 2:tm, :]
    u_scr[SUBLANES - 2:SUBLANES, :] = tail
    nc_ref[0] = tail
    gb = proj(c0, d_conv)
    g_conv = proj(c0 + 3 * d_conv, d_conv)
    z_ref[0] = (gb * y_conv * _silu(g_conv)).astype(jnp.bfloat16)

    sg = _silu(proj(3 * d_att, d_att)).astype(jnp.bfloat16)
    for p in range(n_pairs):
        sg_ref[0, p] = sg[:, p * LANES:(p + 1) * LANES]
    k = proj(d_att, d_att)
    kt_ref[0] = k.T
    _emit_streams(k, st_refs, 1, dils, (s_a, s_b), tm)
    v = proj(2 * d_att, d_att)
    vt_ref[0] = v.T
    _emit_streams(v, st_refs, 2, dils, (s_a, s_b), tm)
    q = proj(0, d_att) * q_scale
    _emit_streams(q, st_refs, 0, dils, (s_a, s_b), tm)


def _inproj(x, w_in_bf, conv_w, hist, cache_vt, vn_s, *, tm, d_att, d_conv, q_scale):
    b, t, d = x.shape
    n_pairs = d_att // LANES
    n_out = w_in_bf.shape[1]
    dils = _dilations()
    assert tm % (dils[-1] * 2 * SUBLANES) == 0
    kern = functools.partial(_inproj_kernel, tm=tm, d_att=d_att, d_conv=d_conv,
                             n_pairs=n_pairs, q_scale=q_scale, dils=dils)
    st_shapes = [jax.ShapeDtypeStruct((b, n_pairs, 3, dd, t // dd, LANES), jnp.bfloat16) for dd in dils]
    st_specs = [pl.BlockSpec((1, n_pairs, 3, dd, tm // dd, LANES), lambda i, j: (i, 0, 0, 0, j, 0))
                for dd in dils]
    pm = jax.ShapeDtypeStruct((b, n_pairs, t, LANES), jnp.bfloat16)
    pm_spec = pl.BlockSpec((1, n_pairs, tm, LANES), lambda i, j: (i, 0, j, 0))
    row_spec = lambda c: pl.BlockSpec((1, tm, c), lambda i, j: (i, j, 0))
    col_spec = pl.BlockSpec((1, d_att, tm), lambda i, j: (i, 0, j))
    nb, _, n_cache = cache_vt.shape
    ts = vn_s.shape[1]
    n_t = t // tm
    assert nb == b * n_t
    cache_spec = pl.BlockSpec((1, d_att, n_cache), lambda i, j: (i * n_t + j, 0, 0))
    vmem = (d * n_out * 2 + 2 * tm * d * 4
            + 2 * tm * ((3 * len(dils) + 1) * d_att * 2 + 2 * d_att * 4 + d_conv * 2)
            + (tm + SUBLANES) * d_conv * 4 + 2 * tm * LANES * 4 + 12 * tm * d_att * 4
            + 5 * d_att * n_cache * 4)
    return pl.pallas_call(
        kern,
        grid=(b, n_t),
        in_specs=[
            row_spec(d),
            pl.BlockSpec((d, n_out), lambda i, j: (0, 0), pipeline_mode=pl.Buffered(1)),
            pl.BlockSpec(conv_w.shape, lambda i, j: (0, 0)),
            pl.BlockSpec((1, 2, d_conv), lambda i, j: (i, 0, 0)),
            cache_spec,
            pl.BlockSpec((1, ts, d_att), lambda i, j: (i * n_t + j, 0, 0)),
        ],
        out_specs=st_specs + [col_spec, col_spec, pm_spec,
                              row_spec(d_conv), pl.BlockSpec((1, 2, d_conv), lambda i, j: (i, 0, 0)),
                              cache_spec],
        out_shape=st_shapes + [
            jax.ShapeDtypeStruct((b, d_att, t), jnp.float32),
            jax.ShapeDtypeStruct((b, d_att, t), jnp.float32),
            pm,
            jax.ShapeDtypeStruct((b, t, d_conv), jnp.bfloat16),
            jax.ShapeDtypeStruct((b, 2, d_conv), jnp.float32),
            jax.ShapeDtypeStruct((nb, d_att, n_cache), jnp.float32)],
        scratch_shapes=[pltpu.VMEM((tm + SUBLANES, d_conv), jnp.float32),
                        pltpu.VMEM((tm, LANES), jnp.float32),
                        pltpu.VMEM((tm, LANES), jnp.float32)],
        compiler_params=pltpu.CompilerParams(
            dimension_semantics=("parallel", "arbitrary"), vmem_limit_bytes=int(vmem)),
        name="prompt_inproj",
    )(x, w_in_bf, conv_w, hist, cache_vt, vn_s)


def _block_scores(q, kwin, lo):
    zero = jnp.zeros_like(q)
    q2 = jnp.concatenate([jnp.where(lo, q, zero), jnp.where(lo, zero, q)], axis=0)
    return lax.dot_general(q2, kwin, (((1,), (1,)), ((), ())), preferred_element_type=jnp.float32)


def _block_probs(s, bias, row_exact):
    ps, ms = [], []
    for h in range(2):
        sh = s[h * Q_BLOCK:(h + 1) * Q_BLOCK] + bias
        if row_exact:
            m = jnp.max(sh, axis=-1, keepdims=True)
        else:
            m = jnp.max(jnp.max(sh, axis=0, keepdims=True), axis=1, keepdims=True)
        ps.append(jnp.exp2(sh - m).astype(jnp.bfloat16))
        ms.append(jnp.broadcast_to(m, (Q_BLOCK, 1)))
    return ps, ms


def _block_values(ps, ms, vwin, lo):
    lo_v = lax.broadcasted_iota(jnp.int32, vwin.shape, 1) < LANES // 2
    one = jnp.ones_like(vwin)
    pv0 = jnp.dot(ps[0], jnp.where(lo_v, vwin, one), preferred_element_type=jnp.float32)
    pv1 = jnp.dot(ps[1], jnp.where(lo_v, one, vwin), preferred_element_type=jnp.float32)
    acc = jnp.where(lo, pv0, pv1)
    l = pltpu.roll(jnp.where(lo, pv1, pv0), LANES // 2, axis=1)
    return acc, jnp.where(lo, ms[0], ms[1]), l


def _branch_multiplicity(n_q, n_cache):
    i = np.arange(n_q)[:, None]
    pos = np.arange(n_cache + n_q)[None, :]
    dist = n_cache + i - pos
    mult = np.zeros(dist.shape, np.float32)
    for window, dil in DILATED_PAIRS:
        mult += (dist >= 0) & (dist <= window) & (dist % dil == 0)
    return mult


def _sample_unit(ck_ref, cv_ref, q_ref, kn_ref, vn_ref, sg_ref, mc_ref, mn_ref,
                 nk_ref, att_ref, *, n_heads, head_dim, ts, n_cache, n_pairs):
    d_att = n_heads * head_dim
    rows = n_heads * ts
    head_of_lane = lax.broadcasted_iota(jnp.int32, (rows, d_att), 1) // head_dim
    head_of_row = lax.broadcasted_iota(jnp.int32, (rows, d_att), 0) // ts
    own = head_of_lane == head_of_row

    q = q_ref[0]
    qx = jnp.where(own, jnp.concatenate([q] * n_heads, axis=0), 0.0).astype(jnp.bfloat16)
    kct = ck_ref[0]
    vct = cv_ref[0]
    knt = _new_rows_t(kn_ref[0])
    vnt = _new_rows_t(vn_ref[0])
    nt = (((1,), (1,)), ((), ()))
    s_c = jnp.dot(qx, kct.astype(jnp.bfloat16), preferred_element_type=jnp.float32)
    s_n = jnp.dot(qx, knt.astype(jnp.bfloat16), preferred_element_type=jnp.float32)
    mc = mc_ref[...]
    mn = mn_ref[...]
    s_c = jnp.where(mc > 0, s_c, NEG_BIG)
    s_n = jnp.where(mn > 0, s_n, NEG_BIG)
    m = jnp.maximum(jnp.max(s_c, axis=-1, keepdims=True), jnp.max(s_n, axis=-1, keepdims=True))
    e_c = jnp.exp2(s_c - m) * mc
    e_n = jnp.exp2(s_n - m) * mn
    l = jnp.sum(e_c, axis=-1, keepdims=True) + jnp.sum(e_n, axis=-1, keepdims=True)
    ox = (lax.dot_general(e_c.astype(jnp.bfloat16), vct.astype(jnp.bfloat16), nt,
                          preferred_element_type=jnp.float32)
          + lax.dot_general(e_n.astype(jnp.bfloat16), vnt.astype(jnp.bfloat16), nt,
                            preferred_element_type=jnp.float32))
    ox = jnp.where(own, ox * (1.0 / l), 0.0)
    o = ox[0:ts]
    for h in range(1, n_heads):
        o = o + ox[h * ts:(h + 1) * ts]
    att = o * sg_ref[0]
    for p in range(n_pairs):
        att_ref[0, p] = att[:, p * LANES:(p + 1) * LANES]

    _shift_cache(kct, knt, ts, nk_ref)


def _split_streams(dils, seq, n_parts):
    items = sorted(((seq // d // Q_BLOCK, i, r) for i, d in enumerate(dils) for r in range(d) if i > 0),
                   reverse=True)
    parts, loads = [[] for _ in range(n_parts)], [0] * n_parts
    loads[-1] = seq // dils[0] // Q_BLOCK
    for cost, i, r in items:
        j = loads.index(min(loads))
        parts[j].append((i, r))
        loads[j] += cost
    parts[-1].append((0, 0))
    return parts


def _attention_kernel(*refs, seq, dils, head_dim, parts, sample_cfg):
    n_br = len(dils)
    qkv = refs[:n_br]
    sg_ref = refs[n_br]
    sample_in = refs[n_br + 1:n_br + 9]
    att_ref = refs[n_br + 9]
    sample_out = refs[n_br + 10:n_br + 12]
    scr = refs[n_br + 12:]
    a_scr, m_scr, l_scr = scr[0::3], scr[1::3], scr[2::3]
    part = pl.program_id(2)

    lane = lax.broadcasted_iota(jnp.int32, (Q_BLOCK, LANES), 1)
    lo = lane < head_dim
    row = lax.broadcasted_iota(jnp.int32, (Q_BLOCK, 2 * Q_BLOCK), 0)
    col = lax.broadcasted_iota(jnp.int32, (Q_BLOCK, 2 * Q_BLOCK), 1)
    band = jnp.where((col >= row) & (col <= row + Q_BLOCK), 0.0, NEG_BIG).astype(jnp.float32)
    row1 = lax.broadcasted_iota(jnp.int32, (Q_BLOCK, Q_BLOCK), 0)
    col1 = lax.broadcasted_iota(jnp.int32, (Q_BLOCK, Q_BLOCK), 1)
    causal = jnp.where(col1 <= row1, 0.0, NEG_BIG).astype(jnp.float32)

    def merge(row0):
        rows = slice(row0, row0 + Q_BLOCK)
        ms = [m_ref[rows, :] for m_ref in m_scr]
        top = functools.reduce(jnp.maximum, ms)
        ws = [jnp.exp2(m - top) for m in ms]
        num = functools.reduce(lambda a, b: a + b, [w * a_ref[rows, :] for w, a_ref in zip(ws, a_scr)])
        den = functools.reduce(lambda a, b: a + b, [w * l_ref[rows, :] for w, l_ref in zip(ws, l_scr)])
        att = num * (1.0 / den) * sg_ref[rows, :].astype(jnp.float32)
        att_ref[rows, :] = att.astype(att_ref.dtype)

    def run_blocks(items, row_exact):
        blocks = [(i, r, qb) for i, r in items for qb in range(seq // dils[i] // Q_BLOCK)]
        l_min = None
        for g in range(0, len(blocks), BLOCK_GROUP):
            group = blocks[g:g + BLOCK_GROUP]
            keys = [slice(max(qb - 1, 0) * Q_BLOCK, (qb + 1) * Q_BLOCK) for _, _, qb in group]
            scores = [_block_scores(qkv[i][0, r, qb * Q_BLOCK:(qb + 1) * Q_BLOCK, :], qkv[i][1, r, ks, :], lo)
                      for (i, r, qb), ks in zip(group, keys)]
            probs = [_block_probs(s, causal if qb == 0 else band, row_exact)
                     for s, (_, _, qb) in zip(scores, group)]
            for (i, r, qb), ks, (ps, ms) in zip(group, keys, probs):
                vals = _block_values(ps, ms, qkv[i][2, r, ks, :], lo)
                d, row0 = dils[i], qb * Q_BLOCK
                idx = pl.ds(row0, Q_BLOCK) if d == 1 else pl.ds(r + d * row0, Q_BLOCK, stride=d)
                for ref, val in zip((a_scr[i], m_scr[i], l_scr[i]), vals):
                    ref[idx, :] = val
                l_min = vals[2] if l_min is None else jnp.minimum(l_min, vals[2])
            for i, _, qb in group:
                if i == 0:
                    merge(qb * Q_BLOCK)
        return l_min

    for idx, items in enumerate(parts):
        @pl.when(part == idx)
        def _(items=items):
            _sample_unit(*sample_in, *sample_out, **sample_cfg)
            l_min = run_blocks(items, False)

            @pl.when(jnp.min(l_min) < L_FLOOR)
            def _():
                run_blocks(items, True)


def _attention(qkv_st, sg_pm, cache_kt, cache_vt, q_s, kn_s, vn_s, sg_s, *, n_heads, head_dim):
    b, n_pairs, t, _ = sg_pm.shape
    nb, d_att, n_cache = cache_kt.shape
    ts = q_s.shape[1]
    dils = _dilations()
    for w, d in DILATED_PAIRS:
        assert w == d * Q_BLOCK and t % (d * Q_BLOCK) == 0
    n_parts = ATTN_PARTS
    steps = b * n_pairs * n_parts
    assert steps % nb == 0 and n_heads % (steps // nb) == 0
    groups = steps // nb
    heads_u = n_heads // groups
    d_u = heads_u * head_dim
    assert d_u % LANES == 0
    pairs_u = d_u // LANES
    rows_u = heads_u * ts

    def unit(i, j, k):
        u = (i * n_pairs + j) * n_parts + k
        return u // groups, u % groups

    ins, specs = [], []
    for a, d in zip(qkv_st, dils):
        assert a.shape == (b, n_pairs, 3, d, t // d, LANES)
        ins.append(a)
        specs.append(pl.BlockSpec((None, None, 3, d, t // d, LANES), lambda i, j, k: (i, j, 0, 0, 0, 0)))
    pair_spec = pl.BlockSpec((None, None, t, LANES), lambda i, j, k: (i, j, 0, 0))
    cache_spec = pl.BlockSpec((1, d_u, n_cache), lambda i, j, k: (unit(i, j, k)[0], unit(i, j, k)[1], 0))
    row_spec = pl.BlockSpec((1, ts, d_u), lambda i, j, k: (unit(i, j, k)[0], 0, unit(i, j, k)[1]))
    mult = np.tile(_branch_multiplicity(ts, n_cache), (heads_u, 1))
    mult_c = jnp.asarray(mult[:, :n_cache])
    mult_n = jnp.asarray(np.pad(mult[:, n_cache:], ((0, 0), (LANES - ts, 0))))
    const = lambda shape: pl.BlockSpec(shape, lambda i, j, k: (0, 0))
    sample_cfg = dict(n_heads=heads_u, head_dim=head_dim, ts=ts, n_cache=n_cache, n_pairs=pairs_u)
    kern = functools.partial(_attention_kernel, seq=t, dils=dils, head_dim=head_dim,
                             parts=_split_streams(dils, t, n_parts), sample_cfg=sample_cfg)
    vmem = (2 * (3 * len(dils) + 2) * t * LANES * 2 + 3 * len(dils) * t * LANES * 4
            + 14 * d_u * n_cache * 4 + 8 * rows_u * n_cache * 4 + 12 * MIB)
    return pl.pallas_call(
        kern,
        grid=(b, n_pairs, n_parts),
        in_specs=specs + [pair_spec, cache_spec, cache_spec, row_spec, row_spec, row_spec, row_spec,
                          const((rows_u, n_cache)), const((rows_u, LANES))],
        out_specs=[pair_spec, cache_spec,
                   pl.BlockSpec((1, pairs_u, ts, LANES),
                                lambda i, j, k: (0, unit(i, j, k)[1], unit(i, j, k)[0], 0))],
        out_shape=[jax.ShapeDtypeStruct((b, n_pairs, t, LANES), jnp.bfloat16),
                   jax.ShapeDtypeStruct((nb, d_att, n_cache), jnp.float32),
                   jax.ShapeDtypeStruct((1, d_att // LANES, nb * ts, LANES), jnp.float32)],
        scratch_shapes=[pltpu.VMEM((t, LANES), jnp.float32) for _ in range(3 * len(dils))],
        compiler_params=pltpu.CompilerParams(
            dimension_semantics=("parallel", "parallel", "arbitrary"), vmem_limit_bytes=int(vmem)),
        name="attention",
    )(*ins, sg_pm, cache_kt, cache_vt, q_s, kn_s, vn_s, sg_s, mult_c, mult_n)


def _outproj_kernel(att_ref, z_ref, x_ref, w_ref, g_ref, b_ref, y_ref, w_bf, *, n_pairs, n_chunks):
    @pl.when((pl.program_id(0) == 0) & (pl.program_id(1) == 0))
    def _():
        _cast_rows(w_ref, w_bf)

    tm = x_ref.shape[1]
    rows = tm // n_chunks
    for c in range(n_chunks):
        sl = slice(c * rows, (c + 1) * rows)
        parts = [att_ref[0, p, sl, :].astype(jnp.bfloat16) for p in range(n_pairs)]
        mix = jnp.concatenate(parts + [z_ref[0, sl, :].astype(jnp.bfloat16)], axis=-1)
        out = jnp.dot(mix, w_bf[...], preferred_element_type=jnp.float32)
        h = DEEPNORM_ALPHA * x_ref[0, sl, :] + out
        mu = jnp.mean(h, axis=-1, keepdims=True)
        d = h - mu
        var = jnp.mean(d * d, axis=-1, keepdims=True)
        y_ref[0, sl, :] = d * lax.rsqrt(var + LN_EPS) * g_ref[...] + b_ref[...]


def _outproj(att_pm, z, x, w_out, ln_g, ln_b, *, tm):
    b, t, d = x.shape
    n_pairs = att_pm.shape[1]
    d_conv = z.shape[-1]
    d_mix = w_out.shape[0]
    vmem = (d_mix * d * (4 + 2) + 4 * tm * d * 4 + 2 * tm * n_pairs * LANES * att_pm.dtype.itemsize
            + 2 * tm * d_conv * 2 + 6 * tm * d * 4)
    return pl.pallas_call(
        functools.partial(_outproj_kernel, n_pairs=n_pairs, n_chunks=max(tm // OUTPROJ_CHUNK_ROWS, 1)),
        grid=(b, t // tm),
        in_specs=[
            pl.BlockSpec((1, n_pairs, tm, LANES), lambda i, j: (i, 0, j, 0)),
            pl.BlockSpec((1, tm, d_conv), lambda i, j: (i, j, 0)),
            pl.BlockSpec((1, tm, d), lambda i, j: (i, j, 0)),
            pl.BlockSpec((d_mix, d), lambda i, j: (0, 0), pipeline_mode=pl.Buffered(1)),
            pl.BlockSpec((1, d), lambda i, j: (0, 0)),
            pl.BlockSpec((1, d), lambda i, j: (0, 0)),
        ],
        out_specs=pl.BlockSpec((1, tm, d), lambda i, j: (i, j, 0)),
        out_shape=jax.ShapeDtypeStruct((b, t, d), jnp.float32),
        scratch_shapes=[pltpu.VMEM((d_mix, d), jnp.bfloat16)],
        compiler_params=pltpu.CompilerParams(
            dimension_semantics=("arbitrary", "arbitrary"), vmem_limit_bytes=int(vmem)),
        name="outproj_layernorm",
    )(att_pm, z, x, w_out, ln_g.reshape(1, d), ln_b.reshape(1, d))


def _sample_inproj_kernel(x_ref, w_ref, cw_ref, st_ref,
                          q_ref, k_ref, v_ref, sg_ref, z_ref, nc_ref, u_scr,
                          *, nb, ts, d_att, d_conv, q_scale):
    x = x_ref[...].astype(jnp.bfloat16)

    def proj(c0, width):
        return jnp.dot(x, w_ref[:, c0:c0 + width], preferred_element_type=jnp.float32)

    q_ref[...] = proj(0, d_att) * q_scale
    k_ref[...] = proj(d_att, d_att)
    v_ref[...] = proj(2 * d_att, d_att)
    sg_ref[...] = _silu(proj(3 * d_att, d_att))
    c0 = 4 * d_att
    gb = proj(c0, d_conv)
    gc = proj(c0 + d_conv, d_conv)
    hh = proj(c0 + 2 * d_conv, d_conv)
    g_conv = proj(c0 + 3 * d_conv, d_conv)
    u = (gc * hh).reshape(nb, ts, d_conv)
    u_scr[:, SUBLANES - 2:SUBLANES, :] = st_ref[...]
    u_scr[:, SUBLANES:SUBLANES + ts, :] = u
    um2 = u_scr[:, SUBLANES - 2:SUBLANES - 2 + ts, :]
    um1 = u_scr[:, SUBLANES - 1:SUBLANES - 1 + ts, :]
    cw = cw_ref[...]
    y_conv = cw[0:1, :][None] * um2 + cw[1:2, :][None] * um1 + cw[2:3, :][None] * u
    z = gb * y_conv.reshape(nb * ts, d_conv) * _silu(g_conv)
    z_ref[...] = z.astype(jnp.bfloat16)
    nc_ref[...] = u_scr[:, ts + SUBLANES - 2:ts + SUBLANES, :]


def _sample_inproj(x2, w_in_bf, conv_w, state, *, nb, ts, d_att, d_conv, q_scale):
    rows, d = x2.shape
    f32 = lambda c: jax.ShapeDtypeStruct((rows, c), jnp.float32)
    kern = functools.partial(_sample_inproj_kernel, nb=nb, ts=ts, d_att=d_att, d_conv=d_conv, q_scale=q_scale)
    vmem = d * w_in_bf.shape[1] * 2 + 16 * rows * d * 4 + 8 * MIB
    return pl.pallas_call(
        kern,
        out_shape=[f32(d_att), f32(d_att), f32(d_att), f32(d_att),
                   jax.ShapeDtypeStruct((rows, d_conv), jnp.bfloat16),
                   jax.ShapeDtypeStruct((nb, 2, d_conv), jnp.float32)],
        scratch_shapes=[pltpu.VMEM((nb, ts + SUBLANES, d_conv), jnp.float32)],
        compiler_params=pltpu.CompilerParams(vmem_limit_bytes=int(vmem)),
        name="sample_inproj",
    )(x2, w_in_bf, conv_w, state)


def kernel(x_prompt, x_sample, cache_k, cache_v, state_conv, w_in, conv_w, w_out, ln_g, ln_b):
    b, t, d = x_prompt.shape
    nb, ts, _ = x_sample.shape
    _, n_cache, n_heads, head_dim = cache_k.shape
    d_att = n_heads * head_dim
    d_conv = conv_w.shape[1]
    assert w_in.shape == (d, 4 * d_att + 4 * d_conv) and w_out.shape == (d_att + d_conv, d)
    assert d_att % LANES == 0 and LANES == 2 * head_dim and conv_w.shape[0] == 3
    max_window = max(w for w, _ in DILATED_PAIRS)
    assert n_cache == max_window and t <= max_window
    q_scale = head_dim ** -0.5 * math.log2(math.e)

    def to_time_minor(a):
        return jnp.transpose(a, (0, 2, 3, 1)).reshape(a.shape[0], d_att, a.shape[1])

    def from_time_minor(a):
        return jnp.transpose(a.reshape(a.shape[0], n_heads, head_dim, a.shape[2]), (0, 3, 1, 2))

    w_in_bf = w_in.astype(jnp.bfloat16)
    q_s, k_s, v_s, sg_s, z_s, new_conv_sample = _sample_inproj(
        x_sample.reshape(nb * ts, d), w_in_bf, conv_w, state_conv, nb=nb, ts=ts, d_att=d_att, d_conv=d_conv,
        q_scale=q_scale)
    three = lambda a: a.reshape(nb, ts, d_att)
    cache_kt, cache_vt = to_time_minor(cache_k), to_time_minor(cache_v)

    conv0 = jnp.zeros((b, 2, d_conv), jnp.float32)
    outs = _inproj(x_prompt, w_in_bf, conv_w, conv0, cache_vt, three(v_s),
                   tm=512, d_att=d_att, d_conv=d_conv, q_scale=q_scale)
    n_d = len(DILATED_PAIRS)
    qkv_st = outs[:n_d]
    kt_new, vt_new, sg_pm, z, new_conv_prompt, new_vt_s = outs[n_d:]

    att_pm, new_kt_s, att_s = _attention(
        qkv_st, sg_pm, cache_kt, cache_vt,
        three(q_s), three(k_s), three(v_s), three(sg_s), n_heads=n_heads, head_dim=head_dim)

    y_prompt = _outproj(att_pm, z, x_prompt, w_out, ln_g, ln_b, tm=2048)
    y_sample = _outproj(att_s, z_s.reshape(1, nb * ts, d_conv), x_sample.reshape(1, nb * ts, d),
                        w_out, ln_g, ln_b, tm=nb * ts)

    return (y_prompt, y_sample.reshape(nb, ts, d),
            from_time_minor(kt_new), from_time_minor(vt_new), new_conv_prompt,
            from_time_minor(new_kt_s), from_time_minor(new_vt_s), new_conv_sample)
```

```python
import functools
import math

import numpy as np
import jax
import jax.numpy as jnp
from jax import lax
from jax.experimental import pallas as pl
from jax.experimental.pallas import tpu as pltpu

DILATED_PAIRS = ((128, 1), (512, 4), (2048, 16))
LN_EPS = 1e-5
DEPTH = 1
DEEPNORM_ALPHA = (2.0 * DEPTH) ** 0.25

LANES = 128
SUBLANES = 8
Q_BLOCK = 128
NEG_BIG = -1e30
ATTN_PARTS = 1
OUTPROJ_CHUNK_ROWS = 256
STREAM_BUFFERS = 3
CAST_ROWS = 128
BLOCK_GROUP = 8
L_FLOOR = 2.0 ** -100

MIB = 1024 * 1024


def _dilations():
    dils = tuple(sorted(d for _, d in DILATED_PAIRS))
    assert dils[0] == 1 and all(b % a == 0 for a, b in zip(dils, dils[1:]))
    return dils


def _silu(x):
    return x * (1.0 / (1.0 + jnp.exp(-x)))


def _new_rows_t(rows):
    ts, d = rows.shape
    pad = jnp.zeros((LANES - ts, d), rows.dtype)
    return jnp.concatenate([pad, rows], axis=0).T


def _shift_cache(old, new_t, ts, dst_ref):
    d, n_cache = old.shape
    main = n_cache - LANES
    is_old = lax.broadcasted_iota(jnp.int32, (d, LANES), 1) < LANES - ts
    rolled = pltpu.roll(old, n_cache - ts, axis=1)
    dst_ref[0, :, 0:main] = rolled[:, 0:main]
    dst_ref[0, :, main:n_cache] = jnp.where(is_old, rolled[:, main:n_cache], new_t)


def _cast_rows(src_ref, dst_ref):
    n = src_ref.shape[0] // CAST_ROWS

    def body(i, carry):
        rows = pl.ds(pl.multiple_of(i * CAST_ROWS, CAST_ROWS), CAST_ROWS)
        dst_ref[rows, :] = src_ref[rows, :].astype(dst_ref.dtype)
        return carry

    lax.fori_loop(0, n, body, 0)


def _emit_streams(val, refs, which, dils, scr, tm):
    n_pairs = val.shape[1] // LANES
    for p in range(n_pairs):
        col = val[:, p * LANES:(p + 1) * LANES]
        refs[0][0, p, which, 0] = col.astype(jnp.bfloat16)
        if len(dils) > 1:
            scr[0][...] = col
        for lvl in range(1, len(dils)):
            d_prev = dils[lvl - 1]
            f = dils[lvl] // d_prev
            n_prev = tm // d_prev
            n = n_prev // f
            src, dst = scr[(lvl - 1) % 2], scr[lvl % 2]
            for s in range(d_prev):
                for r2 in range(f):
                    r = s + d_prev * r2
                    rows = src[pl.ds(s * n_prev + r2, n, stride=f), :]
                    refs[lvl][0, p, which, r] = rows.astype(jnp.bfloat16)
                    if lvl + 1 < len(dils):
                        dst[r * n:(r + 1) * n, :] = rows


def _inproj_kernel(*refs, tm, d_att, d_conv, n_pairs, q_scale, dils):
    x_ref, w_bf, cw_ref, hist_ref, cv_ref, vn_ref = refs[:6]
    n_d = len(dils)
    st_refs = refs[6:6 + n_d]
    kt_ref, vt_ref, sg_ref, z_ref, nc_ref, nv_ref, u_scr, s_a, s_b, cv_buf, cv_sem = refs[6 + n_d:]

    @pl.when(pl.program_id(1) == 0)
    def _():
        u_scr[SUBLANES - 2:SUBLANES, :] = hist_ref[0]

    step = pl.program_id(0) * pl.num_programs(1) + pl.program_id(1)
    n_steps = pl.num_programs(0) * pl.num_programs(1)

    def cache_copy(s):
        slot = s % STREAM_BUFFERS
        return pltpu.make_async_copy(cv_ref.at[s], cv_buf.at[slot], cv_sem.at[slot])

    @pl.when(step == 0)
    def _():
        for s in range(STREAM_BUFFERS - 1):
            cache_copy(s).start()

    @pl.when(step + (STREAM_BUFFERS - 1) < n_steps)
    def _():
        cache_copy(step + (STREAM_BUFFERS - 1)).start()

    cache_copy(step).wait()

    _shift_cache(cv_buf[step % STREAM_BUFFERS], _new_rows_t(vn_ref[0]), vn_ref.shape[1], nv_ref)

    x = x_ref[0].astype(jnp.bfloat16)

    def proj(c0, width):
        return jnp.dot(x, w_bf[:, c0:c0 + width], preferred_element_type=jnp.float32)

    c0 = 4 * d_att
    gc = proj(c0 + d_conv, d_conv)
    hh = proj(c0 + 2 * d_conv, d_conv)
    u = gc * hh
    u_scr[SUBLANES:SUBLANES + tm, :] = u
    um2 = u_scr[SUBLANES - 2:SUBLANES - 2 + tm, :]
    um1 = u_scr[SUBLANES - 1:SUBLANES - 1 + tm, :]
    y_conv = cw_ref[0:1, :] * um2 + cw_ref[1:2, :] * um1 + cw_ref[2:3, :] * u
    tail = u[tm - 2:tm, :]
    u_scr[SUBLANES - 2:SUBLANES, :] = tail
    nc_ref[0] = tail
    gb = proj(c0, d_conv)
    g_conv = proj(c0 + 3 * d_conv, d_conv)
    z_ref[0] = (gb * y_conv * _silu(g_conv)).astype(jnp.bfloat16)

    sg = _silu(proj(3 * d_att, d_att)).astype(jnp.bfloat16)
    for p in range(n_pairs):
        sg_ref[0, p] = sg[:, p * LANES:(p + 1) * LANES]
    k = proj(d_att, d_att)
    kt_ref[0] = k.T
    _emit_streams(k, st_refs, 1, dils, (s_a, s_b), tm)
    v = proj(2 * d_att, d_att)
    vt_ref[0] = v.T
    _emit_streams(v, st_refs, 2, dils, (s_a, s_b), tm)
    q = proj(0, d_att) * q_scale
    _emit_streams(q, st_refs, 0, dils, (s_a, s_b), tm)


def _inproj(x, w_in_bf, conv_w, hist, cache_vt, vn_s, *, tm, d_att, d_conv, q_scale):
    b, t, d = x.shape
    n_pairs = d_att // LANES
    n_out = w_in_bf.shape[1]
    dils = _dilations()
    assert tm % (dils[-1] * 2 * SUBLANES) == 0
    kern = functools.partial(_inproj_kernel, tm=tm, d_att=d_att, d_conv=d_conv,
                             n_pairs=n_pairs, q_scale=q_scale, dils=dils)
    st_shapes = [jax.ShapeDtypeStruct((b, n_pairs, 3, dd, t // dd, LANES), jnp.bfloat16) for dd in dils]
    st_specs = [pl.BlockSpec((1, n_pairs, 3, dd, tm // dd, LANES), lambda i, j: (i, 0, 0, 0, j, 0))
                for dd in dils]
    pm = jax.ShapeDtypeStruct((b, n_pairs, t, LANES), jnp.bfloat16)
    pm_spec = pl.BlockSpec((1, n_pairs, tm, LANES), lambda i, j: (i, 0, j, 0))
    row_spec = lambda c: pl.BlockSpec((1, tm, c), lambda i, j: (i, j, 0))
    col_spec = pl.BlockSpec((1, d_att, tm), lambda i, j: (i, 0, j))
    nb, _, n_cache = cache_vt.shape
    ts = vn_s.shape[1]
    n_t = t // tm
    assert nb == b * n_t
    cache_spec = pl.BlockSpec((1, d_att, n_cache), lambda i, j: (i * n_t + j, 0, 0))
    vmem = (d * n_out * 2 + 2 * tm * d * 4
            + 2 * tm * ((3 * len(dils) + 1) * d_att * 2 + 2 * d_att * 4 + d_conv * 2)
            + (tm + SUBLANES) * d_conv * 4 + 2 * tm * LANES * 4 + 12 * tm * d_att * 4
            + (3 + STREAM_BUFFERS) * d_att * n_cache * 4)
    return pl.pallas_call(
        kern,
        grid=(b, n_t),
        in_specs=[
            row_spec(d),
            pl.BlockSpec((d, n_out), lambda i, j: (0, 0), pipeline_mode=pl.Buffered(1)),
            pl.BlockSpec(conv_w.shape, lambda i, j: (0, 0)),
            pl.BlockSpec((1, 2, d_conv), lambda i, j: (i, 0, 0)),
            pl.BlockSpec(memory_space=pl.ANY),
            pl.BlockSpec((1, ts, d_att), lambda i, j: (i * n_t + j, 0, 0)),
        ],
        out_specs=st_specs + [col_spec, col_spec, pm_spec,
                              row_spec(d_conv), pl.BlockSpec((1, 2, d_conv), lambda i, j: (i, 0, 0)),
                              cache_spec],
        out_shape=st_shapes + [
            jax.ShapeDtypeStruct((b, d_att, t), jnp.float32),
            jax.ShapeDtypeStruct((b, d_att, t), jnp.float32),
            pm,
            jax.ShapeDtypeStruct((b, t, d_conv), jnp.bfloat16),
            jax.ShapeDtypeStruct((b, 2, d_conv), jnp.float32),
            jax.ShapeDtypeStruct((nb, d_att, n_cache), jnp.float32)],
        scratch_shapes=[pltpu.VMEM((tm + SUBLANES, d_conv), jnp.float32),
                        pltpu.VMEM((tm, LANES), jnp.float32),
                        pltpu.VMEM((tm, LANES), jnp.float32),
                        pltpu.VMEM((STREAM_BUFFERS, d_att, n_cache), jnp.float32),
                        pltpu.SemaphoreType.DMA((STREAM_BUFFERS,))],
        compiler_params=pltpu.CompilerParams(
            dimension_semantics=("arbitrary", "arbitrary"), vmem_limit_bytes=int(vmem)),
        name="prompt_inproj",
    )(x, w_in_bf, conv_w, hist, cache_vt, vn_s)


def _block_scores(q, kwin, lo):
    zero = jnp.zeros_like(q)
    q2 = jnp.concatenate([jnp.where(lo, q, zero), jnp.where(lo, zero, q)], axis=0)
    return lax.dot_general(q2, kwin, (((1,), (1,)), ((), ())), preferred_element_type=jnp.float32)


def _block_probs(s, bias, row_exact):
    ps, ms = [], []
    for h in range(2):
        sh = s[h * Q_BLOCK:(h + 1) * Q_BLOCK] + bias
        if row_exact:
            m = jnp.max(sh, axis=-1, keepdims=True)
        else:
            m = jnp.max(jnp.max(sh, axis=0, keepdims=True), axis=1, keepdims=True)
        ps.append(jnp.exp2(sh - m).astype(jnp.bfloat16))
        ms.append(jnp.broadcast_to(m, (Q_BLOCK, 1)))
    return ps, ms


def _block_values(ps, ms, vwin, lo):
    lo_v = lax.broadcasted_iota(jnp.int32, vwin.shape, 1) < LANES // 2
    one = jnp.ones_like(vwin)
    pv0 = jnp.dot(ps[0], jnp.where(lo_v, vwin, one), preferred_element_type=jnp.float32)
    pv1 = jnp.dot(ps[1], jnp.where(lo_v, one, vwin), preferred_element_type=jnp.float32)
    acc = jnp.where(lo, pv0, pv1)
    l = pltpu.roll(jnp.where(lo, pv1, pv0), LANES // 2, axis=1)
    return acc, jnp.where(lo, ms[0], ms[1]), l


def _branch_multiplicity(n_q, n_cache):
    i = np.arange(n_q)[:, None]
    pos = np.arange(n_cache + n_q)[None, :]
    dist = n_cache + i - pos
    mult = np.zeros(dist.shape, np.float32)
    for window, dil in DILATED_PAIRS:
        mult += (dist >= 0) & (dist <= window) & (dist % dil == 0)
    return mult


def _sample_unit(ck_ref, cv_ref, q_ref, kn_ref, vn_ref, sg_ref, mc_ref, mn_ref,
                 nk_ref, att_ref, *, n_heads, head_dim, ts, n_cache, n_pairs):
    d_att = n_heads * head_dim
    rows = n_heads * ts
    head_of_lane = lax.broadcasted_iota(jnp.int32, (rows, d_att), 1) // head_dim
    head_of_row = lax.broadcasted_iota(jnp.int32, (rows, d_att), 0) // ts
    own = head_of_lane == head_of_row

    q = q_ref[0]
    qx = jnp.where(own, jnp.concatenate([q] * n_heads, axis=0), 0.0).astype(jnp.bfloat16)
    kct = ck_ref[0]
    vct = cv_ref[0]
    knt = _new_rows_t(kn_ref[0])
    vnt = _new_rows_t(vn_ref[0])
    nt = (((1,), (1,)), ((), ()))
    s_c = jnp.dot(qx, kct.astype(jnp.bfloat16), preferred_element_type=jnp.float32)
    s_n = jnp.dot(qx, knt.astype(jnp.bfloat16), preferred_element_type=jnp.float32)
    mc = mc_ref[...]
    mn = mn_ref[...]
    s_c = jnp.where(mc > 0, s_c, NEG_BIG)
    s_n = jnp.where(mn > 0, s_n, NEG_BIG)
    m = jnp.maximum(jnp.max(s_c, axis=-1, keepdims=True), jnp.max(s_n, axis=-1, keepdims=True))
    e_c = jnp.exp2(s_c - m) * mc
    e_n = jnp.exp2(s_n - m) * mn
    l = jnp.sum(e_c, axis=-1, keepdims=True) + jnp.sum(e_n, axis=-1, keepdims=True)
    ox = (lax.dot_general(e_c.astype(jnp.bfloat16), vct.astype(jnp.bfloat16), nt,
                          preferred_element_type=jnp.float32)
          + lax.dot_general(e_n.astype(jnp.bfloat16), vnt.astype(jnp.bfloat16), nt,
                            preferred_element_type=jnp.float32))
    ox = jnp.where(own, ox * (1.0 / l), 0.0)
    o = ox[0:ts]
    for h in range(1, n_heads):
        o = o + ox[h * ts:(h + 1) * ts]
    att = o * sg_ref[0]
    for p in range(n_pairs):
        att_ref[0, p] = att[:, p * LANES:(p + 1) * LANES]

    _shift_cache(kct, knt, ts, nk_ref)


def _split_streams(dils, seq, n_parts):
    items = sorted(((seq // d // Q_BLOCK, i, r) for i, d in enumerate(dils) for r in range(d) if i > 0),
                   reverse=True)
    parts, loads = [[] for _ in range(n_parts)], [0] * n_parts
    loads[-1] = seq // dils[0] // Q_BLOCK
    for cost, i, r in items:
        j = loads.index(min(loads))
        parts[j].append((i, r))
        loads[j] += cost
    parts[-1].append((0, 0))
    return parts


def _attention_kernel(*refs, seq, dils, head_dim, parts, sample_cfg):
    n_br = len(dils)
    qkv = refs[:n_br]
    sg_ref = refs[n_br]
    sample_in = refs[n_br + 1:n_br + 9]
    att_ref = refs[n_br + 9]
    sample_out = refs[n_br + 10:n_br + 12]
    scr = refs[n_br + 12:]
    a_scr, m_scr, l_scr = scr[0::3], scr[1::3], scr[2::3]
    part = pl.program_id(2)

    lane = lax.broadcasted_iota(jnp.int32, (Q_BLOCK, LANES), 1)
    lo = lane < head_dim
    row = lax.broadcasted_iota(jnp.int32, (Q_BLOCK, 2 * Q_BLOCK), 0)
    col = lax.broadcasted_iota(jnp.int32, (Q_BLOCK, 2 * Q_BLOCK), 1)
    band = jnp.where((col >= row) & (col <= row + Q_BLOCK), 0.0, NEG_BIG).astype(jnp.float32)
    row1 = lax.broadcasted_iota(jnp.int32, (Q_BLOCK, Q_BLOCK), 0)
    col1 = lax.broadcasted_iota(jnp.int32, (Q_BLOCK, Q_BLOCK), 1)
    causal = jnp.where(col1 <= row1, 0.0, NEG_BIG).astype(jnp.float32)

    def merge(row0):
        rows = slice(row0, row0 + Q_BLOCK)
        ms = [m_ref[rows, :] for m_ref in m_scr]
        top = functools.reduce(jnp.maximum, ms)
        ws = [jnp.exp2(m - top) for m in ms]
        num = functools.reduce(lambda a, b: a + b, [w * a_ref[rows, :] for w, a_ref in zip(ws, a_scr)])
        den = functools.reduce(lambda a, b: a + b, [w * l_ref[rows, :] for w, l_ref in zip(ws, l_scr)])
        att = num * (1.0 / den) * sg_ref[rows, :].astype(jnp.float32)
        att_ref[rows, :] = att.astype(att_ref.dtype)

    def run_blocks(items, row_exact):
        blocks = [(i, r, qb) for i, r in items for qb in range(seq // dils[i] // Q_BLOCK)]
        l_min = None
        for g in range(0, len(blocks), BLOCK_GROUP):
            group = blocks[g:g + BLOCK_GROUP]
            keys = [slice(max(qb - 1, 0) * Q_BLOCK, (qb + 1) * Q_BLOCK) for _, _, qb in group]
            scores = [_block_scores(qkv[i][0, r, qb * Q_BLOCK:(qb + 1) * Q_BLOCK, :], qkv[i][1, r, ks, :], lo)
                      for (i, r, qb), ks in zip(group, keys)]
            probs = [_block_probs(s, causal if qb == 0 else band, row_exact)
                     for s, (_, _, qb) in zip(scores, group)]
            for (i, r, qb), ks, (ps, ms) in zip(group, keys, probs):
                vals = _block_values(ps, ms, qkv[i][2, r, ks, :], lo)
                d, row0 = dils[i], qb * Q_BLOCK
                idx = pl.ds(row0, Q_BLOCK) if d == 1 else pl.ds(r + d * row0, Q_BLOCK, stride=d)
                for ref, val in zip((a_scr[i], m_scr[i], l_scr[i]), vals):
                    ref[idx, :] = val
                l_min = vals[2] if l_min is None else jnp.minimum(l_min, vals[2])
            for i, _, qb in group:
                if i == 0:
                    merge(qb * Q_BLOCK)
        return l_min

    for idx, items in enumerate(parts):
        @pl.when(part == idx)
        def _(items=items):
            _sample_unit(*sample_in, *sample_out, **sample_cfg)
            l_min = run_blocks(items, False)

            @pl.when(jnp.min(l_min) < L_FLOOR)
            def _():
                run_blocks(items, True)


def _attention(qkv_st, sg_pm, cache_kt, cache_vt, q_s, kn_s, vn_s, sg_s, *, n_heads, head_dim):
    b, n_pairs, t, _ = sg_pm.shape
    nb, d_att, n_cache = cache_kt.shape
    ts = q_s.shape[1]
    dils = _dilations()
    for w, d in DILATED_PAIRS:
        assert w == d * Q_BLOCK and t % (d * Q_BLOCK) == 0
    n_parts = ATTN_PARTS
    steps = b * n_pairs * n_parts
    assert steps % nb == 0 and n_heads % (steps // nb) == 0
    groups = steps // nb
    heads_u = n_heads // groups
    d_u = heads_u * head_dim
    assert d_u % LANES == 0
    pairs_u = d_u // LANES
    rows_u = heads_u * ts

    def unit(i, j, k):
        u = (i * n_pairs + j) * n_parts + k
        return u // groups, u % groups

    ins, specs = [], []
    for a, d in zip(qkv_st, dils):
        assert a.shape == (b, n_pairs, 3, d, t // d, LANES)
        ins.append(a)
        specs.append(pl.BlockSpec((None, None, 3, d, t // d, LANES), lambda i, j, k: (i, j, 0, 0, 0, 0)))
    pair_spec = pl.BlockSpec((None, None, t, LANES), lambda i, j, k: (i, j, 0, 0))
    cache_spec = pl.BlockSpec((1, d_u, n_cache), lambda i, j, k: (unit(i, j, k)[0], unit(i, j, k)[1], 0))
    row_spec = pl.BlockSpec((1, ts, d_u), lambda i, j, k: (unit(i, j, k)[0], 0, unit(i, j, k)[1]))
    mult = np.tile(_branch_multiplicity(ts, n_cache), (heads_u, 1))
    mult_c = jnp.asarray(mult[:, :n_cache])
    mult_n = jnp.asarray(np.pad(mult[:, n_cache:], ((0, 0), (LANES - ts, 0))))
    const = lambda shape: pl.BlockSpec(shape, lambda i, j, k: (0, 0))
    sample_cfg = dict(n_heads=heads_u, head_dim=head_dim, ts=ts, n_cache=n_cache, n_pairs=pairs_u)
    kern = functools.partial(_attention_kernel, seq=t, dils=dils, head_dim=head_dim,
                             parts=_split_streams(dils, t, n_parts), sample_cfg=sample_cfg)
    vmem = (2 * (3 * len(dils) + 2) * t * LANES * 2 + 3 * len(dils) * t * LANES * 4
            + 14 * d_u * n_cache * 4 + 8 * rows_u * n_cache * 4 + 12 * MIB)
    return pl.pallas_call(
        kern,
        grid=(b, n_pairs, n_parts),
        in_specs=specs + [pair_spec, cache_spec, cache_spec, row_spec, row_spec, row_spec, row_spec,
                          const((rows_u, n_cache)), const((rows_u, LANES))],
        out_specs=[pair_spec, cache_spec,
                   pl.BlockSpec((1, pairs_u, ts, LANES),
                                lambda i, j, k: (0, unit(i, j, k)[1], unit(i, j, k)[0], 0))],
        out_shape=[jax.ShapeDtypeStruct((b, n_pairs, t, LANES), jnp.bfloat16),
                   jax.ShapeDtypeStruct((nb, d_att, n_cache), jnp.float32),
                   jax.ShapeDtypeStruct((1, d_att // LANES, nb * ts, LANES), jnp.float32)],
        scratch_shapes=[pltpu.VMEM((t, LANES), jnp.float32) for _ in range(3 * len(dils))],
        compiler_params=pltpu.CompilerParams(
            dimension_semantics=("parallel", "parallel", "arbitrary"), vmem_limit_bytes=int(vmem)),
        name="attention",
    )(*ins, sg_pm, cache_kt, cache_vt, q_s, kn_s, vn_s, sg_s, mult_c, mult_n)


def _outproj_kernel(att_ref, z_ref, x_ref, w_ref, g_ref, b_ref, y_ref, w_bf, *, n_pairs, n_chunks):
    @pl.when((pl.program_id(0) == 0) & (pl.program_id(1) == 0))
    def _():
        _cast_rows(w_ref, w_bf)

    tm = x_ref.shape[1]
    rows = tm // n_chunks
    for c in range(n_chunks):
        sl = slice(c * rows, (c + 1) * rows)
        parts = [att_ref[0, p, sl, :].astype(jnp.bfloat16) for p in range(n_pairs)]
        mix = jnp.concatenate(parts + [z_ref[0, sl, :].astype(jnp.bfloat16)], axis=-1)
        out = jnp.dot(mix, w_bf[...], preferred_element_type=jnp.float32)
        h = DEEPNORM_ALPHA * x_ref[0, sl, :] + out
        mu = jnp.mean(h, axis=-1, keepdims=True)
        d = h - mu
        var = jnp.mean(d * d, axis=-1, keepdims=True)
        y_ref[0, sl, :] = d * lax.rsqrt(var + LN_EPS) * g_ref[...] + b_ref[...]


def _outproj(att_pm, z, x, w_out, ln_g, ln_b, *, tm):
    b, t, d = x.shape
    n_pairs = att_pm.shape[1]
    d_conv = z.shape[-1]
    d_mix = w_out.shape[0]
    vmem = (d_mix * d * (4 + 2) + 4 * tm * d * 4 + 2 * tm * n_pairs * LANES * att_pm.dtype.itemsize
            + 2 * tm * d_conv * 2 + 6 * tm * d * 4)
    return pl.pallas_call(
        functools.partial(_outproj_kernel, n_pairs=n_pairs, n_chunks=max(tm // OUTPROJ_CHUNK_ROWS, 1)),
        grid=(b, t // tm),
        in_specs=[
            pl.BlockSpec((1, n_pairs, tm, LANES), lambda i, j: (i, 0, j, 0)),
            pl.BlockSpec((1, tm, d_conv), lambda i, j: (i, j, 0)),
            pl.BlockSpec((1, tm, d), lambda i, j: (i, j, 0)),
            pl.BlockSpec((d_mix, d), lambda i, j: (0, 0), pipeline_mode=pl.Buffered(1)),
            pl.BlockSpec((1, d), lambda i, j: (0, 0)),
            pl.BlockSpec((1, d), lambda i, j: (0, 0)),
        ],
        out_specs=pl.BlockSpec((1, tm, d), lambda i, j: (i, j, 0)),
        out_shape=jax.ShapeDtypeStruct((b, t, d), jnp.float32),
        scratch_shapes=[pltpu.VMEM((d_mix, d), jnp.bfloat16)],
        compiler_params=pltpu.CompilerParams(
            dimension_semantics=("arbitrary", "arbitrary"), vmem_limit_bytes=int(vmem)),
        name="outproj_layernorm",
    )(att_pm, z, x, w_out, ln_g.reshape(1, d), ln_b.reshape(1, d))


def _sample_inproj_kernel(x_ref, w_ref, cw_ref, st_ref,
                          q_ref, k_ref, v_ref, sg_ref, z_ref, nc_ref, u_scr,
                          *, nb, ts, d_att, d_conv, q_scale):
    x = x_ref[...].astype(jnp.bfloat16)

    def proj(c0, width):
        return jnp.dot(x, w_ref[:, c0:c0 + width], preferred_element_type=jnp.float32)

    q_ref[...] = proj(0, d_att) * q_scale
    k_ref[...] = proj(d_att, d_att)
    v_ref[...] = proj(2 * d_att, d_att)
    sg_ref[...] = _silu(proj(3 * d_att, d_att))
    c0 = 4 * d_att
    gb = proj(c0, d_conv)
    gc = proj(c0 + d_conv, d_conv)
    hh = proj(c0 + 2 * d_conv, d_conv)
    g_conv = proj(c0 + 3 * d_conv, d_conv)
    u = (gc * hh).reshape(nb, ts, d_conv)
    u_scr[:, SUBLANES - 2:SUBLANES, :] = st_ref[...]
    u_scr[:, SUBLANES:SUBLANES + ts, :] = u
    um2 = u_scr[:, SUBLANES - 2:SUBLANES - 2 + ts, :]
    um1 = u_scr[:, SUBLANES - 1:SUBLANES - 1 + ts, :]
    cw = cw_ref[...]
    y_conv = cw[0:1, :][None] * um2 + cw[1:2, :][None] * um1 + cw[2:3, :][None] * u
    z = gb * y_conv.reshape(nb * ts, d_conv) * _silu(g_conv)
    z_ref[...] = z.astype(jnp.bfloat16)
    nc_ref[...] = u_scr[:, ts + SUBLANES - 2:ts + SUBLANES, :]


def _sample_inproj(x2, w_in_bf, conv_w, state, *, nb, ts, d_att, d_conv, q_scale):
    rows, d = x2.shape
    f32 = lambda c: jax.ShapeDtypeStruct((rows, c), jnp.float32)
    kern = functools.partial(_sample_inproj_kernel, nb=nb, ts=ts, d_att=d_att, d_conv=d_conv, q_scale=q_scale)
    vmem = d * w_in_bf.shape[1] * 2 + 16 * rows * d * 4 + 8 * MIB
    return pl.pallas_call(
        kern,
        out_shape=[f32(d_att), f32(d_att), f32(d_att), f32(d_att),
                   jax.ShapeDtypeStruct((rows, d_conv), jnp.bfloat16),
                   jax.ShapeDtypeStruct((nb, 2, d_conv), jnp.float32)],
        scratch_shapes=[pltpu.VMEM((nb, ts + SUBLANES, d_conv), jnp.float32)],
        compiler_params=pltpu.CompilerParams(vmem_limit_bytes=int(vmem)),
        name="sample_inproj",
    )(x2, w_in_bf, conv_w, state)


def kernel(x_prompt, x_sample, cache_k, cache_v, state_conv, w_in, conv_w, w_out, ln_g, ln_b):
    b, t, d = x_prompt.shape
    nb, ts, _ = x_sample.shape
    _, n_cache, n_heads, head_dim = cache_k.shape
    d_att = n_heads * head_dim
    d_conv = conv_w.shape[1]
    assert w_in.shape == (d, 4 * d_att + 4 * d_conv) and w_out.shape == (d_att + d_conv, d)
    assert d_att % LANES == 0 and LANES == 2 * head_dim and conv_w.shape[0] == 3
    max_window = max(w for w, _ in DILATED_PAIRS)
    assert n_cache == max_window and t <= max_window
    q_scale = head_dim ** -0.5 * math.log2(math.e)

    def to_time_minor(a):
        return jnp.transpose(a, (0, 2, 3, 1)).reshape(a.shape[0], d_att, a.shape[1])

    def from_time_minor(a):
        return jnp.transpose(a.reshape(a.shape[0], n_heads, head_dim, a.shape[2]), (0, 3, 1, 2))

    w_in_bf = w_in.astype(jnp.bfloat16)
    q_s, k_s, v_s, sg_s, z_s, new_conv_sample = _sample_inproj(
        x_sample.reshape(nb * ts, d), w_in_bf, conv_w, state_conv, nb=nb, ts=ts, d_att=d_att, d_conv=d_conv,
        q_scale=q_scale)
    three = lambda a: a.reshape(nb, ts, d_att)
    cache_kt, cache_vt = to_time_minor(cache_k), to_time_minor(cache_v)

    conv0 = jnp.zeros((b, 2, d_conv), jnp.float32)
    outs = _inproj(x_prompt, w_in_bf, conv_w, conv0, cache_vt, three(v_s),
                   tm=512, d_att=d_att, d_conv=d_conv, q_scale=q_scale)
    n_d = len(DILATED_PAIRS)
    qkv_st = outs[:n_d]
    kt_new, vt_new, sg_pm, z, new_conv_prompt, new_vt_s = outs[n_d:]

    att_pm, new_kt_s, att_s = _attention(
        qkv_st, sg_pm, cache_kt, cache_vt,
        three(q_s), three(k_s), three(v_s), three(sg_s), n_heads=n_heads, head_dim=head_dim)

    y_prompt = _outproj(att_pm, z, x_prompt, w_out, ln_g, ln_b, tm=2048)
    y_sample = _outproj(att_s, z_s.reshape(1, nb * ts, d_conv), x_sample.reshape(1, nb * ts, d),
                        w_out, ln_g, ln_b, tm=nb * ts)

    return (y_prompt, y_sample.reshape(nb, ts, d),
            from_time_minor(kt_new), from_time_minor(vt_new), new_conv_prompt,
            from_time_minor(new_kt_s), from_time_minor(new_vt_s), new_conv_sample)
```
